```python
import numpy as np
import jax
import jax.numpy as jnp
from jax import lax

D_MODEL = 1024
BATCH = 1
SEQ = 16384
DEPTH = 4

GRID_W = 64
CTX_LEN = 256
N_MIXERS = 2
N_HEADS = 16
HEAD_DIM = D_MODEL // N_HEADS
D_FF = 2816
CONV_WIDTH = 31
WIN_R = 8
WIN_C = 16
N_MOD = 9
N_CONV_LAYERS = (DEPTH + 1) // 2
N_NA_LAYERS = DEPTH // 2
RMS_EPS = 1e-6
LN_EPS = 1e-5
MASK_VALUE = -1e30

kernel_name = 'hybrid_conformer_natten_dit_trunk'


def _rmsnorm(x, g):
    xf = x.astype(jnp.float32)
    y = xf * lax.rsqrt(jnp.mean(xf * xf, axis=-1, keepdims=True) + RMS_EPS)
    return y.astype(x.dtype) * g


def _layernorm(x, g, b):
    xf = x.astype(jnp.float32)
    mu = jnp.mean(xf, axis=-1, keepdims=True)
    var = jnp.mean(jnp.square(xf - mu), axis=-1, keepdims=True)
    return ((xf - mu) * lax.rsqrt(var + LN_EPS)).astype(x.dtype) * g + b


def _modulate(h, shift, scale):
    return h * (1 + scale) + shift


def _swiglu(h, wg, wu, wd):
    return (jax.nn.silu(h @ wg) * (h @ wu)) @ wd


def _ffn_half(s, mod, g, wg, wu, wd):
    shift, scale, gate = mod
    h = _modulate(_rmsnorm(s, g), shift, scale)
    return s + 0.5 * gate * _swiglu(h, wg, wu, wd)


def _conv_module(h, w1, b1, wdw, bdw, lng, lnb, w2, b2):
    a, gt = jnp.split(h @ w1 + b1, 2, axis=-1)
    u = a * jax.nn.sigmoid(gt)
    u = lax.conv_general_dilated(
        u, wdw[:, None, :], window_strides=(1,),
        padding=[(CONV_WIDTH // 2, CONV_WIDTH // 2)],
        dimension_numbers=('NWC', 'WIO', 'NWC'),
        feature_group_count=u.shape[-1]) + bdw
    u = jax.nn.silu(_layernorm(u, lng, lnb))
    return u @ w2 + b2


def _col_window_tables():
    ncb = GRID_W // WIN_C
    span = 2 * WIN_C
    n = np.arange(ncb)
    blk_start = np.clip(n * WIN_C - WIN_C // 2, 0, GRID_W - span)
    key_cols = blk_start[:, None] + np.arange(span)[None, :]
    q_cols = n[:, None] * WIN_C + np.arange(WIN_C)[None, :]
    q_start = np.clip(q_cols - WIN_C // 2, 0, GRID_W - WIN_C)
    kc3 = key_cols[:, None, :]
    valid = (kc3 >= q_start[:, :, None]) & (kc3 < q_start[:, :, None] + WIN_C)
    rel = np.clip(kc3 - q_cols[:, :, None], -(WIN_C - 1), WIN_C - 1) + WIN_C - 1
    return key_cols.astype(np.int32), valid, rel.astype(np.int32)


def _neighbourhood_attention(q, k, v, kc, vc, rpb):
    B, T, H, Dh = q.shape
    rows = T // GRID_W
    wr = min(WIN_R, rows)
    key_cols, valid, rel_idx = _col_window_tables()
    ncb, span = key_cols.shape
    key_cols_j = jnp.asarray(key_cols)
    rel_idx_j = jnp.asarray(rel_idx)
    valid_j = jnp.asarray(valid)[None, None, :, :, None, :]
    rpb32 = rpb.astype(jnp.float32)
    qg = (q * (Dh ** -0.5)).reshape(B, rows, ncb, WIN_C, H, Dh).transpose(1, 0, 2, 3, 4, 5)
    kg = k.reshape(B, rows, GRID_W, H, Dh)
    vg = v.reshape(B, rows, GRID_W, H, Dh)

    def row_step(args):
        r, q_row = args
        r0 = jnp.clip(r - wr // 2, 0, rows - wr)
        k_band = lax.dynamic_slice_in_dim(kg, r0, wr, axis=1)
        v_band = lax.dynamic_slice_in_dim(vg, r0, wr, axis=1)
        k_blk = jnp.take(k_band, key_cols_j, axis=2)
        v_blk = jnp.take(v_band, key_cols_j, axis=2)
        s_lat = jnp.einsum('bnqhd,bjnmhd->bhnqjm', q_row, k_blk).astype(jnp.float32)
        rel_r = r0 + jnp.arange(wr) - r + WIN_R - 1
        bias = jnp.take(rpb32, rel_r, axis=1)
        bias = jnp.take(bias, rel_idx_j, axis=2).transpose(0, 2, 3, 1, 4)
        s_lat = jnp.where(valid_j, s_lat + bias, MASK_VALUE)
        s_ctx = jnp.einsum('bnqhd,bkhd->bhnqk', q_row, kc).astype(jnp.float32)
        s = jnp.concatenate([s_lat.reshape(B, H, ncb, WIN_C, wr * span), s_ctx], axis=-1)
        p = jax.nn.softmax(s, axis=-1).astype(v.dtype)
        p_lat = p[..., :wr * span].reshape(B, H, ncb, WIN_C, wr, span)
        p_ctx = p[..., wr * span:]
        return (jnp.einsum('bhnqjm,bjnmhd->bnqhd', p_lat, v_blk)
                + jnp.einsum('bhnqk,bkhd->bnqhd', p_ctx, vc))

    o = lax.map(row_step, (jnp.arange(rows), qg))
    return o.transpose(1, 0, 2, 3, 4, 5).reshape(B, T, H * Dh)


def _context_attention(qc, kc, vc):
    B, K, H, Dh = qc.shape
    s = jnp.einsum('bqhd,bkhd->bhqk', qc * (Dh ** -0.5), kc).astype(jnp.float32)
    p = jax.nn.softmax(s, axis=-1).astype(vc.dtype)
    return jnp.einsum('bhqk,bkhd->bqhd', p, vc).reshape(B, K, H * Dh)


def setup_inputs(seed: int = 0) -> dict:
    key = jax.random.key(seed)
    ks = iter(jax.random.split(key, 32))

    def nrm(shape, scale):
        return jax.random.normal(next(ks), shape, jnp.float32) * scale

    D, F = D_MODEL, D_FF
    return {
        'x': nrm((BATCH, SEQ, D), 1.0),
        'c': nrm((BATCH, D), 1.0),
        'ctx': nrm((BATCH, CTX_LEN, D), 1.0),
        'c_ctx': nrm((D,), 1.0),
        'mod_w': nrm((DEPTH, D, N_MOD * D), 0.5 * D ** -0.5),
        'mod_b': nrm((DEPTH, N_MOD * D), 0.02),
        'norm_g': 1.0 + nrm((DEPTH, 3, D), 0.05),
        'ffn_w_gate': nrm((DEPTH, 2, D, F), D ** -0.5),
        'ffn_w_up': nrm((DEPTH, 2, D, F), D ** -0.5),
        'ffn_w_down': nrm((DEPTH, 2, F, D), F ** -0.5),
        'conv_w_pw1': nrm((N_CONV_LAYERS, D, 2 * D), D ** -0.5),
        'conv_b_pw1': nrm((N_CONV_LAYERS, 2 * D), 0.02),
        'conv_w_dw': nrm((N_CONV_LAYERS, CONV_WIDTH, D), CONV_WIDTH ** -0.5),
        'conv_b_dw': nrm((N_CONV_LAYERS, D), 0.02),
        'conv_ln_g': 1.0 + nrm((N_CONV_LAYERS, D), 0.05),
        'conv_ln_b': nrm((N_CONV_LAYERS, D), 0.02),
        'conv_w_pw2': nrm((N_CONV_LAYERS, D, D), D ** -0.5),
        'conv_b_pw2': nrm((N_CONV_LAYERS, D), 0.02),
        'na_w_qkv': nrm((N_NA_LAYERS, D, 3 * D), D ** -0.5),
        'na_b_qkv': nrm((N_NA_LAYERS, 3 * D), 0.02),
        'na_rpb': nrm((N_NA_LAYERS, N_HEADS, 2 * WIN_R - 1, 2 * WIN_C - 1), 0.1),
        'na_w_o': nrm((N_NA_LAYERS, D, D), D ** -0.5),
        'na_b_o': nrm((N_NA_LAYERS, D), 0.02),
        'final_g': 1.0 + nrm((D,), 0.05),
    }


def reference(x, c, ctx, c_ctx, mod_w, mod_b, norm_g, ffn_w_gate, ffn_w_up, ffn_w_down,
              conv_w_pw1, conv_b_pw1, conv_w_dw, conv_b_dw, conv_ln_g, conv_ln_b, conv_w_pw2, conv_b_pw2,
              na_w_qkv, na_b_qkv, na_rpb, na_w_o, na_b_o, final_g):
    B, T, D = x.shape
    K = ctx.shape[1]
    xc = ctx
    for i in range(DEPTH):
        mixer = i % N_MIXERS
        j = i // N_MIXERS
        last = i == DEPTH - 1
        run_ctx = (not last) or mixer == 1
        ml = (jax.nn.silu(c) @ mod_w[i] + mod_b[i]).reshape(B, 1, N_MOD, D)
        mc = (jax.nn.silu(c_ctx) @ mod_w[i] + mod_b[i]).reshape(N_MOD, D)
        lat_m = [ml[:, :, n] for n in range(N_MOD)]
        ctx_m = [mc[n] for n in range(N_MOD)]

        x = _ffn_half(x, lat_m[0:3], norm_g[i, 0], ffn_w_gate[i, 0], ffn_w_up[i, 0], ffn_w_down[i, 0])
        if run_ctx:
            xc = _ffn_half(xc, ctx_m[0:3], norm_g[i, 0], ffn_w_gate[i, 0], ffn_w_up[i, 0], ffn_w_down[i, 0])

        h = _modulate(_rmsnorm(x, norm_g[i, 1]), lat_m[3], lat_m[4])
        if run_ctx:
            hc = _modulate(_rmsnorm(xc, norm_g[i, 1]), ctx_m[3], ctx_m[4])
        if mixer == 0:
            conv_p = (conv_w_pw1[j], conv_b_pw1[j], conv_w_dw[j], conv_b_dw[j],
                      conv_ln_g[j], conv_ln_b[j], conv_w_pw2[j], conv_b_pw2[j])
            y = _conv_module(h, *conv_p)
            if not last:
                yc = _conv_module(hc, *conv_p)
        else:
            qkv = (h @ na_w_qkv[j] + na_b_qkv[j]).reshape(B, T, 3, N_HEADS, HEAD_DIM)
            qkvc = (hc @ na_w_qkv[j] + na_b_qkv[j]).reshape(B, K, 3, N_HEADS, HEAD_DIM)
            kc, vc = qkvc[:, :, 1], qkvc[:, :, 2]
            o = _neighbourhood_attention(qkv[:, :, 0], qkv[:, :, 1], qkv[:, :, 2], kc, vc, na_rpb[j])
            y = o @ na_w_o[j] + na_b_o[j]
            if not last:
                yc = _context_attention(qkvc[:, :, 0], kc, vc) @ na_w_o[j] + na_b_o[j]
        x = x + lat_m[5] * y

        x = _ffn_half(x, lat_m[6:9], norm_g[i, 2], ffn_w_gate[i, 1], ffn_w_up[i, 1], ffn_w_down[i, 1])
        if not last:
            xc = xc + ctx_m[5] * yc
            xc = _ffn_half(xc, ctx_m[6:9], norm_g[i, 2], ffn_w_gate[i, 1], ffn_w_up[i, 1], ffn_w_down[i, 1])
    return _rmsnorm(x, final_g)
```

```python
import functools

import numpy as np
import jax
import jax.numpy as jnp
from jax import lax
from jax.experimental import pallas as pl
from jax.experimental.pallas import tpu as pltpu

D_MODEL = 1024
DEPTH = 4
GRID_W = 64
N_HEADS = 16
HEAD_DIM = D_MODEL // N_HEADS
D_FF = 2816
CONV_WIDTH = 31
WIN_R = 8
WIN_C = 16
N_MOD = 9
RMS_EPS = 1e-6
LN_EPS = 1e-5
MASK_VALUE = -1e30

F32 = jnp.float32
BF16 = jnp.bfloat16

V7X_LANES = 128
FF_CHUNK = 256
CONV_HALO = 16
CONV_ROWS = 128
NA_Q_ROWS = 8
NA_HALO_ROWS = 4
NA_KEY_ROWS = NA_Q_ROWS + 2 * NA_HALO_ROWS
NA_SPAN = 2 * WIN_C
VMEM_LIMIT = 56 * 1024 * 1024


def _const_spec(shape):
    return pl.BlockSpec(shape, lambda *_: (0,) * len(shape))


def _params(n_grid_dims, vmem=VMEM_LIMIT):
    return pltpu.CompilerParams(
        dimension_semantics=("arbitrary",) * n_grid_dims, vmem_limit_bytes=vmem)


def _dot(a, b):
    return jnp.dot(a, b, preferred_element_type=F32)


def _dot_nt(a, b):
    return lax.dot_general(a, b, (((1,), (1,)), ((), ())), preferred_element_type=F32)


def _rmsnorm(x, g):
    return x * lax.rsqrt(jnp.mean(x * x, axis=-1, keepdims=True) + RMS_EPS) * g


def _rms_mod(x, g, shift, scale):
    return _rmsnorm(x, g) * (1.0 + scale) + shift


def _mod_kernel(cc_ref, w_ref, b_ref, o_ref):
    s = jax.nn.silu(cc_ref[...]).astype(BF16)
    o_ref[...] = _dot(s, w_ref[...].astype(BF16)) + b_ref[...]


def _modulations(cc, mod_w, mod_b):
    rows = cc.shape[0]
    return pl.pallas_call(
        _mod_kernel,
        grid=(DEPTH, N_MOD),
        in_specs=[
            _const_spec((rows, D_MODEL)),
            pl.BlockSpec((None, D_MODEL, D_MODEL), lambda l, j: (l, 0, j)),
            pl.BlockSpec((None, 1, D_MODEL), lambda l, j: (l, 0, j)),
        ],
        out_specs=pl.BlockSpec((None, rows, D_MODEL), lambda l, j: (l, 0, j)),
        out_shape=jax.ShapeDtypeStruct((DEPTH, rows, N_MOD * D_MODEL), F32),
        compiler_params=_params(2),
        name="modulations",
    )(cc, mod_w, mod_b.reshape(DEPTH, 1, N_MOD * D_MODEL))


def _ffn_kernel(x_ref, mod_ref, g_ref, wg_ref, wu_ref, wd_ref, *rest, mod0, final):
    o_ref = rest[-1]
    x = x_ref[...]
    shift, scale, gate = (mod_ref[mod0 + k:mod0 + k + 1, :] for k in range(3))
    h = _rms_mod(x, g_ref[...], shift, scale).astype(BF16)
    acc = jnp.zeros(x.shape, F32)
    for f in range(D_FF // FF_CHUNK):
        sl = slice(f * FF_CHUNK, (f + 1) * FF_CHUNK)
        a = jax.nn.silu(_dot(h, wg_ref[:, sl])) * _dot(h, wu_ref[:, sl])
        acc = acc + _dot(a.astype(BF16), wd_ref[sl, :])
    out = x + 0.5 * gate * acc
    if final:
        out = _rmsnorm(out, rest[0][...])
    o_ref[...] = out


def _ffn_half(x, mods, g, wg, wu, wd, mod0, tm, final_g=None):
    t = x.shape[0]
    final = final_g is not None
    in_specs = [
        pl.BlockSpec((tm, D_MODEL), lambda i: (i, 0)),
        _const_spec((N_MOD, D_MODEL)),
        _const_spec((1, D_MODEL)),
        _const_spec((D_MODEL, D_FF)),
        _const_spec((D_MODEL, D_FF)),
        _const_spec((D_FF, D_MODEL)),
    ]
    args = [x, mods, g.reshape(1, D_MODEL), wg, wu, wd]
    if final:
        in_specs.append(_const_spec((1, D_MODEL)))
        args.append(final_g.reshape(1, D_MODEL))
    return pl.pallas_call(
        functools.partial(_ffn_kernel, mod0=mod0, final=final),
        grid=(t // tm,),
        in_specs=in_specs,
        out_specs=pl.BlockSpec((tm, D_MODEL), lambda i: (i, 0)),
        out_shape=jax.ShapeDtypeStruct((t, D_MODEL), F32),
        compiler_params=_params(1),
        name="ffn_half",
    )(*args)


def _conv_glu_kernel(x_ref, mod_ref, g_ref, w1_ref, b1_ref, u_ref):
    h = _rms_mod(x_ref[...], g_ref[...], mod_ref[3:4, :], mod_ref[4:5, :]).astype(BF16)
    a = _dot(h, w1_ref[:, :D_MODEL]) + b1_ref[:, :D_MODEL]
    gt = _dot(h, w1_ref[:, D_MODEL:]) + b1_ref[:, D_MODEL:]
    u_ref[...] = a * jax.nn.sigmoid(gt)


def _conv_glu(x, mods, g, w1, b1, tm):
    t = x.shape[0]
    return pl.pallas_call(
        _conv_glu_kernel,
        grid=(t // tm,),
        in_specs=[
            pl.BlockSpec((tm, D_MODEL), lambda i: (i, 0)),
            _const_spec((N_MOD, D_MODEL)),
            _const_spec((1, D_MODEL)),
            _const_spec((D_MODEL, 2 * D_MODEL)),
            _const_spec((1, 2 * D_MODEL)),
        ],
        out_specs=pl.BlockSpec((tm, D_MODEL), lambda i: (i, 0)),
        out_shape=jax.ShapeDtypeStruct((t, D_MODEL), F32),
        compiler_params=_params(1),
        name="conv_glu",
    )(x, mods, g.reshape(1, D_MODEL), w1, b1.reshape(1, 2 * D_MODEL))


def _conv_tail_kernel(ul_ref, um_ref, ur_ref, x_ref, mod_ref, wdw_ref, bdw_ref, lng_ref, lnb_ref,
                      w2_ref, b2_ref, o_ref, win_ref, cv_ref, *, tm):
    i = pl.program_id(0)
    last = pl.num_programs(0) - 1
    win_ref[0:CONV_HALO, :] = jnp.where(i > 0, ul_ref[...], 0.0)
    win_ref[CONV_HALO:CONV_HALO + tm, :] = um_ref[...]
    win_ref[CONV_HALO + tm:, :] = jnp.where(i < last, ur_ref[...], 0.0)
    off0 = CONV_HALO - CONV_WIDTH // 2
    rows = min(CONV_ROWS, tm)
    for r in range(tm // rows):
        for c in range(D_MODEL // V7X_LANES):
            cs = slice(c * V7X_LANES, (c + 1) * V7X_LANES)
            acc = jnp.zeros((rows, V7X_LANES), F32)
            for k in range(CONV_WIDTH):
                start = r * rows + off0 + k
                acc = acc + win_ref[start:start + rows, cs] * wdw_ref[k:k + 1, cs]
            cv_ref[r * rows:(r + 1) * rows, cs] = acc
    cv = cv_ref[...] + bdw_ref[...]
    mu = jnp.mean(cv, axis=-1, keepdims=True)
    cen = cv - mu
    var = jnp.mean(cen * cen, axis=-1, keepdims=True)
    y = cen * lax.rsqrt(var + LN_EPS) * lng_ref[...] + lnb_ref[...]
    out = _dot(jax.nn.silu(y).astype(BF16), w2_ref[...]) + b2_ref[...]
    o_ref[...] = x_ref[...] + mod_ref[5:6, :] * out


def _conv_tail(u, x, mods, wdw, bdw, lng, lnb, w2, b2, tm):
    t = x.shape[0]
    hb = tm // CONV_HALO
    n_halo_blocks = t // CONV_HALO
    row = lambda v: v.reshape(1, D_MODEL)
    return pl.pallas_call(
        functools.partial(_conv_tail_kernel, tm=tm),
        grid=(t // tm,),
        in_specs=[
            pl.BlockSpec((CONV_HALO, D_MODEL), lambda i: (jnp.maximum(i * hb - 1, 0), 0)),
            pl.BlockSpec((tm, D_MODEL), lambda i: (i, 0)),
            pl.BlockSpec((CONV_HALO, D_MODEL),
                         lambda i: (jnp.minimum((i + 1) * hb, n_halo_blocks - 1), 0)),
            pl.BlockSpec((tm, D_MODEL), lambda i: (i, 0)),
            _const_spec((N_MOD, D_MODEL)),
            _const_spec((CONV_WIDTH, D_MODEL)),
            _const_spec((1, D_MODEL)),
            _const_spec((1, D_MODEL)),
            _const_spec((1, D_MODEL)),
            _const_spec((D_MODEL, D_MODEL)),
            _const_spec((1, D_MODEL)),
        ],
        out_specs=pl.BlockSpec((tm, D_MODEL), lambda i: (i, 0)),
        out_shape=jax.ShapeDtypeStruct((t, D_MODEL), F32),
        scratch_shapes=[
            pltpu.VMEM((tm + 2 * CONV_HALO, D_MODEL), F32),
            pltpu.VMEM((tm, D_MODEL), F32),
        ],
        compiler_params=_params(1),
        name="conv_tail",
    )(u, u, u, x, mods, wdw, row(bdw), row(lng), row(lnb), w2, row(b2))


def _qkv_kernel(x_ref, mod_ref, g_ref, w_ref, b_ref, q_ref, k_ref, v_ref):
    h = _rms_mod(x_ref[...], g_ref[...], mod_ref[3:4, :], mod_ref[4:5, :]).astype(BF16)
    d = D_MODEL
    q = _dot(h, w_ref[:, 0:d]) + b_ref[:, 0:d]
    q_ref[...] = (q * (HEAD_DIM ** -0.5)).astype(BF16)
    k_ref[...] = _dot(h, w_ref[:, d:2 * d]) + b_ref[:, d:2 * d]
    v_ref[...] = _dot(h, w_ref[:, 2 * d:3 * d]) + b_ref[:, 2 * d:3 * d]


def _qkv_proj(x, mods, g, w, b, tm):
    t = x.shape[0]
    tok = pl.BlockSpec((tm, D_MODEL), lambda i: (i, 0))
    return pl.pallas_call(
        _qkv_kernel,
        grid=(t // tm,),
        in_specs=[
            tok,
            _const_spec((N_MOD, D_MODEL)),
            _const_spec((1, D_MODEL)),
            _const_spec((D_MODEL, 3 * D_MODEL)),
            _const_spec((1, 3 * D_MODEL)),
        ],
        out_specs=[tok, tok, tok],
        out_shape=[
            jax.ShapeDtypeStruct((t, D_MODEL), BF16),
            jax.ShapeDtypeStruct((t, D_MODEL), F32),
            jax.ShapeDtypeStruct((t, D_MODEL), F32),
        ],
        compiler_params=_params(1),
        name="qkv_proj",
    )(x, mods, g.reshape(1, D_MODEL), w, b.reshape(1, 3 * D_MODEL))


def _head_lane_masks():
    lane = lax.broadcasted_iota(jnp.int32, (1, V7X_LANES), 1)
    return [lane < HEAD_DIM, lane >= HEAD_DIM]


def _softmax_pv(scores, values, head_mask):
    mx = functools.reduce(jnp.maximum, [jnp.max(s, axis=-1, keepdims=True) for s in scores])
    ps = [jnp.exp(s - mx) for s in scores]
    denom = functools.reduce(jnp.add, [jnp.sum(p, axis=-1, keepdims=True) for p in ps])
    o = functools.reduce(jnp.add, [
        _dot(p.astype(BF16), jnp.where(head_mask, v, jnp.zeros_like(v)))
        for p, v in zip(ps, values)])
    return o / denom


def _na_kernel(q_ref, kt_ref, km_ref, kb_ref, vt_ref, vm_ref, vb_ref, kc_ref, vc_ref, tab_ref, o_ref):
    masks = _head_lane_masks()
    q3 = q_ref[...].reshape(NA_Q_ROWS, GRID_W, V7X_LANES)
    kw = jnp.concatenate([kt_ref[...], km_ref[...], kb_ref[...]], axis=0)
    vw = jnp.concatenate([vt_ref[...], vm_ref[...], vb_ref[...]], axis=0)
    kw3 = kw.reshape(NA_KEY_ROWS, GRID_W, V7X_LANES)
    vw3 = vw.reshape(NA_KEY_ROWS, GRID_W, V7X_LANES)
    kc = kc_ref[...].astype(BF16)
    vc = vc_ref[...].astype(BF16)
    nq = NA_Q_ROWS * WIN_C
    nk = NA_KEY_ROWS * NA_SPAN
    for n in range(GRID_W // WIN_C):
        c0 = min(max(n * WIN_C - WIN_C // 2, 0), GRID_W - NA_SPAN)
        qn = q3[:, n * WIN_C:(n + 1) * WIN_C, :].reshape(nq, V7X_LANES)
        kn = kw3[:, c0:c0 + NA_SPAN, :].reshape(nk, V7X_LANES).astype(BF16)
        vn = vw3[:, c0:c0 + NA_SPAN, :].reshape(nk, V7X_LANES).astype(BF16)
        o = jnp.zeros((nq, V7X_LANES), F32)
        for hh in range(2):
            qm = jnp.where(masks[hh], qn, jnp.zeros_like(qn))
            s_lat = _dot_nt(qm, kn) + tab_ref[hh, n]
            s_ctx = _dot_nt(qm, kc)
            o = o + _softmax_pv([s_lat, s_ctx], [vn, vc], masks[hh])
        o3 = o.astype(BF16).reshape(NA_Q_ROWS, WIN_C, V7X_LANES)
        for r in range(NA_Q_ROWS):
            o_ref[r * GRID_W + n * WIN_C:r * GRID_W + (n + 1) * WIN_C, :] = o3[r]


def _na_attention(q, k, v, kc, vc, table):
    t = q.shape[0]
    tq = NA_Q_ROWS * GRID_W
    th = NA_HALO_ROWS * GRID_W
    nb = t // tq
    n_halo_blocks = t // th
    hpb = tq // th
    n_pairs = N_HEADS // 2
    main = lambda: pl.BlockSpec((tq, V7X_LANES), lambda hp, b: (b, hp))
    top = lambda: pl.BlockSpec((th, V7X_LANES), lambda hp, b: (jnp.maximum(b * hpb - 1, 0), hp))
    bot = lambda: pl.BlockSpec(
        (th, V7X_LANES), lambda hp, b: (jnp.minimum((b + 1) * hpb, n_halo_blocks - 1), hp))
    ctx = lambda: pl.BlockSpec((kc.shape[0], V7X_LANES), lambda hp, b: (0, hp))

    def table_index(hp, b):
        cls = jnp.where(b == 0, 0, jnp.where(b == nb - 1, 2, 1))
        return (cls, hp, 0, 0, 0, 0)

    nq = NA_Q_ROWS * WIN_C
    nk = NA_KEY_ROWS * NA_SPAN
    return pl.pallas_call(
        _na_kernel,
        grid=(n_pairs, nb),
        in_specs=[main(), top(), main(), bot(), top(), main(), bot(), ctx(), ctx(),
                  pl.BlockSpec((None, None, 2, GRID_W // WIN_C, nq, nk), table_index)],
        out_specs=main(),
        out_shape=jax.ShapeDtypeStruct((t, D_MODEL), BF16),
        compiler_params=_params(2),
        name="neighbourhood_attention",
    )(q, k, k, k, v, v, v, kc, vc, table)


def _ctx_attn_kernel(q_ref, k_ref, v_ref, o_ref):
    masks = _head_lane_masks()
    q = q_ref[...]
    k = k_ref[...].astype(BF16)
    v = v_ref[...].astype(BF16)
    o = jnp.zeros(q.shape, F32)
    for hh in range(2):
        qm = jnp.where(masks[hh], q, jnp.zeros_like(q))
        o = o + _softmax_pv([_dot_nt(qm, k)], [v], masks[hh])
    o_ref[...] = o.astype(BF16)


def _ctx_attention(q, k, v):
    t = q.shape[0]
    blk = pl.BlockSpec((t, V7X_LANES), lambda hp: (0, hp))
    return pl.pallas_call(
        _ctx_attn_kernel,
        grid=(N_HEADS // 2,),
        in_specs=[blk, blk, blk],
        out_specs=blk,
        out_shape=jax.ShapeDtypeStruct((t, D_MODEL), BF16),
        compiler_params=_params(1),
        name="context_attention",
    )(q, k, v)


def _out_proj_kernel(o_ref, x_ref, mod_ref, w_ref, b_ref, y_ref):
    y = _dot(o_ref[...], w_ref[...]) + b_ref[...]
    y_ref[...] = x_ref[...] + mod_ref[5:6, :] * y


def _out_proj(o, x, mods, w, b, tm):
    t = x.shape[0]
    tok = pl.BlockSpec((tm, D_MODEL), lambda i: (i, 0))
    return pl.pallas_call(
        _out_proj_kernel,
        grid=(t // tm,),
        in_specs=[tok, tok, _const_spec((N_MOD, D_MODEL)), _const_spec((D_MODEL, D_MODEL)),
                  _const_spec((1, D_MODEL))],
        out_specs=tok,
        out_shape=jax.ShapeDtypeStruct((t, D_MODEL), F32),
        compiler_params=_params(1),
        name="attn_out_proj",
    )(o, x, mods, w, b.reshape(1, D_MODEL))


def _na_index_tables(rows):
    ncb = GRID_W // WIN_C
    n = np.arange(ncb)
    c0 = np.clip(n * WIN_C - WIN_C // 2, 0, GRID_W - NA_SPAN)
    q_col = n[:, None] * WIN_C + np.arange(WIN_C)[None, :]
    k_col = c0[:, None] + np.arange(NA_SPAN)[None, :]
    q_start = np.clip(q_col - WIN_C // 2, 0, GRID_W - WIN_C)
    col_valid = ((k_col[:, None, :] >= q_start[:, :, None])
                 & (k_col[:, None, :] < q_start[:, :, None] + WIN_C))
    col_idx = np.clip(k_col[:, None, :] - q_col[:, :, None], -(WIN_C - 1), WIN_C - 1) + WIN_C - 1
    wr = min(WIN_R, rows)
    rho = np.arange(NA_Q_ROWS)
    kappa = np.arange(NA_KEY_ROWS)
    row_idx = np.clip(kappa[None, :] - NA_HALO_ROWS - rho[:, None], -(WIN_R - 1), WIN_R - 1) + WIN_R - 1
    row_valid = []
    for r_start in (0, NA_Q_ROWS, rows - NA_Q_ROWS):
        r = r_start + rho
        j = r_start - NA_HALO_ROWS + kappa
        r0 = np.clip(r - wr // 2, 0, rows - wr)
        row_valid.append((j[None, :] >= r0[:, None]) & (j[None, :] < r0[:, None] + wr))
    row_valid = np.stack(row_valid)
    valid = (row_valid[:, None, :, None, :, None] & col_valid[None, :, None, :, None, :])
    mask_add = np.where(valid, 0.0, MASK_VALUE).astype(np.float32)
    return col_idx.astype(np.int32), row_idx.astype(np.int32), mask_add


def _na_bias_table(rpb, rows):
    ncb = GRID_W // WIN_C
    col_idx, row_idx, mask_add = _na_index_tables(rows)
    by_col = jnp.take(rpb, jnp.asarray(col_idx.reshape(-1)), axis=2)
    by_row = jnp.take(by_col, jnp.asarray(row_idx.reshape(-1)), axis=1)
    base = by_row.reshape(N_HEADS, NA_Q_ROWS, NA_KEY_ROWS, ncb, WIN_C, NA_SPAN)
    base = base.transpose(0, 3, 1, 4, 2, 5)
    table = base[None] + jnp.asarray(mask_add)[:, None]
    return table.reshape(3, N_HEADS // 2, 2, ncb, NA_Q_ROWS * WIN_C, NA_KEY_ROWS * NA_SPAN)


def kernel(x, c, ctx, c_ctx, mod_w, mod_b, norm_g, ffn_w_gate, ffn_w_up, ffn_w_down, conv_w_pw1, conv_b_pw1, conv_w_dw, conv_b_dw, conv_ln_g, conv_ln_b, conv_w_pw2, conv_b_pw2, na_w_qkv, na_b_qkv, na_rpb, na_w_o, na_b_o, final_g):
    b, t, d = x.shape
    assert b == 1 and d == D_MODEL and t % (NA_Q_ROWS * GRID_W) == 0
    n_ctx = ctx.shape[1]
    tm = 512
    tmc = n_ctx
    xl = x.reshape(t, d)
    xc = ctx.reshape(n_ctx, d)

    cc = jnp.zeros((8, d), F32).at[0].set(c[0]).at[1].set(c_ctx)
    mods = _modulations(cc, mod_w, mod_b).reshape(DEPTH, 8, N_MOD, d)
    wg, wu, wd = (w.astype(BF16) for w in (ffn_w_gate, ffn_w_up, ffn_w_down))
    w1, w2 = conv_w_pw1.astype(BF16), conv_w_pw2.astype(BF16)
    wqkv, wo = na_w_qkv.astype(BF16), na_w_o.astype(BF16)

    for i in range(DEPTH):
        mixer, j = i % 2, i // 2
        last = i == DEPTH - 1
        ml, mc = mods[i, 0], mods[i, 1]
        xl = _ffn_half(xl, ml, norm_g[i, 0], wg[i, 0], wu[i, 0], wd[i, 0], 0, tm)
        xc = _ffn_half(xc, mc, norm_g[i, 0], wg[i, 0], wu[i, 0], wd[i, 0], 0, tmc)
        if mixer == 0:
            conv = (conv_w_dw[j], conv_b_dw[j], conv_ln_g[j], conv_ln_b[j], w2[j], conv_b_pw2[j])
            ul = _conv_glu(xl, ml, norm_g[i, 1], w1[j], conv_b_pw1[j], tm)
            xl = _conv_tail(ul, xl, ml, *conv, tm)
            if not last:
                uc = _conv_glu(xc, mc, norm_g[i, 1], w1[j], conv_b_pw1[j], tmc)
                xc = _conv_tail(uc, xc, mc, *conv, tmc)
        else:
            table = _na_bias_table(na_rpb[j], t // GRID_W)
            ql, kl, vl = _qkv_proj(xl, ml, norm_g[i, 1], wqkv[j], na_b_qkv[j], tm)
            qc, kc, vc = _qkv_proj(xc, mc, norm_g[i, 1], wqkv[j], na_b_qkv[j], tmc)
            ol = _na_attention(ql, kl, vl, kc, vc, table)
            xl = _out_proj(ol, xl, ml, wo[j], na_b_o[j], tm)
            if not last:
                oc = _ctx_attention(qc, kc, vc)
                xc = _out_proj(oc, xc, mc, wo[j], na_b_o[j], tmc)
        xl = _ffn_half(xl, ml, norm_g[i, 2], wg[i, 1], wu[i, 1], wd[i, 1], 6, tm,
                       final_g=final_g if last else None)
        if not last:
            xc = _ffn_half(xc, mc, norm_g[i, 2], wg[i, 1], wu[i, 1], wd[i, 1], 6, tmc)
    return xl.reshape(b, t, d)
```

```python
import functools

import numpy as np
import jax
import jax.numpy as jnp
from jax import lax
from jax.experimental import pallas as pl
from jax.experimental.pallas import tpu as pltpu

D_MODEL = 1024
DEPTH = 4
GRID_W = 64
N_HEADS = 16
HEAD_DIM = D_MODEL // N_HEADS
D_FF = 2816
CONV_WIDTH = 31
WIN_R = 8
WIN_C = 16
N_MOD = 9
RMS_EPS = 1e-6
LN_EPS = 1e-5
MASK_VALUE = -1e30

F32 = jnp.float32
BF16 = jnp.bfloat16

V7X_LANES = 128
V7X_SUBLANES = 8
V7X_MXU = 256
FF_CHUNK = V7X_MXU
CONV_HALO = 16
CONV_ROWS = 128
NA_Q_ROWS = 8
NA_HALO_ROWS = WIN_R // 2
NA_KEY_ROWS = NA_Q_ROWS + 2 * NA_HALO_ROWS
NA_SPAN = 2 * WIN_C
NA_GROUP = V7X_MXU // HEAD_DIM
NA_NQ = NA_Q_ROWS * WIN_C
NA_NK = NA_KEY_ROWS * NA_SPAN
VMEM_LIMIT = 56 * 1024 * 1024


def _sel_spec(prefix, tail):
    prefix, tail = tuple(prefix), tuple(tail)
    return pl.BlockSpec((None,) * len(prefix) + tail, lambda *_: prefix + (0,) * len(tail),
                        pipeline_mode=pl.Buffered(1))


def _tok_spec(tm):
    return pl.BlockSpec((tm, D_MODEL), lambda i: (i, 0))


def _params(n_grid_dims):
    return pltpu.CompilerParams(
        dimension_semantics=("arbitrary",) * n_grid_dims, vmem_limit_bytes=VMEM_LIMIT)


def _dot(a, b):
    return jnp.dot(a, b, preferred_element_type=F32)


def _dot_nt(a, b):
    return lax.dot_general(a, b, (((1,), (1,)), ((), ())), preferred_element_type=F32)


def _rmsnorm(x, g):
    return x * lax.rsqrt(jnp.mean(x * x, axis=-1, keepdims=True) + RMS_EPS) * g


def _rms_mod(x, g, shift, scale):
    return _rmsnorm(x, g) * (1.0 + scale) + shift


def _cast_once(w_ref, wb_ref):
    @pl.when(pl.program_id(0) == 0)
    def _():
        wb_ref[...] = w_ref[...].astype(BF16)


def _mod_kernel(cc_ref, w_ref, b_ref, o_ref):
    s = jax.nn.silu(cc_ref[...]).astype(BF16)
    o_ref[...] = _dot(s, w_ref[...].astype(BF16)) + b_ref[...]


def _modulations(cc, mod_w, mod_b):
    rows = cc.shape[0]
    return pl.pallas_call(
        _mod_kernel,
        grid=(DEPTH, N_MOD),
        in_specs=[
            pl.BlockSpec((rows, D_MODEL), lambda l, j: (0, 0)),
            pl.BlockSpec((None, D_MODEL, D_MODEL), lambda l, j: (l, 0, j)),
            pl.BlockSpec((None, 1, D_MODEL), lambda l, j: (l, 0, j)),
        ],
        out_specs=pl.BlockSpec((None, rows, D_MODEL), lambda l, j: (l, 0, j)),
        out_shape=jax.ShapeDtypeStruct((DEPTH, rows, N_MOD * D_MODEL), F32),
        compiler_params=_params(2),
        name="modulations",
    )(cc, mod_w, mod_b.reshape(DEPTH, 1, N_MOD * D_MODEL))


def _ffn_kernel(x_ref, mod_ref, g_ref, wg_ref, wu_ref, wd_ref, *rest, mod0, final):
    o_ref = rest[-1]
    x = x_ref[...]
    shift, scale, gate = (mod_ref[mod0 + k:mod0 + k + 1, :] for k in range(3))
    h = _rms_mod(x, g_ref[...], shift, scale).astype(BF16)
    acc = jnp.zeros(x.shape, F32)
    for f in range(D_FF // FF_CHUNK):
        sl = slice(f * FF_CHUNK, (f + 1) * FF_CHUNK)
        a = jax.nn.silu(_dot(h, wg_ref[:, sl])) * _dot(h, wu_ref[:, sl])
        acc = acc + _dot(a.astype(BF16), wd_ref[sl, :])
    out = x + 0.5 * gate * acc
    if final:
        out = _rmsnorm(out, rest[0][...])
    o_ref[...] = out


def _ffn_half(x, mods, norm_g, wg, wu, wd, layer, stream, half, tm, final_g=None):
    t = x.shape[0]
    final = final_g is not None
    in_specs = [
        _tok_spec(tm),
        _sel_spec((layer, stream), (N_MOD, D_MODEL)),
        _sel_spec((layer, 2 * half), (1, D_MODEL)),
        _sel_spec((layer, half), (D_MODEL, D_FF)),
        _sel_spec((layer, half), (D_MODEL, D_FF)),
        _sel_spec((layer, half), (D_FF, D_MODEL)),
    ]
    args = [x, mods, norm_g, wg, wu, wd]
    if final:
        in_specs.append(_sel_spec((), (1, D_MODEL)))
        args.append(final_g.reshape(1, D_MODEL))
    return pl.pallas_call(
        functools.partial(_ffn_kernel, mod0=6 * half, final=final),
        grid=(t // tm,),
        in_specs=in_specs,
        out_specs=_tok_spec(tm),
        out_shape=jax.ShapeDtypeStruct((t, D_MODEL), F32),
        compiler_params=_params(1),
        name="ffn_half",
    )(*args)


def _conv_glu_kernel(x_ref, mod_ref, g_ref, w1_ref, b1_ref, u_ref, w1b_ref):
    _cast_once(w1_ref, w1b_ref)
    h = _rms_mod(x_ref[...], g_ref[...], mod_ref[3:4, :], mod_ref[4:5, :]).astype(BF16)
    a = _dot(h, w1b_ref[:, :D_MODEL]) + b1_ref[:, :D_MODEL]
    gt = _dot(h, w1b_ref[:, D_MODEL:]) + b1_ref[:, D_MODEL:]
    u_ref[...] = a * jax.nn.sigmoid(gt)


def _conv_glu(x, mods, norm_g, w1, b1, layer, stream, j, tm):
    t = x.shape[0]
    return pl.pallas_call(
        _conv_glu_kernel,
        grid=(t // tm,),
        in_specs=[
            _tok_spec(tm),
            _sel_spec((layer, stream), (N_MOD, D_MODEL)),
            _sel_spec((layer, 1), (1, D_MODEL)),
            _sel_spec((j,), (D_MODEL, 2 * D_MODEL)),
            _sel_spec((j,), (1, 2 * D_MODEL)),
        ],
        out_specs=_tok_spec(tm),
        out_shape=jax.ShapeDtypeStruct((t, D_MODEL), F32),
        scratch_shapes=[pltpu.VMEM((D_MODEL, 2 * D_MODEL), BF16)],
        compiler_params=_params(1),
        name="conv_glu",
    )(x, mods, norm_g, w1, b1)


def _conv_tail_kernel(ul_ref, um_ref, ur_ref, x_ref, mod_ref, wdw_ref, bdw_ref, lng_ref, lnb_ref,
                      w2_ref, b2_ref, o_ref, win_ref, cv_ref, w2b_ref, *, tm):
    _cast_once(w2_ref, w2b_ref)
    i = pl.program_id(0)
    last = pl.num_programs(0) - 1
    win_ref[0:CONV_HALO, :] = jnp.where(i > 0, ul_ref[...], 0.0)
    win_ref[CONV_HALO:CONV_HALO + tm, :] = um_ref[...]
    win_ref[CONV_HALO + tm:, :] = jnp.where(i < last, ur_ref[...], 0.0)
    off0 = CONV_HALO - CONV_WIDTH // 2
    rows = min(CONV_ROWS, tm)
    ext = rows + V7X_SUBLANES
    for r in range(tm // rows):
        for c in range(D_MODEL // V7X_LANES):
            cs = slice(c * V7X_LANES, (c + 1) * V7X_LANES)
            acc = None
            for s in range(V7X_SUBLANES):
                p = None
                for k in range(CONV_WIDTH):
                    a, res = divmod(off0 + k, V7X_SUBLANES)
                    if res != s:
                        continue
                    start = r * rows + V7X_SUBLANES * a
                    term = win_ref[start:start + ext, cs] * wdw_ref[k:k + 1, cs]
                    p = term if p is None else p + term
                p = p[s:s + rows]
                acc = p if acc is None else acc + p
            cv_ref[r * rows:(r + 1) * rows, cs] = acc
    cv = cv_ref[...] + bdw_ref[...]
    mu = jnp.mean(cv, axis=-1, keepdims=True)
    cen = cv - mu
    var = jnp.mean(cen * cen, axis=-1, keepdims=True)
    y = cen * lax.rsqrt(var + LN_EPS) * lng_ref[...] + lnb_ref[...]
    out = _dot(jax.nn.silu(y).astype(BF16), w2b_ref[...]) + b2_ref[...]
    o_ref[...] = x_ref[...] + mod_ref[5:6, :] * out


def _conv_tail(u, x, mods, wdw, bdw, lng, lnb, w2, b2, layer, stream, j, tm):
    t = x.shape[0]
    hb = tm // CONV_HALO
    n_halo_blocks = t // CONV_HALO
    vec = _sel_spec((j,), (1, D_MODEL))
    return pl.pallas_call(
        functools.partial(_conv_tail_kernel, tm=tm),
        grid=(t // tm,),
        in_specs=[
            pl.BlockSpec((CONV_HALO, D_MODEL), lambda i: (jnp.maximum(i * hb - 1, 0), 0)),
            _tok_spec(tm),
            pl.BlockSpec((CONV_HALO, D_MODEL),
                         lambda i: (jnp.minimum((i + 1) * hb, n_halo_blocks - 1), 0)),
            _tok_spec(tm),
            _sel_spec((layer, stream), (N_MOD, D_MODEL)),
            _sel_spec((j,), (CONV_WIDTH, D_MODEL)),
            vec, vec, vec,
            _sel_spec((j,), (D_MODEL, D_MODEL)),
            vec,
        ],
        out_specs=_tok_spec(tm),
        out_shape=jax.ShapeDtypeStruct((t, D_MODEL), F32),
        scratch_shapes=[
            pltpu.VMEM((tm + 2 * CONV_HALO, D_MODEL), F32),
            pltpu.VMEM((tm, D_MODEL), F32),
            pltpu.VMEM((D_MODEL, D_MODEL), BF16),
        ],
        compiler_params=_params(1),
        name="conv_tail",
    )(u, u, u, x, mods, wdw, bdw, lng, lnb, w2, b2)


def _qkv_kernel(x_ref, mod_ref, g_ref, w_ref, b_ref, q_ref, k_ref, v_ref, wb_ref):
    _cast_once(w_ref, wb_ref)
    h = _rms_mod(x_ref[...], g_ref[...], mod_ref[3:4, :], mod_ref[4:5, :]).astype(BF16)
    d = D_MODEL
    q = _dot(h, wb_ref[:, 0:d]) + b_ref[:, 0:d]
    q_ref[...] = (q * (HEAD_DIM ** -0.5)).astype(BF16)
    k_ref[...] = _dot(h, wb_ref[:, d:2 * d]) + b_ref[:, d:2 * d]
    v_ref[...] = _dot(h, wb_ref[:, 2 * d:3 * d]) + b_ref[:, 2 * d:3 * d]


def _qkv_proj(x, mods, norm_g, w, b, layer, stream, j, tm):
    t = x.shape[0]
    return pl.pallas_call(
        _qkv_kernel,
        grid=(t // tm,),
        in_specs=[
            _tok_spec(tm),
            _sel_spec((layer, stream), (N_MOD, D_MODEL)),
            _sel_spec((layer, 1), (1, D_MODEL)),
            _sel_spec((j,), (D_MODEL, 3 * D_MODEL)),
            _sel_spec((j,), (1, 3 * D_MODEL)),
        ],
        out_specs=[_tok_spec(tm)] * 3,
        out_shape=[
            jax.ShapeDtypeStruct((t, D_MODEL), BF16),
            jax.ShapeDtypeStruct((t, D_MODEL), F32),
            jax.ShapeDtypeStruct((t, D_MODEL), F32),
        ],
        scratch_shapes=[pltpu.VMEM((D_MODEL, 3 * D_MODEL), BF16)],
        compiler_params=_params(1),
        name="qkv_proj",
    )(x, mods, norm_g, w, b)


def _head_lane_masks():
    lane = lax.broadcasted_iota(jnp.int32, (1, V7X_MXU), 1)
    return [(lane >= h * HEAD_DIM) & (lane < (h + 1) * HEAD_DIM) for h in range(NA_GROUP)]


def _stack_heads(q, masks):
    return jnp.concatenate([q * m.astype(q.dtype) for m in masks], axis=0)


def _softmax_pv(scores, values):
    mx = functools.reduce(jnp.maximum, [jnp.max(s, axis=-1, keepdims=True) for s in scores])
    ps = [jnp.exp(s - mx) for s in scores]
    denom = functools.reduce(jnp.add, [jnp.sum(p, axis=-1, keepdims=True) for p in ps])
    o = functools.reduce(jnp.add, [_dot(p.astype(BF16), v) for p, v in zip(ps, values)])
    return o / denom


def _unstack_heads(o, masks, nq):
    out = o[(NA_GROUP - 1) * nq:]
    for h in range(NA_GROUP - 2, -1, -1):
        out = jnp.where(masks[h], o[h * nq:(h + 1) * nq], out)
    return out


def _na_kernel(q_ref, kt_ref, km_ref, kb_ref, vt_ref, vm_ref, vb_ref, kc_ref, vc_ref, tab_ref, o_ref):
    masks = _head_lane_masks()
    q3 = q_ref[...].reshape(NA_Q_ROWS, GRID_W, V7X_MXU)
    kw = jnp.concatenate([kt_ref[...], km_ref[...], kb_ref[...]], axis=0)
    vw = jnp.concatenate([vt_ref[...], vm_ref[...], vb_ref[...]], axis=0)
    kw3 = kw.reshape(NA_KEY_ROWS, GRID_W, V7X_MXU)
    vw3 = vw.reshape(NA_KEY_ROWS, GRID_W, V7X_MXU)
    kc = kc_ref[...].astype(BF16)
    vc = vc_ref[...].astype(BF16)
    for n in range(GRID_W // WIN_C):
        c0 = min(max(n * WIN_C - WIN_C // 2, 0), GRID_W - NA_SPAN)
        qn = q3[:, n * WIN_C:(n + 1) * WIN_C, :].reshape(NA_NQ, V7X_MXU)
        kn = kw3[:, c0:c0 + NA_SPAN, :].reshape(NA_NK, V7X_MXU).astype(BF16)
        vn = vw3[:, c0:c0 + NA_SPAN, :].reshape(NA_NK, V7X_MXU).astype(BF16)
        qs = _stack_heads(qn, masks)
        s_lat = _dot_nt(qs, kn) + tab_ref[n]
        s_ctx = _dot_nt(qs, kc)
        o = _unstack_heads(_softmax_pv([s_lat, s_ctx], [vn, vc]), masks, NA_NQ)
        o3 = o.astype(BF16).reshape(NA_Q_ROWS, WIN_C, V7X_MXU)
        for r in range(NA_Q_ROWS):
            o_ref[r * GRID_W + n * WIN_C:r * GRID_W + (n + 1) * WIN_C, :] = o3[r]


def _na_attention(q, k, v, kc, vc, table):
    t = q.shape[0]
    tq = NA_Q_ROWS * GRID_W
    th = NA_HALO_ROWS * GRID_W
    nb = t // tq
    n_halo_blocks = t // th
    hpb = tq // th
    main = lambda: pl.BlockSpec((tq, V7X_MXU), lambda g, b: (b, g))
    top = lambda: pl.BlockSpec((th, V7X_MXU), lambda g, b: (jnp.maximum(b * hpb - 1, 0), g))
    bot = lambda: pl.BlockSpec(
        (th, V7X_MXU), lambda g, b: (jnp.minimum((b + 1) * hpb, n_halo_blocks - 1), g))
    ctx = lambda: pl.BlockSpec((kc.shape[0], V7X_MXU), lambda g, b: (0, g))

    def table_index(g, b):
        cls = jnp.where(b == 0, 0, jnp.where(b == nb - 1, 2, 1))
        return (cls, g, 0, 0, 0)

    return pl.pallas_call(
        _na_kernel,
        grid=(N_HEADS // NA_GROUP, nb),
        in_specs=[main(), top(), main(), bot(), top(), main(), bot(), ctx(), ctx(),
                  pl.BlockSpec((None, None, GRID_W // WIN_C, NA_GROUP * NA_NQ, NA_NK), table_index)],
        out_specs=main(),
        out_shape=jax.ShapeDtypeStruct((t, D_MODEL), BF16),
        compiler_params=_params(2),
        name="neighbourhood_attention",
    )(q, k, k, k, v, v, v, kc, vc, table)


def _ctx_attn_kernel(q_ref, k_ref, v_ref, o_ref):
    masks = _head_lane_masks()
    nq = q_ref.shape[0]
    qs = _stack_heads(q_ref[...], masks)
    k = k_ref[...].astype(BF16)
    v = v_ref[...].astype(BF16)
    o_ref[...] = _unstack_heads(_softmax_pv([_dot_nt(qs, k)], [v]), masks, nq).astype(BF16)


def _ctx_attention(q, k, v):
    t = q.shape[0]
    blk = pl.BlockSpec((t, V7X_MXU), lambda g: (0, g))
    return pl.pallas_call(
        _ctx_attn_kernel,
        grid=(N_HEADS // NA_GROUP,),
        in_specs=[blk, blk, blk],
        out_specs=blk,
        out_shape=jax.ShapeDtypeStruct((t, D_MODEL), BF16),
        compiler_params=_params(1),
        name="context_attention",
    )(q, k, v)


def _out_proj_kernel(o_ref, x_ref, mod_ref, w_ref, b_ref, y_ref, wb_ref):
    _cast_once(w_ref, wb_ref)
    y = _dot(o_ref[...], wb_ref[...]) + b_ref[...]
    y_ref[...] = x_ref[...] + mod_ref[5:6, :] * y


def _out_proj(o, x, mods, w, b, layer, stream, j, tm):
    t = x.shape[0]
    return pl.pallas_call(
        _out_proj_kernel,
        grid=(t // tm,),
        in_specs=[_tok_spec(tm), _tok_spec(tm),
                  _sel_spec((layer, stream), (N_MOD, D_MODEL)),
                  _sel_spec((j,), (D_MODEL, D_MODEL)),
                  _sel_spec((j,), (1, D_MODEL))],
        out_specs=_tok_spec(tm),
        out_shape=jax.ShapeDtypeStruct((t, D_MODEL), F32),
        scratch_shapes=[pltpu.VMEM((D_MODEL, D_MODEL), BF16)],
        compiler_params=_params(1),
        name="attn_out_proj",
    )(o, x, mods, w, b)


def _na_index_tables(rows):
    ncb = GRID_W // WIN_C
    n = np.arange(ncb)
    c0 = np.clip(n * WIN_C - WIN_C // 2, 0, GRID_W - NA_SPAN)
    q_col = n[:, None] * WIN_C + np.arange(WIN_C)[None, :]
    k_col = c0[:, None] + np.arange(NA_SPAN)[None, :]
    q_start = np.clip(q_col - WIN_C // 2, 0, GRID_W - WIN_C)
    col_valid = ((k_col[:, None, :] >= q_start[:, :, None])
                 & (k_col[:, None, :] < q_start[:, :, None] + WIN_C))
    col_idx = np.clip(k_col[:, None, :] - q_col[:, :, None], -(WIN_C - 1), WIN_C - 1) + WIN_C - 1
    wr = min(WIN_R, rows)
    rho = np.arange(NA_Q_ROWS)
    kappa = np.arange(NA_KEY_ROWS)
    row_valid = []
    for r_start in (0, NA_Q_ROWS, rows - NA_Q_ROWS):
        r = r_start + rho
        j = r_start - NA_HALO_ROWS + kappa
        r0 = np.clip(r - wr // 2, 0, rows - wr)
        row_valid.append((j[None, :] >= r0[:, None]) & (j[None, :] < r0[:, None] + wr))
    row_valid = np.stack(row_valid)
    valid = (row_valid[:, None, :, None, :, None] & col_valid[None, :, None, :, None, :])
    valid = valid.reshape(3, ncb, NA_Q_ROWS, WIN_C, NA_NK)
    return col_idx.astype(np.int32), valid


def _na_bias_table(rpb, rows):
    ncb = GRID_W // WIN_C
    col_idx, valid = _na_index_tables(rows)
    by_col = jnp.take(rpb, jnp.asarray(col_idx.reshape(-1)), axis=2)
    by_col = by_col.reshape(N_HEADS, 2 * WIN_R - 1, ncb, WIN_C, NA_SPAN).transpose(0, 2, 3, 1, 4)
    shift = WIN_R - 1 - NA_HALO_ROWS
    front = NA_Q_ROWS - 1 - shift
    back = NA_KEY_ROWS - 1 + shift - (2 * WIN_R - 2)
    padded = jnp.pad(by_col, ((0, 0), (0, 0), (0, 0), (front, back), (0, 0)))
    strips = [padded[:, :, :, front + shift - rho:front + shift - rho + NA_KEY_ROWS, :]
              .reshape(N_HEADS, ncb, WIN_C, NA_NK) for rho in range(NA_Q_ROWS)]
    base = jnp.stack(strips, axis=2)
    base = base.reshape(N_HEADS // NA_GROUP, NA_GROUP, ncb, NA_Q_ROWS, WIN_C, NA_NK)
    base = base.transpose(0, 2, 1, 3, 4, 5)
    table = jnp.where(jnp.asarray(valid)[:, None, :, None], base[None], MASK_VALUE)
    return table.reshape(3, N_HEADS // NA_GROUP, ncb, NA_GROUP * NA_NQ, NA_NK)


def kernel(x, c, ctx, c_ctx, mod_w, mod_b, norm_g, ffn_w_gate, ffn_w_up, ffn_w_down, conv_w_pw1, conv_b_pw1, conv_w_dw, conv_b_dw, conv_ln_g, conv_ln_b, conv_w_pw2, conv_b_pw2, na_w_qkv, na_b_qkv, na_rpb, na_w_o, na_b_o, final_g):
    b, t, d = x.shape
    assert b == 1 and d == D_MODEL and t % (NA_Q_ROWS * GRID_W) == 0
    n_ctx = ctx.shape[1]
    tm = {0: 512, 1: n_ctx}
    xs = {0: x.reshape(t, d), 1: ctx.reshape(n_ctx, d)}

    cc = jnp.zeros((V7X_SUBLANES, d), F32).at[0].set(c[0]).at[1].set(c_ctx)
    mods = _modulations(cc, mod_w, mod_b).reshape(DEPTH, V7X_SUBLANES, N_MOD, d)
    wg, wu, wd = (w.astype(BF16) for w in (ffn_w_gate, ffn_w_up, ffn_w_down))
    ng = norm_g.reshape(DEPTH, 3, 1, d)
    row = lambda v: v.reshape(v.shape[0], 1, v.shape[1])
    conv_vecs = tuple(row(v) for v in (conv_b_dw, conv_ln_g, conv_ln_b))

    for i in range(DEPTH):
        mixer, j = i % 2, i // 2
        last = i == DEPTH - 1
        for s in (0, 1):
            xs[s] = _ffn_half(xs[s], mods, ng, wg, wu, wd, i, s, 0, tm[s])
        if mixer == 0:
            for s in ((0,) if last else (0, 1)):
                u = _conv_glu(xs[s], mods, ng, conv_w_pw1, row(conv_b_pw1), i, s, j, tm[s])
                xs[s] = _conv_tail(u, xs[s], mods, conv_w_dw, *conv_vecs, conv_w_pw2, row(conv_b_pw2),
                                   i, s, j, tm[s])
        else:
            table = _na_bias_table(na_rpb[j], t // GRID_W)
            qkv = {s: _qkv_proj(xs[s], mods, ng, na_w_qkv, row(na_b_qkv), i, s, j, tm[s]) for s in (0, 1)}
            o = {0: _na_attention(*qkv[0], qkv[1][1], qkv[1][2], table)}
            if not last:
                o[1] = _ctx_attention(*qkv[1])
            for s in o:
                xs[s] = _out_proj(o[s], xs[s], mods, na_w_o, row(na_b_o), i, s, j, tm[s])
        for s in ((0,) if last else (0, 1)):
            xs[s] = _ffn_half(xs[s], mods, ng, wg, wu, wd, i, s, 1, tm[s],
                              final_g=final_g if (last and s == 0) else None)
    return xs[0].reshape(b, t, d)
```

```python
import functools

import numpy as np
import jax
import jax.numpy as jnp
from jax import lax
from jax.experimental import pallas as pl
from jax.experimental.pallas import tpu as pltpu

D_MODEL = 1024
DEPTH = 4
GRID_W = 64
N_HEADS = 16
HEAD_DIM = D_MODEL // N_HEADS
D_FF = 2816
CONV_WIDTH = 31
WIN_R = 8
WIN_C = 16
N_MOD = 9
RMS_EPS = 1e-6
LN_EPS = 1e-5
MASK_VALUE = -1e30

F32 = jnp.float32
BF16 = jnp.bfloat16

V7X_LANES = 128
V7X_SUBLANES = 8
V7X_MXU = 256
FF_CHUNK = V7X_MXU
CONV_HALO = 16
CONV_ROWS = 128
NA_Q_ROWS = 8
NA_HALO_ROWS = WIN_R // 2
NA_KEY_ROWS = NA_Q_ROWS + 2 * NA_HALO_ROWS
NA_SPAN = 2 * WIN_C
NA_GROUP = V7X_MXU // HEAD_DIM
NA_NQ = NA_Q_ROWS * WIN_C
NA_NK = NA_KEY_ROWS * NA_SPAN
VMEM_LIMIT = 56 * 1024 * 1024


def _sel_spec(prefix, tail):
    prefix, tail = tuple(prefix), tuple(tail)
    return pl.BlockSpec((None,) * len(prefix) + tail, lambda *_: prefix + (0,) * len(tail),
                        pipeline_mode=pl.Buffered(1))


def _tok_spec(tm):
    return pl.BlockSpec((tm, D_MODEL), lambda i: (i, 0))


def _params(n_grid_dims):
    return pltpu.CompilerParams(
        dimension_semantics=("arbitrary",) * n_grid_dims, vmem_limit_bytes=VMEM_LIMIT)


def _dot(a, b):
    return jnp.dot(a, b, preferred_element_type=F32)


def _dot_nt(a, b):
    return lax.dot_general(a, b, (((1,), (1,)), ((), ())), preferred_element_type=F32)


def _rmsnorm(x, g):
    return x * lax.rsqrt(jnp.mean(x * x, axis=-1, keepdims=True) + RMS_EPS) * g


def _rms_mod(x, g, shift, scale):
    return _rmsnorm(x, g) * (1.0 + scale) + shift


def _cast_once(w_ref, wb_ref):
    @pl.when(pl.program_id(0) == 0)
    def _():
        wb_ref[...] = w_ref[...].astype(BF16)


def _mod_kernel(cc_ref, w_ref, b_ref, o_ref):
    s = jax.nn.silu(cc_ref[...]).astype(BF16)
    o_ref[...] = _dot(s, w_ref[...].astype(BF16)) + b_ref[...]


def _modulations(cc, mod_w, mod_b):
    rows = cc.shape[0]
    return pl.pallas_call(
        _mod_kernel,
        grid=(DEPTH, N_MOD),
        in_specs=[
            pl.BlockSpec((rows, D_MODEL), lambda l, j: (0, 0)),
            pl.BlockSpec((None, D_MODEL, D_MODEL), lambda l, j: (l, 0, j)),
            pl.BlockSpec((None, 1, D_MODEL), lambda l, j: (l, 0, j)),
        ],
        out_specs=pl.BlockSpec((None, rows, D_MODEL), lambda l, j: (l, 0, j)),
        out_shape=jax.ShapeDtypeStruct((DEPTH, rows, N_MOD * D_MODEL), F32),
        compiler_params=_params(2),
        name="modulations",
    )(cc, mod_w, mod_b.reshape(DEPTH, 1, N_MOD * D_MODEL))


def _ffn_kernel(x_ref, mod_ref, g_ref, wg_ref, wu_ref, wd_ref, *rest, mod0, final):
    o_ref = rest[-1]
    x = x_ref[...]
    shift, scale, gate = (mod_ref[mod0 + k:mod0 + k + 1, :] for k in range(3))
    h = _rms_mod(x, g_ref[...], shift, scale).astype(BF16)
    acc = jnp.zeros(x.shape, F32)
    for f in range(D_FF // FF_CHUNK):
        sl = slice(f * FF_CHUNK, (f + 1) * FF_CHUNK)
        a = jax.nn.silu(_dot(h, wg_ref[:, sl])) * _dot(h, wu_ref[:, sl])
        acc = acc + _dot(a.astype(BF16), wd_ref[sl, :])
    out = x + 0.5 * gate * acc
    if final:
        out = _rmsnorm(out, rest[0][...])
    o_ref[...] = out


def _ffn_half(x, mods, norm_g, wg, wu, wd, layer, stream, half, tm, final_g=None):
    t = x.shape[0]
    final = final_g is not None
    in_specs = [
        _tok_spec(tm),
        _sel_spec((layer, stream), (N_MOD, D_MODEL)),
        _sel_spec((layer, 2 * half), (1, D_MODEL)),
        _sel_spec((layer, half), (D_MODEL, D_FF)),
        _sel_spec((layer, half), (D_MODEL, D_FF)),
        _sel_spec((layer, half), (D_FF, D_MODEL)),
    ]
    args = [x, mods, norm_g, wg, wu, wd]
    if final:
        in_specs.append(_sel_spec((), (1, D_MODEL)))
        args.append(final_g.reshape(1, D_MODEL))
    return pl.pallas_call(
        functools.partial(_ffn_kernel, mod0=6 * half, final=final),
        grid=(t // tm,),
        in_specs=in_specs,
        out_specs=_tok_spec(tm),
        out_shape=jax.ShapeDtypeStruct((t, D_MODEL), F32),
        compiler_params=_params(1),
        name="ffn_half",
    )(*args)


def _conv_glu_kernel(x_ref, mod_ref, g_ref, w1_ref, b1_ref, u_ref, w1b_ref):
    _cast_once(w1_ref, w1b_ref)
    h = _rms_mod(x_ref[...], g_ref[...], mod_ref[3:4, :], mod_ref[4:5, :]).astype(BF16)
    a = _dot(h, w1b_ref[:, :D_MODEL]) + b1_ref[:, :D_MODEL]
    gt = _dot(h, w1b_ref[:, D_MODEL:]) + b1_ref[:, D_MODEL:]
    u_ref[...] = a * jax.nn.sigmoid(gt)


def _conv_glu(x, mods, norm_g, w1, b1, layer, stream, j, tm):
    t = x.shape[0]
    return pl.pallas_call(
        _conv_glu_kernel,
        grid=(t // tm,),
        in_specs=[
            _tok_spec(tm),
            _sel_spec((layer, stream), (N_MOD, D_MODEL)),
            _sel_spec((layer, 1), (1, D_MODEL)),
            _sel_spec((j,), (D_MODEL, 2 * D_MODEL)),
            _sel_spec((j,), (1, 2 * D_MODEL)),
        ],
        out_specs=_tok_spec(tm),
        out_shape=jax.ShapeDtypeStruct((t, D_MODEL), F32),
        scratch_shapes=[pltpu.VMEM((D_MODEL, 2 * D_MODEL), BF16)],
        compiler_params=_params(1),
        name="conv_glu",
    )(x, mods, norm_g, w1, b1)


def _conv_tail_kernel(ul_ref, um_ref, ur_ref, x_ref, mod_ref, wdw_ref, bdw_ref, lng_ref, lnb_ref,
                      w2_ref, b2_ref, o_ref, win_ref, cv_ref, w2b_ref, *, tm):
    _cast_once(w2_ref, w2b_ref)
    i = pl.program_id(0)
    last = pl.num_programs(0) - 1
    win_ref[0:CONV_HALO, :] = jnp.where(i > 0, ul_ref[...], 0.0)
    win_ref[CONV_HALO:CONV_HALO + tm, :] = um_ref[...]
    win_ref[CONV_HALO + tm:, :] = jnp.where(i < last, ur_ref[...], 0.0)
    off0 = CONV_HALO - CONV_WIDTH // 2
    rows = min(CONV_ROWS, tm)
    ext = rows + V7X_SUBLANES
    for r in range(tm // rows):
        for c in range(D_MODEL // V7X_LANES):
            cs = slice(c * V7X_LANES, (c + 1) * V7X_LANES)
            acc = None
            for s in range(V7X_SUBLANES):
                p = None
                for k in range(CONV_WIDTH):
                    a, res = divmod(off0 + k, V7X_SUBLANES)
                    if res != s:
                        continue
                    start = r * rows + V7X_SUBLANES * a
                    term = win_ref[start:start + ext, cs] * wdw_ref[k:k + 1, cs]
                    p = term if p is None else p + term
                p = p[s:s + rows]
                acc = p if acc is None else acc + p
            cv_ref[r * rows:(r + 1) * rows, cs] = acc
    cv = cv_ref[...] + bdw_ref[...]
    mu = jnp.mean(cv, axis=-1, keepdims=True)
    cen = cv - mu
    var = jnp.mean(cen * cen, axis=-1, keepdims=True)
    y = cen * lax.rsqrt(var + LN_EPS) * lng_ref[...] + lnb_ref[...]
    out = _dot(jax.nn.silu(y).astype(BF16), w2b_ref[...]) + b2_ref[...]
    o_ref[...] = x_ref[...] + mod_ref[5:6, :] * out


def _conv_tail(u, x, mods, wdw, bdw, lng, lnb, w2, b2, layer, stream, j, tm):
    t = x.shape[0]
    hb = tm // CONV_HALO
    n_halo_blocks = t // CONV_HALO
    vec = _sel_spec((j,), (1, D_MODEL))
    return pl.pallas_call(
        functools.partial(_conv_tail_kernel, tm=tm),
        grid=(t // tm,),
        in_specs=[
            pl.BlockSpec((CONV_HALO, D_MODEL), lambda i: (jnp.maximum(i * hb - 1, 0), 0)),
            _tok_spec(tm),
            pl.BlockSpec((CONV_HALO, D_MODEL),
                         lambda i: (jnp.minimum((i + 1) * hb, n_halo_blocks - 1), 0)),
            _tok_spec(tm),
            _sel_spec((layer, stream), (N_MOD, D_MODEL)),
            _sel_spec((j,), (CONV_WIDTH, D_MODEL)),
            vec, vec, vec,
            _sel_spec((j,), (D_MODEL, D_MODEL)),
            vec,
        ],
        out_specs=_tok_spec(tm),
        out_shape=jax.ShapeDtypeStruct((t, D_MODEL), F32),
        scratch_shapes=[
            pltpu.VMEM((tm + 2 * CONV_HALO, D_MODEL), F32),
            pltpu.VMEM((tm, D_MODEL), F32),
            pltpu.VMEM((D_MODEL, D_MODEL), BF16),
        ],
        compiler_params=_params(1),
        name="conv_tail",
    )(u, u, u, x, mods, wdw, bdw, lng, lnb, w2, b2)


def _qkv_kernel(x_ref, mod_ref, g_ref, w_ref, b_ref, q_ref, k_ref, v_ref, wb_ref):
    _cast_once(w_ref, wb_ref)
    h = _rms_mod(x_ref[...], g_ref[...], mod_ref[3:4, :], mod_ref[4:5, :]).astype(BF16)
    d = D_MODEL
    q = _dot(h, wb_ref[:, 0:d]) + b_ref[:, 0:d]
    q_ref[...] = (q * (HEAD_DIM ** -0.5)).astype(BF16)
    k_ref[...] = _dot(h, wb_ref[:, d:2 * d]) + b_ref[:, d:2 * d]
    v_ref[...] = _dot(h, wb_ref[:, 2 * d:3 * d]) + b_ref[:, 2 * d:3 * d]


def _qkv_proj(x, mods, norm_g, w, b, layer, stream, j, tm):
    t = x.shape[0]
    return pl.pallas_call(
        _qkv_kernel,
        grid=(t // tm,),
        in_specs=[
            _tok_spec(tm),
            _sel_spec((layer, stream), (N_MOD, D_MODEL)),
            _sel_spec((layer, 1), (1, D_MODEL)),
            _sel_spec((j,), (D_MODEL, 3 * D_MODEL)),
            _sel_spec((j,), (1, 3 * D_MODEL)),
        ],
        out_specs=[_tok_spec(tm)] * 3,
        out_shape=[
            jax.ShapeDtypeStruct((t, D_MODEL), BF16),
            jax.ShapeDtypeStruct((t, D_MODEL), F32),
            jax.ShapeDtypeStruct((t, D_MODEL), F32),
        ],
        scratch_shapes=[pltpu.VMEM((D_MODEL, 3 * D_MODEL), BF16)],
        compiler_params=_params(1),
        name="qkv_proj",
    )(x, mods, norm_g, w, b)


def _head_lane_masks():
    lane = lax.broadcasted_iota(jnp.int32, (1, V7X_MXU), 1)
    return [(lane >= h * HEAD_DIM) & (lane < (h + 1) * HEAD_DIM) for h in range(NA_GROUP)]


def _stack_heads(q, masks):
    return jnp.concatenate([q * m.astype(q.dtype) for m in masks], axis=0)


def _softmax_pv(scores, values):
    mx = functools.reduce(jnp.maximum, [jnp.max(s, axis=-1, keepdims=True) for s in scores])
    ps = [jnp.exp(s - mx) for s in scores]
    denom = functools.reduce(jnp.add, [jnp.sum(p, axis=-1, keepdims=True) for p in ps])
    o = functools.reduce(jnp.add, [_dot(p.astype(BF16), v) for p, v in zip(ps, values)])
    return o / denom


def _unstack_heads(o, masks, nq):
    out = o[(NA_GROUP - 1) * nq:]
    for h in range(NA_GROUP - 2, -1, -1):
        out = jnp.where(masks[h], o[h * nq:(h + 1) * nq], out)
    return out


def _build_bias_tables(x_ref, rowmask_ref, tab_ref):
    top = (NA_Q_ROWS - 1) * NA_SPAN
    for n in range(GRID_W // WIN_C):
        for hh in range(NA_GROUP):
            compact = x_ref[hh, n]
            for rho in range(NA_Q_ROWS):
                strip = compact[:, top - rho * NA_SPAN:top - rho * NA_SPAN + NA_NK]
                r0 = hh * NA_NQ + rho * WIN_C
                for cls in range(3):
                    tab_ref[cls, n, r0:r0 + WIN_C, :] = jnp.where(
                        rowmask_ref[cls, rho] > 0.5, strip, MASK_VALUE)


def _na_kernel(q_ref, kt_ref, km_ref, kb_ref, vt_ref, vm_ref, vb_ref, kc_ref, vc_ref, x_ref, rowmask_ref,
               o_ref, tab_ref):
    b = pl.program_id(1)
    last = pl.num_programs(1) - 1

    @pl.when(b == 0)
    def _():
        _build_bias_tables(x_ref, rowmask_ref, tab_ref)

    cls = jnp.where(b == 0, 0, jnp.where(b == last, 2, 1))
    masks = _head_lane_masks()
    q3 = q_ref[...].reshape(NA_Q_ROWS, GRID_W, V7X_MXU)
    kw = jnp.concatenate([kt_ref[...], km_ref[...], kb_ref[...]], axis=0)
    vw = jnp.concatenate([vt_ref[...], vm_ref[...], vb_ref[...]], axis=0)
    kw3 = kw.reshape(NA_KEY_ROWS, GRID_W, V7X_MXU)
    vw3 = vw.reshape(NA_KEY_ROWS, GRID_W, V7X_MXU)
    kc = kc_ref[...].astype(BF16)
    vc = vc_ref[...].astype(BF16)
    for n in range(GRID_W // WIN_C):
        c0 = min(max(n * WIN_C - WIN_C // 2, 0), GRID_W - NA_SPAN)
        qn = q3[:, n * WIN_C:(n + 1) * WIN_C, :].reshape(NA_NQ, V7X_MXU)
        kn = kw3[:, c0:c0 + NA_SPAN, :].reshape(NA_NK, V7X_MXU).astype(BF16)
        vn = vw3[:, c0:c0 + NA_SPAN, :].reshape(NA_NK, V7X_MXU).astype(BF16)
        qs = _stack_heads(qn, masks)
        s_lat = _dot_nt(qs, kn) + tab_ref[cls, n]
        s_ctx = _dot_nt(qs, kc)
        o = _unstack_heads(_softmax_pv([s_lat, s_ctx], [vn, vc]), masks, NA_NQ)
        o3 = o.astype(BF16).reshape(NA_Q_ROWS, WIN_C, V7X_MXU)
        for r in range(NA_Q_ROWS):
            o_ref[r * GRID_W + n * WIN_C:r * GRID_W + (n + 1) * WIN_C, :] = o3[r]


def _na_attention(q, k, v, kc, vc, compact, rowmask):
    t = q.shape[0]
    tq = NA_Q_ROWS * GRID_W
    th = NA_HALO_ROWS * GRID_W
    nb = t // tq
    n_halo_blocks = t // th
    hpb = tq // th
    ncb = GRID_W // WIN_C
    main = lambda: pl.BlockSpec((tq, V7X_MXU), lambda g, b: (b, g))
    top = lambda: pl.BlockSpec((th, V7X_MXU), lambda g, b: (jnp.maximum(b * hpb - 1, 0), g))
    bot = lambda: pl.BlockSpec(
        (th, V7X_MXU), lambda g, b: (jnp.minimum((b + 1) * hpb, n_halo_blocks - 1), g))
    ctx = lambda: pl.BlockSpec((kc.shape[0], V7X_MXU), lambda g, b: (0, g))
    return pl.pallas_call(
        _na_kernel,
        grid=(N_HEADS // NA_GROUP, nb),
        in_specs=[main(), top(), main(), bot(), top(), main(), bot(), ctx(), ctx(),
                  pl.BlockSpec((NA_GROUP,) + compact.shape[1:], lambda g, b: (g, 0, 0, 0)),
                  pl.BlockSpec(rowmask.shape, lambda g, b: (0, 0, 0, 0))],
        out_specs=main(),
        out_shape=jax.ShapeDtypeStruct((t, D_MODEL), BF16),
        scratch_shapes=[pltpu.VMEM((3, ncb, NA_GROUP * NA_NQ, NA_NK), F32)],
        compiler_params=_params(2),
        name="neighbourhood_attention",
    )(q, k, k, k, v, v, v, kc, vc, compact, rowmask)


def _ctx_attn_kernel(q_ref, k_ref, v_ref, o_ref):
    masks = _head_lane_masks()
    nq = q_ref.shape[0]
    qs = _stack_heads(q_ref[...], masks)
    k = k_ref[...].astype(BF16)
    v = v_ref[...].astype(BF16)
    o_ref[...] = _unstack_heads(_softmax_pv([_dot_nt(qs, k)], [v]), masks, nq).astype(BF16)


def _ctx_attention(q, k, v):
    t = q.shape[0]
    blk = pl.BlockSpec((t, V7X_MXU), lambda g: (0, g))
    return pl.pallas_call(
        _ctx_attn_kernel,
        grid=(N_HEADS // NA_GROUP,),
        in_specs=[blk, blk, blk],
        out_specs=blk,
        out_shape=jax.ShapeDtypeStruct((t, D_MODEL), BF16),
        compiler_params=_params(1),
        name="context_attention",
    )(q, k, v)


def _out_proj_kernel(o_ref, x_ref, mod_ref, w_ref, b_ref, y_ref, wb_ref):
    _cast_once(w_ref, wb_ref)
    y = _dot(o_ref[...], wb_ref[...]) + b_ref[...]
    y_ref[...] = x_ref[...] + mod_ref[5:6, :] * y


def _out_proj(o, x, mods, w, b, layer, stream, j, tm):
    t = x.shape[0]
    return pl.pallas_call(
        _out_proj_kernel,
        grid=(t // tm,),
        in_specs=[_tok_spec(tm), _tok_spec(tm),
                  _sel_spec((layer, stream), (N_MOD, D_MODEL)),
                  _sel_spec((j,), (D_MODEL, D_MODEL)),
                  _sel_spec((j,), (1, D_MODEL))],
        out_specs=_tok_spec(tm),
        out_shape=jax.ShapeDtypeStruct((t, D_MODEL), F32),
        scratch_shapes=[pltpu.VMEM((D_MODEL, D_MODEL), BF16)],
        compiler_params=_params(1),
        name="attn_out_proj",
    )(o, x, mods, w, b)


def _na_index_tables(rows):
    ncb = GRID_W // WIN_C
    n = np.arange(ncb)
    c0 = np.clip(n * WIN_C - WIN_C // 2, 0, GRID_W - NA_SPAN)
    q_col = n[:, None] * WIN_C + np.arange(WIN_C)[None, :]
    k_col = c0[:, None] + np.arange(NA_SPAN)[None, :]
    q_start = np.clip(q_col - WIN_C // 2, 0, GRID_W - WIN_C)
    col_valid = ((k_col[:, None, :] >= q_start[:, :, None])
                 & (k_col[:, None, :] < q_start[:, :, None] + WIN_C))
    col_idx = np.clip(k_col[:, None, :] - q_col[:, :, None], -(WIN_C - 1), WIN_C - 1) + WIN_C - 1
    wr = min(WIN_R, rows)
    rho = np.arange(NA_Q_ROWS)
    kappa = np.arange(NA_KEY_ROWS)
    row_valid = []
    for r_start in (0, NA_Q_ROWS, rows - NA_Q_ROWS):
        r = r_start + rho
        j = r_start - NA_HALO_ROWS + kappa
        r0 = np.clip(r - wr // 2, 0, rows - wr)
        row_valid.append((j[None, :] >= r0[:, None]) & (j[None, :] < r0[:, None] + wr))
    row_valid = np.repeat(np.stack(row_valid), NA_SPAN, axis=-1)
    rowmask = row_valid.astype(np.float32).reshape(3, NA_Q_ROWS, 1, NA_NK)
    return col_idx.astype(np.int32), col_valid, rowmask


def _na_compact_bias(rpb, col_idx, col_valid):
    ncb = GRID_W // WIN_C
    by_col = jnp.take(rpb, jnp.asarray(col_idx.reshape(-1)), axis=2)
    by_col = by_col.reshape(N_HEADS, 2 * WIN_R - 1, ncb, WIN_C, NA_SPAN).transpose(0, 2, 3, 1, 4)
    by_col = jnp.where(jnp.asarray(col_valid)[None, :, :, None, :], by_col, MASK_VALUE)
    shift = WIN_R - 1 - NA_HALO_ROWS
    front = NA_Q_ROWS - 1 - shift
    total = NA_Q_ROWS + NA_KEY_ROWS
    back = total - front - (2 * WIN_R - 1)
    padded = jnp.pad(by_col, ((0, 0), (0, 0), (0, 0), (front, back), (0, 0)))
    return padded.reshape(N_HEADS, ncb, WIN_C, total * NA_SPAN)


def kernel(x, c, ctx, c_ctx, mod_w, mod_b, norm_g, ffn_w_gate, ffn_w_up, ffn_w_down, conv_w_pw1, conv_b_pw1, conv_w_dw, conv_b_dw, conv_ln_g, conv_ln_b, conv_w_pw2, conv_b_pw2, na_w_qkv, na_b_qkv, na_rpb, na_w_o, na_b_o, final_g):
    b, t, d = x.shape
    assert b == 1 and d == D_MODEL and t % (NA_Q_ROWS * GRID_W) == 0
    n_ctx = ctx.shape[1]
    tm = {0: 512, 1: n_ctx}
    tm_ffn = {0: 1024, 1: n_ctx}
    col_idx, col_valid, rowmask = _na_index_tables(t // GRID_W)
    rowmask = jnp.asarray(rowmask)
    xs = {0: x.reshape(t, d), 1: ctx.reshape(n_ctx, d)}

    cc = jnp.zeros((V7X_SUBLANES, d), F32).at[0].set(c[0]).at[1].set(c_ctx)
    mods = _modulations(cc, mod_w, mod_b).reshape(DEPTH, V7X_SUBLANES, N_MOD, d)
    wg, wu, wd = (w.astype(BF16) for w in (ffn_w_gate, ffn_w_up, ffn_w_down))
    ng = norm_g.reshape(DEPTH, 3, 1, d)
    row = lambda v: v.reshape(v.shape[0], 1, v.shape[1])
    conv_vecs = tuple(row(v) for v in (conv_b_dw, conv_ln_g, conv_ln_b))

    for i in range(DEPTH):
        mixer, j = i % 2, i // 2
        last = i == DEPTH - 1
        for s in (0, 1):
            xs[s] = _ffn_half(xs[s], mods, ng, wg, wu, wd, i, s, 0, tm_ffn[s])
        if mixer == 0:
            for s in ((0,) if last else (0, 1)):
                u = _conv_glu(xs[s], mods, ng, conv_w_pw1, row(conv_b_pw1), i, s, j, tm[s])
                xs[s] = _conv_tail(u, xs[s], mods, conv_w_dw, *conv_vecs, conv_w_pw2, row(conv_b_pw2),
                                   i, s, j, tm[s])
        else:
            compact = _na_compact_bias(na_rpb[j], col_idx, col_valid)
            qkv = {s: _qkv_proj(xs[s], mods, ng, na_w_qkv, row(na_b_qkv), i, s, j, tm[s]) for s in (0, 1)}
            o = {0: _na_attention(*qkv[0], qkv[1][1], qkv[1][2], compact, rowmask)}
            if not last:
                o[1] = _ctx_attention(*qkv[1])
            for s in o:
                xs[s] = _out_proj(o[s], xs[s], mods, na_w_o, row(na_b_o), i, s, j, tm[s])
        for s in ((0,) if last else (0, 1)):
            xs[s] = _ffn_half(xs[s], mods, ng, wg, wu, wd, i, s, 1, tm_ffn[s],
                              final_g=final_g if (last and s == 0) else None)
    return xs[0].reshape(b, t, d)
```

```python
import functools

import numpy as np
import jax
import jax.numpy as jnp
from jax import lax
from jax.experimental import pallas as pl
from jax.experimental.pallas import tpu as pltpu

D_MODEL = 1024
DEPTH = 4
GRID_W = 64
N_HEADS = 16
HEAD_DIM = D_MODEL // N_HEADS
D_FF = 2816
CONV_WIDTH = 31
WIN_R = 8
WIN_C = 16
N_MOD = 9
RMS_EPS = 1e-6
LN_EPS = 1e-5
MASK_VALUE = -1e30

F32 = jnp.float32
BF16 = jnp.bfloat16

V7X_LANES = 128
V7X_SUBLANES = 8
V7X_MXU = 256
FF_CHUNK = V7X_MXU
FFN_PREP_PIECES = 8
QKV_PREP_PIECES = 8
CONV_HALO = 16
CONV_ROWS = 128
NA_Q_ROWS = 8
NA_STEP_BLOCKS = 2
NA_HALO_ROWS = WIN_R // 2
NA_KEY_ROWS = NA_Q_ROWS + 2 * NA_HALO_ROWS
NA_SPAN = 2 * WIN_C
NA_GROUP = V7X_MXU // HEAD_DIM
NA_NQ = NA_Q_ROWS * WIN_C
NA_NK = NA_KEY_ROWS * NA_SPAN
VMEM_LIMIT = 56 * 1024 * 1024


def _sel_spec(prefix, tail):
    prefix, tail = tuple(prefix), tuple(tail)
    return pl.BlockSpec((None,) * len(prefix) + tail, lambda *_: prefix + (0,) * len(tail),
                        pipeline_mode=pl.Buffered(1))


def _tok_spec(tm):
    return pl.BlockSpec((tm, D_MODEL), lambda i: (i, 0))


def _params(n_grid_dims):
    return pltpu.CompilerParams(
        dimension_semantics=("arbitrary",) * n_grid_dims, vmem_limit_bytes=VMEM_LIMIT)


def _dot(a, b):
    return jnp.dot(a, b, preferred_element_type=F32)


def _dot_nt(a, b):
    return lax.dot_general(a, b, (((1,), (1,)), ((), ())), preferred_element_type=F32)


def _rmsnorm(x, g):
    return x * lax.rsqrt(jnp.mean(x * x, axis=-1, keepdims=True) + RMS_EPS) * g


def _rms_mod(x, g, shift, scale):
    return _rmsnorm(x, g) * (1.0 + scale) + shift


def _cast_once(w_ref, wb_ref):
    @pl.when(pl.program_id(0) == 0)
    def _():
        wb_ref[...] = w_ref[...].astype(BF16)


def _next_tok_spec(tm, n_blocks):
    return pl.BlockSpec((tm, D_MODEL), lambda i: (jnp.minimum(i + 1, n_blocks - 1), 0))


def _prime_norm(x_ref, h_ref, norm):
    i = pl.program_id(0)

    @pl.when(i == 0)
    def _():
        h_ref[0] = norm(x_ref[...])

    return i % 2


def _prep_next(xn_ref, h_ref, slot, k, n_pieces, norm):
    piece = xn_ref.shape[0] // n_pieces
    rows = slice(k * piece, (k + 1) * piece)
    h_ref[1 - slot, rows, :] = norm(xn_ref[rows, :])


def _mod_kernel(cc_ref, w_ref, b_ref, o_ref):
    s = jax.nn.silu(cc_ref[...]).astype(BF16)
    o_ref[...] = _dot(s, w_ref[...].astype(BF16)) + b_ref[...]


def _modulations(cc, mod_w, mod_b):
    rows = cc.shape[0]
    return pl.pallas_call(
        _mod_kernel,
        grid=(DEPTH, N_MOD),
        in_specs=[
            pl.BlockSpec((rows, D_MODEL), lambda l, j: (0, 0)),
            pl.BlockSpec((None, D_MODEL, D_MODEL), lambda l, j: (l, 0, j)),
            pl.BlockSpec((None, 1, D_MODEL), lambda l, j: (l, 0, j)),
        ],
        out_specs=pl.BlockSpec((None, rows, D_MODEL), lambda l, j: (l, 0, j)),
        out_shape=jax.ShapeDtypeStruct((DEPTH, rows, N_MOD * D_MODEL), F32),
        compiler_params=_params(2),
        name="modulations",
    )(cc, mod_w, mod_b.reshape(DEPTH, 1, N_MOD * D_MODEL))


def _ffn_kernel(x_ref, xn_ref, mod_ref, g_ref, wg_ref, wu_ref, wd_ref, *rest, mod0, final):
    o_ref, h_ref = rest[-2:]
    shift, scale, gate = (mod_ref[mod0 + k:mod0 + k + 1, :] for k in range(3))
    norm = lambda v: _rms_mod(v, g_ref[...], shift, scale).astype(BF16)
    slot = _prime_norm(x_ref, h_ref, norm)
    x = x_ref[...]
    acc = jnp.zeros(x.shape, F32)
    for f in range(D_FF // FF_CHUNK):
        sl = slice(f * FF_CHUNK, (f + 1) * FF_CHUNK)
        h = h_ref[slot]
        a = jax.nn.silu(_dot(h, wg_ref[:, sl])) * _dot(h, wu_ref[:, sl])
        acc = acc + _dot(a.astype(BF16), wd_ref[sl, :])
        if f < FFN_PREP_PIECES:
            _prep_next(xn_ref, h_ref, slot, f, FFN_PREP_PIECES, norm)
    out = x + 0.5 * gate * acc
    if final:
        out = _rmsnorm(out, rest[0][...])
    o_ref[...] = out


def _ffn_half(x, mods, norm_g, wg, wu, wd, layer, stream, half, tm, final_g=None):
    t = x.shape[0]
    final = final_g is not None
    n_blocks = t // tm
    in_specs = [
        _tok_spec(tm),
        _next_tok_spec(tm, n_blocks),
        _sel_spec((layer, stream), (N_MOD, D_MODEL)),
        _sel_spec((layer, 2 * half), (1, D_MODEL)),
        _sel_spec((layer, half), (D_MODEL, D_FF)),
        _sel_spec((layer, half), (D_MODEL, D_FF)),
        _sel_spec((layer, half), (D_FF, D_MODEL)),
    ]
    args = [x, x, mods, norm_g, wg, wu, wd]
    if final:
        in_specs.append(_sel_spec((), (1, D_MODEL)))
        args.append(final_g.reshape(1, D_MODEL))
    return pl.pallas_call(
        functools.partial(_ffn_kernel, mod0=6 * half, final=final),
        grid=(t // tm,),
        in_specs=in_specs,
        out_specs=_tok_spec(tm),
        out_shape=jax.ShapeDtypeStruct((t, D_MODEL), F32),
        scratch_shapes=[pltpu.VMEM((2, tm, D_MODEL), BF16)],
        compiler_params=_params(1),
        name="ffn_half",
    )(*args)


def _conv_glu_kernel(x_ref, xn_ref, mod_ref, g_ref, w1_ref, b1_ref, u_ref, w1b_ref, h_ref):
    _cast_once(w1_ref, w1b_ref)
    norm = lambda v: _rms_mod(v, g_ref[...], mod_ref[3:4, :], mod_ref[4:5, :]).astype(BF16)
    slot = _prime_norm(x_ref, h_ref, norm)
    n_chunks = D_MODEL // V7X_MXU
    for c in range(n_chunks):
        lo = slice(c * V7X_MXU, (c + 1) * V7X_MXU)
        hi = slice(D_MODEL + c * V7X_MXU, D_MODEL + (c + 1) * V7X_MXU)
        h = h_ref[slot]
        a = _dot(h, w1b_ref[:, lo]) + b1_ref[:, lo]
        gt = _dot(h, w1b_ref[:, hi]) + b1_ref[:, hi]
        u_ref[:, lo] = a * jax.nn.sigmoid(gt)
        _prep_next(xn_ref, h_ref, slot, c, n_chunks, norm)


def _conv_glu(x, mods, norm_g, w1, b1, layer, stream, j, tm):
    t = x.shape[0]
    return pl.pallas_call(
        _conv_glu_kernel,
        grid=(t // tm,),
        in_specs=[
            _tok_spec(tm),
            _next_tok_spec(tm, t // tm),
            _sel_spec((layer, stream), (N_MOD, D_MODEL)),
            _sel_spec((layer, 1), (1, D_MODEL)),
            _sel_spec((j,), (D_MODEL, 2 * D_MODEL)),
            _sel_spec((j,), (1, 2 * D_MODEL)),
        ],
        out_specs=_tok_spec(tm),
        out_shape=jax.ShapeDtypeStruct((t, D_MODEL), F32),
        scratch_shapes=[pltpu.VMEM((D_MODEL, 2 * D_MODEL), BF16), pltpu.VMEM((2, tm, D_MODEL), BF16)],
        compiler_params=_params(1),
        name="conv_glu",
    )(x, x, mods, norm_g, w1, b1)


def _conv_tail_kernel(ul_ref, um_ref, ur_ref, x_ref, mod_ref, wdw_ref, bdw_ref, lng_ref, lnb_ref,
                      w2_ref, b2_ref, o_ref, win_ref, cv_ref, w2b_ref, *, tm):
    _cast_once(w2_ref, w2b_ref)
    i = pl.program_id(0)
    last = pl.num_programs(0) - 1
    win_ref[0:CONV_HALO, :] = jnp.where(i > 0, ul_ref[...], 0.0)
    win_ref[CONV_HALO:CONV_HALO + tm, :] = um_ref[...]
    win_ref[CONV_HALO + tm:, :] = jnp.where(i < last, ur_ref[...], 0.0)
    off0 = CONV_HALO - CONV_WIDTH // 2
    rows = min(CONV_ROWS, tm)
    ext = rows + V7X_SUBLANES
    for r in range(tm // rows):
        for c in range(D_MODEL // V7X_LANES):
            cs = slice(c * V7X_LANES, (c + 1) * V7X_LANES)
            acc = None
            for s in range(V7X_SUBLANES):
                p = None
                for k in range(CONV_WIDTH):
                    a, res = divmod(off0 + k, V7X_SUBLANES)
                    if res != s:
                        continue
                    start = r * rows + V7X_SUBLANES * a
                    term = win_ref[start:start + ext, cs] * wdw_ref[k:k + 1, cs]
                    p = term if p is None else p + term
                p = p[s:s + rows]
                acc = p if acc is None else acc + p
            cv_ref[r * rows:(r + 1) * rows, cs] = acc
    cv = cv_ref[...] + bdw_ref[...]
    mu = jnp.mean(cv, axis=-1, keepdims=True)
    cen = cv - mu
    var = jnp.mean(cen * cen, axis=-1, keepdims=True)
    y = cen * lax.rsqrt(var + LN_EPS) * lng_ref[...] + lnb_ref[...]
    out = _dot(jax.nn.silu(y).astype(BF16), w2b_ref[...]) + b2_ref[...]
    o_ref[...] = x_ref[...] + mod_ref[5:6, :] * out


def _conv_tail(u, x, mods, wdw, bdw, lng, lnb, w2, b2, layer, stream, j, tm):
    t = x.shape[0]
    hb = tm // CONV_HALO
    n_halo_blocks = t // CONV_HALO
    vec = _sel_spec((j,), (1, D_MODEL))
    return pl.pallas_call(
        functools.partial(_conv_tail_kernel, tm=tm),
        grid=(t // tm,),
        in_specs=[
            pl.BlockSpec((CONV_HALO, D_MODEL), lambda i: (jnp.maximum(i * hb - 1, 0), 0)),
            _tok_spec(tm),
            pl.BlockSpec((CONV_HALO, D_MODEL),
                         lambda i: (jnp.minimum((i + 1) * hb, n_halo_blocks - 1), 0)),
            _tok_spec(tm),
            _sel_spec((layer, stream), (N_MOD, D_MODEL)),
            _sel_spec((j,), (CONV_WIDTH, D_MODEL)),
            vec, vec, vec,
            _sel_spec((j,), (D_MODEL, D_MODEL)),
            vec,
        ],
        out_specs=_tok_spec(tm),
        out_shape=jax.ShapeDtypeStruct((t, D_MODEL), F32),
        scratch_shapes=[
            pltpu.VMEM((tm + 2 * CONV_HALO, D_MODEL), F32),
            pltpu.VMEM((tm, D_MODEL), F32),
            pltpu.VMEM((D_MODEL, D_MODEL), BF16),
        ],
        compiler_params=_params(1),
        name="conv_tail",
    )(u, u, u, x, mods, wdw, bdw, lng, lnb, w2, b2)


def _qkv_kernel(x_ref, xn_ref, mod_ref, g_ref, w_ref, b_ref, q_ref, k_ref, v_ref, wb_ref, h_ref):
    _cast_once(w_ref, wb_ref)
    norm = lambda v: _rms_mod(v, g_ref[...], mod_ref[3:4, :], mod_ref[4:5, :]).astype(BF16)
    slot = _prime_norm(x_ref, h_ref, norm)
    per_out = D_MODEL // V7X_MXU
    for c in range(3 * per_out):
        cols = slice(c * V7X_MXU, (c + 1) * V7X_MXU)
        y = _dot(h_ref[slot], wb_ref[:, cols]) + b_ref[:, cols]
        which, part = divmod(c, per_out)
        out_cols = slice(part * V7X_MXU, (part + 1) * V7X_MXU)
        if which == 0:
            q_ref[:, out_cols] = (y * (HEAD_DIM ** -0.5)).astype(BF16)
        elif which == 1:
            k_ref[:, out_cols] = y
        else:
            v_ref[:, out_cols] = y
        if c < QKV_PREP_PIECES:
            _prep_next(xn_ref, h_ref, slot, c, QKV_PREP_PIECES, norm)


def _qkv_proj(x, mods, norm_g, w, b, layer, stream, j, tm):
    t = x.shape[0]
    return pl.pallas_call(
        _qkv_kernel,
        grid=(t // tm,),
        in_specs=[
            _tok_spec(tm),
            _next_tok_spec(tm, t // tm),
            _sel_spec((layer, stream), (N_MOD, D_MODEL)),
            _sel_spec((layer, 1), (1, D_MODEL)),
            _sel_spec((j,), (D_MODEL, 3 * D_MODEL)),
            _sel_spec((j,), (1, 3 * D_MODEL)),
        ],
        out_specs=[_tok_spec(tm)] * 3,
        out_shape=[
            jax.ShapeDtypeStruct((t, D_MODEL), BF16),
            jax.ShapeDtypeStruct((t, D_MODEL), F32),
            jax.ShapeDtypeStruct((t, D_MODEL), F32),
        ],
        scratch_shapes=[pltpu.VMEM((D_MODEL, 3 * D_MODEL), BF16), pltpu.VMEM((2, tm, D_MODEL), BF16)],
        compiler_params=_params(1),
        name="qkv_proj",
    )(x, x, mods, norm_g, w, b)


def _head_lane_masks():
    lane = lax.broadcasted_iota(jnp.int32, (1, V7X_MXU), 1)
    return [(lane >= h * HEAD_DIM) & (lane < (h + 1) * HEAD_DIM) for h in range(NA_GROUP)]


def _stack_heads(q, masks):
    return jnp.concatenate([q * m.astype(q.dtype) for m in masks], axis=0)


def _softmax_pv(scores, values):
    mx = functools.reduce(jnp.maximum, [jnp.max(s, axis=-1, keepdims=True) for s in scores])
    ps = [jnp.exp(s - mx) for s in scores]
    denom = functools.reduce(jnp.add, [jnp.sum(p, axis=-1, keepdims=True) for p in ps])
    o = functools.reduce(jnp.add, [_dot(p.astype(BF16), v) for p, v in zip(ps, values)])
    return o / denom


def _unstack_heads(o, masks, nq):
    out = o[(NA_GROUP - 1) * nq:]
    for h in range(NA_GROUP - 2, -1, -1):
        out = jnp.where(masks[h], o[h * nq:(h + 1) * nq], out)
    return out


def _build_bias_tables(x_ref, rowmask_ref, tab_ref):
    top = (NA_Q_ROWS - 1) * NA_SPAN
    for n in range(GRID_W // WIN_C):
        for hh in range(NA_GROUP):
            compact = x_ref[hh, n]
            for rho in range(NA_Q_ROWS):
                strip = compact[:, top - rho * NA_SPAN:top - rho * NA_SPAN + NA_NK]
                r0 = hh * NA_NQ + rho * WIN_C
                for cls in range(3):
                    tab_ref[cls, n, r0:r0 + WIN_C, :] = jnp.where(
                        rowmask_ref[cls, rho] > 0.5, strip, MASK_VALUE)


def _na_kernel(q_ref, kt_ref, km_ref, kb_ref, vt_ref, vm_ref, vb_ref, kc_ref, vc_ref, x_ref, rowmask_ref,
               o_ref, tab_ref):
    b = pl.program_id(1)
    last = pl.num_programs(1) - 1

    @pl.when(b == 0)
    def _():
        _build_bias_tables(x_ref, rowmask_ref, tab_ref)

    masks = _head_lane_masks()
    step_rows = NA_STEP_BLOCKS * NA_Q_ROWS
    q3 = q_ref[...].reshape(step_rows, GRID_W, V7X_MXU)
    kw = jnp.concatenate([kt_ref[...], km_ref[...], kb_ref[...]], axis=0)
    vw = jnp.concatenate([vt_ref[...], vm_ref[...], vb_ref[...]], axis=0)
    kw3 = kw.reshape(step_rows + 2 * NA_HALO_ROWS, GRID_W, V7X_MXU)
    vw3 = vw.reshape(step_rows + 2 * NA_HALO_ROWS, GRID_W, V7X_MXU)
    kc = kc_ref[...].astype(BF16)
    vc = vc_ref[...].astype(BF16)
    for sb in range(NA_STEP_BLOCKS):
        cls = 1
        if sb == 0:
            cls = jnp.where(b == 0, 0, cls)
        if sb == NA_STEP_BLOCKS - 1:
            cls = jnp.where(b == last, 2, cls)
        r_lo = sb * NA_Q_ROWS
        for n in range(GRID_W // WIN_C):
            c0 = min(max(n * WIN_C - WIN_C // 2, 0), GRID_W - NA_SPAN)
            qn = q3[r_lo:r_lo + NA_Q_ROWS, n * WIN_C:(n + 1) * WIN_C, :].reshape(NA_NQ, V7X_MXU)
            kn = kw3[r_lo:r_lo + NA_KEY_ROWS, c0:c0 + NA_SPAN, :].reshape(NA_NK, V7X_MXU).astype(BF16)
            vn = vw3[r_lo:r_lo + NA_KEY_ROWS, c0:c0 + NA_SPAN, :].reshape(NA_NK, V7X_MXU).astype(BF16)
            qs = _stack_heads(qn, masks)
            s_lat = _dot_nt(qs, kn) + tab_ref[cls, n]
            s_ctx = _dot_nt(qs, kc)
            o = _unstack_heads(_softmax_pv([s_lat, s_ctx], [vn, vc]), masks, NA_NQ)
            o3 = o.astype(BF16).reshape(NA_Q_ROWS, WIN_C, V7X_MXU)
            for r in range(NA_Q_ROWS):
                row0 = (r_lo + r) * GRID_W + n * WIN_C
                o_ref[row0:row0 + WIN_C, :] = o3[r]


def _na_attention(q, k, v, kc, vc, compact, rowmask):
    t = q.shape[0]
    tq = NA_STEP_BLOCKS * NA_Q_ROWS * GRID_W
    th = NA_HALO_ROWS * GRID_W
    nb = t // tq
    n_halo_blocks = t // th
    hpb = tq // th
    ncb = GRID_W // WIN_C
    main = lambda: pl.BlockSpec((tq, V7X_MXU), lambda g, b: (b, g))
    top = lambda: pl.BlockSpec((th, V7X_MXU), lambda g, b: (jnp.maximum(b * hpb - 1, 0), g))
    bot = lambda: pl.BlockSpec(
        (th, V7X_MXU), lambda g, b: (jnp.minimum((b + 1) * hpb, n_halo_blocks - 1), g))
    ctx = lambda: pl.BlockSpec((kc.shape[0], V7X_MXU), lambda g, b: (0, g))
    return pl.pallas_call(
        _na_kernel,
        grid=(N_HEADS // NA_GROUP, nb),
        in_specs=[main(), top(), main(), bot(), top(), main(), bot(), ctx(), ctx(),
                  pl.BlockSpec((NA_GROUP,) + compact.shape[1:], lambda g, b: (g, 0, 0, 0)),
                  pl.BlockSpec(rowmask.shape, lambda g, b: (0, 0, 0, 0))],
        out_specs=main(),
        out_shape=jax.ShapeDtypeStruct((t, D_MODEL), BF16),
        scratch_shapes=[pltpu.VMEM((3, ncb, NA_GROUP * NA_NQ, NA_NK), F32)],
        compiler_params=_params(2),
        name="neighbourhood_attention",
    )(q, k, k, k, v, v, v, kc, vc, compact, rowmask)


def _ctx_attn_kernel(q_ref, k_ref, v_ref, o_ref):
    masks = _head_lane_masks()
    nq = q_ref.shape[0]
    qs = _stack_heads(q_ref[...], masks)
    k = k_ref[...].astype(BF16)
    v = v_ref[...].astype(BF16)
    o_ref[...] = _unstack_heads(_softmax_pv([_dot_nt(qs, k)], [v]), masks, nq).astype(BF16)


def _ctx_attention(q, k, v):
    t = q.shape[0]
    blk = pl.BlockSpec((t, V7X_MXU), lambda g: (0, g))
    return pl.pallas_call(
        _ctx_attn_kernel,
        grid=(N_HEADS // NA_GROUP,),
        in_specs=[blk, blk, blk],
        out_specs=blk,
        out_shape=jax.ShapeDtypeStruct((t, D_MODEL), BF16),
        compiler_params=_params(1),
        name="context_attention",
    )(q, k, v)


def _out_proj_kernel(o_ref, x_ref, mod_ref, w_ref, b_ref, y_ref, wb_ref):
    _cast_once(w_ref, wb_ref)
    y = _dot(o_ref[...], wb_ref[...]) + b_ref[...]
    y_ref[...] = x_ref[...] + mod_ref[5:6, :] * y


def _out_proj(o, x, mods, w, b, layer, stream, j, tm):
    t = x.shape[0]
    return pl.pallas_call(
        _out_proj_kernel,
        grid=(t // tm,),
        in_specs=[_tok_spec(tm), _tok_spec(tm),
                  _sel_spec((layer, stream), (N_MOD, D_MODEL)),
                  _sel_spec((j,), (D_MODEL, D_MODEL)),
                  _sel_spec((j,), (1, D_MODEL))],
        out_specs=_tok_spec(tm),
        out_shape=jax.ShapeDtypeStruct((t, D_MODEL), F32),
        scratch_shapes=[pltpu.VMEM((D_MODEL, D_MODEL), BF16)],
        compiler_params=_params(1),
        name="attn_out_proj",
    )(o, x, mods, w, b)


def _na_index_tables(rows):
    ncb = GRID_W // WIN_C
    n = np.arange(ncb)
    c0 = np.clip(n * WIN_C - WIN_C // 2, 0, GRID_W - NA_SPAN)
    q_col = n[:, None] * WIN_C + np.arange(WIN_C)[None, :]
    k_col = c0[:, None] + np.arange(NA_SPAN)[None, :]
    q_start = np.clip(q_col - WIN_C // 2, 0, GRID_W - WIN_C)
    col_valid = ((k_col[:, None, :] >= q_start[:, :, None])
                 & (k_col[:, None, :] < q_start[:, :, None] + WIN_C))
    col_idx = np.clip(k_col[:, None, :] - q_col[:, :, None], -(WIN_C - 1), WIN_C - 1) + WIN_C - 1
    wr = min(WIN_R, rows)
    rho = np.arange(NA_Q_ROWS)
    kappa = np.arange(NA_KEY_ROWS)
    row_valid = []
    for r_start in (0, NA_Q_ROWS, rows - NA_Q_ROWS):
        r = r_start + rho
        j = r_start - NA_HALO_ROWS + kappa
        r0 = np.clip(r - wr // 2, 0, rows - wr)
        row_valid.append((j[None, :] >= r0[:, None]) & (j[None, :] < r0[:, None] + wr))
    row_valid = np.repeat(np.stack(row_valid), NA_SPAN, axis=-1)
    rowmask = row_valid.astype(np.float32).reshape(3, NA_Q_ROWS, 1, NA_NK)
    return col_idx.astype(np.int32), col_valid, rowmask


def _na_compact_bias(rpb, col_idx, col_valid):
    ncb = GRID_W // WIN_C
    by_col = jnp.take(rpb, jnp.asarray(col_idx.reshape(-1)), axis=2)
    by_col = by_col.reshape(N_HEADS, 2 * WIN_R - 1, ncb, WIN_C, NA_SPAN).transpose(0, 2, 3, 1, 4)
    by_col = jnp.where(jnp.asarray(col_valid)[None, :, :, None, :], by_col, MASK_VALUE)
    shift = WIN_R - 1 - NA_HALO_ROWS
    front = NA_Q_ROWS - 1 - shift
    total = NA_Q_ROWS + NA_KEY_ROWS
    back = total - front - (2 * WIN_R - 1)
    padded = jnp.pad(by_col, ((0, 0), (0, 0), (0, 0), (front, back), (0, 0)))
    return padded.reshape(N_HEADS, ncb, WIN_C, total * NA_SPAN)


def kernel(x, c, ctx, c_ctx, mod_w, mod_b, norm_g, ffn_w_gate, ffn_w_up, ffn_w_down, conv_w_pw1, conv_b_pw1, conv_w_dw, conv_b_dw, conv_ln_g, conv_ln_b, conv_w_pw2, conv_b_pw2, na_w_qkv, na_b_qkv, na_rpb, na_w_o, na_b_o, final_g):
    b, t, d = x.shape
    assert b == 1 and d == D_MODEL and t % (NA_STEP_BLOCKS * NA_Q_ROWS * GRID_W) == 0
    n_ctx = ctx.shape[1]
    tm = {0: 512, 1: n_ctx}
    tm_ffn = tm
    col_idx, col_valid, rowmask = _na_index_tables(t // GRID_W)
    rowmask = jnp.asarray(rowmask)
    xs = {0: x.reshape(t, d), 1: ctx.reshape(n_ctx, d)}

    cc = jnp.zeros((V7X_SUBLANES, d), F32).at[0].set(c[0]).at[1].set(c_ctx)
    mods = _modulations(cc, mod_w, mod_b).reshape(DEPTH, V7X_SUBLANES, N_MOD, d)
    wg, wu, wd = (w.astype(BF16) for w in (ffn_w_gate, ffn_w_up, ffn_w_down))
    ng = norm_g.reshape(DEPTH, 3, 1, d)
    row = lambda v: v.reshape(v.shape[0], 1, v.shape[1])
    conv_vecs = tuple(row(v) for v in (conv_b_dw, conv_ln_g, conv_ln_b))

    for i in range(DEPTH):
        mixer, j = i % 2, i // 2
        last = i == DEPTH - 1
        for s in (0, 1):
            xs[s] = _ffn_half(xs[s], mods, ng, wg, wu, wd, i, s, 0, tm_ffn[s])
        if mixer == 0:
            for s in ((0,) if last else (0, 1)):
                u = _conv_glu(xs[s], mods, ng, conv_w_pw1, row(conv_b_pw1), i, s, j, tm[s])
                xs[s] = _conv_tail(u, xs[s], mods, conv_w_dw, *conv_vecs, conv_w_pw2, row(conv_b_pw2),
                                   i, s, j, tm[s])
        else:
            compact = _na_compact_bias(na_rpb[j], col_idx, col_valid)
            qkv = {s: _qkv_proj(xs[s], mods, ng, na_w_qkv, row(na_b_qkv), i, s, j, tm[s]) for s in (0, 1)}
            o = {0: _na_attention(*qkv[0], qkv[1][1], qkv[1][2], compact, rowmask)}
            if not last:
                o[1] = _ctx_attention(*qkv[1])
            for s in o:
                xs[s] = _out_proj(o[s], xs[s], mods, na_w_o, row(na_b_o), i, s, j, tm[s])
        for s in ((0,) if last else (0, 1)):
            xs[s] = _ffn_half(xs[s], mods, ng, wg, wu, wd, i, s, 1, tm_ffn[s],
                              final_g=final_g if (last and s == 0) else None)
    return xs[0].reshape(b, t, d)
```

```python
import functools

import numpy as np
import jax
import jax.numpy as jnp
from jax import lax
from jax.experimental import pallas as pl
from jax.experimental.pallas import tpu as pltpu

D_MODEL = 1024
DEPTH = 4
GRID_W = 64
N_HEADS = 16
HEAD_DIM = D_MODEL // N_HEADS
D_FF = 2816
CONV_WIDTH = 31
WIN_R = 8
WIN_C = 16
N_MOD = 9
RMS_EPS = 1e-6
LN_EPS = 1e-5
MASK_VALUE = -1e30

F32 = jnp.float32
BF16 = jnp.bfloat16

V7X_LANES = 128
V7X_SUBLANES = 8
V7X_MXU = 256
FF_CHUNK = V7X_MXU
FFN_PREP_PIECES = 8
CONV_HALO = 16
CONV_ROWS = 128
NA_Q_ROWS = 8
NA_STEP_BLOCKS = 2
NA_HALO_ROWS = WIN_R // 2
NA_KEY_ROWS = NA_Q_ROWS + 2 * NA_HALO_ROWS
NA_SPAN = 2 * WIN_C
NA_GROUP = V7X_MXU // HEAD_DIM
NA_NQ = NA_Q_ROWS * WIN_C
NA_NK = NA_KEY_ROWS * NA_SPAN
VMEM_LIMIT = 56 * 1024 * 1024


def _sel_spec(prefix, tail):
    prefix, tail = tuple(prefix), tuple(tail)
    return pl.BlockSpec((None,) * len(prefix) + tail, lambda *_: prefix + (0,) * len(tail),
                        pipeline_mode=pl.Buffered(1))


def _tok_spec(tm):
    return pl.BlockSpec((tm, D_MODEL), lambda i: (i, 0))


def _params(n_grid_dims):
    return pltpu.CompilerParams(
        dimension_semantics=("arbitrary",) * n_grid_dims, vmem_limit_bytes=VMEM_LIMIT)


def _dot(a, b):
    return jnp.dot(a, b, preferred_element_type=F32)


def _dot_nt(a, b):
    return lax.dot_general(a, b, (((1,), (1,)), ((), ())), preferred_element_type=F32)


def _rmsnorm(x, g):
    return x * lax.rsqrt(jnp.mean(x * x, axis=-1, keepdims=True) + RMS_EPS) * g


def _rms_mod(x, g, shift, scale):
    return _rmsnorm(x, g) * (1.0 + scale) + shift


def _cast_once(w_ref, wb_ref):
    @pl.when(pl.program_id(0) == 0)
    def _():
        wb_ref[...] = w_ref[...].astype(BF16)


def _next_tok_spec(tm, n_blocks):
    return pl.BlockSpec((tm, D_MODEL), lambda i: (jnp.minimum(i + 1, n_blocks - 1), 0))


def _prime_norm(x_ref, h_ref, norm):
    i = pl.program_id(0)

    @pl.when(i == 0)
    def _():
        h_ref[0] = norm(x_ref[...])

    return i % 2


def _prep_next(xn_ref, h_ref, slot, k, n_pieces, norm):
    piece = xn_ref.shape[0] // n_pieces
    rows = slice(k * piece, (k + 1) * piece)
    h_ref[1 - slot, rows, :] = norm(xn_ref[rows, :])


def _mod_kernel(cc_ref, w_ref, b_ref, o_ref):
    s = jax.nn.silu(cc_ref[...]).astype(BF16)
    o_ref[...] = _dot(s, w_ref[...].astype(BF16)) + b_ref[...]


def _modulations(cc, mod_w, mod_b):
    rows = cc.shape[0]
    return pl.pallas_call(
        _mod_kernel,
        grid=(DEPTH, N_MOD),
        in_specs=[
            pl.BlockSpec((rows, D_MODEL), lambda l, j: (0, 0)),
            pl.BlockSpec((None, D_MODEL, D_MODEL), lambda l, j: (l, 0, j)),
            pl.BlockSpec((None, 1, D_MODEL), lambda l, j: (l, 0, j)),
        ],
        out_specs=pl.BlockSpec((None, rows, D_MODEL), lambda l, j: (l, 0, j)),
        out_shape=jax.ShapeDtypeStruct((DEPTH, rows, N_MOD * D_MODEL), F32),
        compiler_params=_params(2),
        name="modulations",
    )(cc, mod_w, mod_b.reshape(DEPTH, 1, N_MOD * D_MODEL))


def _ffn_kernel(x_ref, xn_ref, mod_ref, g_ref, wg_ref, wu_ref, wd_ref, *rest, mod0, final, n_cast):
    cast_in, rest = rest[:n_cast], rest[n_cast:]
    final_ref = rest[0] if final else None
    o_ref = rest[1 if final else 0]
    cast_out = rest[(2 if final else 1):-1]
    h_ref = rest[-1]
    shift, scale, gate = (mod_ref[mod0 + k:mod0 + k + 1, :] for k in range(3))
    norm = lambda v: _rms_mod(v, g_ref[...], shift, scale).astype(BF16)
    slot = _prime_norm(x_ref, h_ref, norm)
    x = x_ref[...]
    acc = jnp.zeros(x.shape, F32)
    for f in range(D_FF // FF_CHUNK):
        sl = slice(f * FF_CHUNK, (f + 1) * FF_CHUNK)
        h = h_ref[slot]
        a = jax.nn.silu(_dot(h, wg_ref[:, sl])) * _dot(h, wu_ref[:, sl])
        acc = acc + _dot(a.astype(BF16), wd_ref[sl, :])
        if f < FFN_PREP_PIECES:
            _prep_next(xn_ref, h_ref, slot, f, FFN_PREP_PIECES, norm)
    out = x + 0.5 * gate * acc
    if final:
        out = _rmsnorm(out, final_ref[...])
    o_ref[...] = out
    for src, dst in zip(cast_in, cast_out):
        dst[...] = src[...].astype(BF16)


def _ffn_half(x, mods, norm_g, w_bf16, layer, stream, half, tm, final_g=None, next_w=None):
    t = x.shape[0]
    final = final_g is not None
    n_blocks = t // tm
    wg, wu, wd = w_bf16
    in_specs = [
        _tok_spec(tm),
        _next_tok_spec(tm, n_blocks),
        _sel_spec((layer, stream), (N_MOD, D_MODEL)),
        _sel_spec((layer, 2 * half), (1, D_MODEL)),
        _sel_spec((), (D_MODEL, D_FF)),
        _sel_spec((), (D_MODEL, D_FF)),
        _sel_spec((), (D_FF, D_MODEL)),
    ]
    args = [x, x, mods, norm_g, wg, wu, wd]
    out_specs = [_tok_spec(tm)]
    out_shape = [jax.ShapeDtypeStruct((t, D_MODEL), F32)]
    n_cast = 0
    if next_w is not None:
        stacked, nl, nh = next_w
        n_cast = len(stacked)
        slab = D_MODEL // n_blocks
        for w in stacked:
            in_specs.append(pl.BlockSpec((None, None, slab, D_FF), lambda i: (nl, nh, i, 0)))
            args.append(w)
            out_specs.append(pl.BlockSpec((slab, D_FF), lambda i: (i, 0)))
            out_shape.append(jax.ShapeDtypeStruct((D_MODEL, D_FF), BF16))
    if final:
        in_specs.append(_sel_spec((), (1, D_MODEL)))
        args.append(final_g.reshape(1, D_MODEL))
    outs = pl.pallas_call(
        functools.partial(_ffn_kernel, mod0=6 * half, final=final, n_cast=n_cast),
        grid=(n_blocks,),
        in_specs=in_specs,
        out_specs=out_specs,
        out_shape=out_shape,
        scratch_shapes=[pltpu.VMEM((2, tm, D_MODEL), BF16)],
        compiler_params=_params(1),
        name="ffn_half",
    )(*args)
    return outs[0], tuple(outs[1:])


def _conv_glu_kernel(x_ref, mod_ref, g_ref, w1_ref, b1_ref, u_ref, w1b_ref):
    _cast_once(w1_ref, w1b_ref)
    h = _rms_mod(x_ref[...], g_ref[...], mod_ref[3:4, :], mod_ref[4:5, :]).astype(BF16)
    a = _dot(h, w1b_ref[:, :D_MODEL]) + b1_ref[:, :D_MODEL]
    gt = _dot(h, w1b_ref[:, D_MODEL:]) + b1_ref[:, D_MODEL:]
    u_ref[...] = a * jax.nn.sigmoid(gt)


def _conv_glu(x, mods, norm_g, w1, b1, layer, stream, j, tm):
    t = x.shape[0]
    return pl.pallas_call(
        _conv_glu_kernel,
        grid=(t // tm,),
        in_specs=[
            _tok_spec(tm),
            _sel_spec((layer, stream), (N_MOD, D_MODEL)),
            _sel_spec((layer, 1), (1, D_MODEL)),
            _sel_spec((j,), (D_MODEL, 2 * D_MODEL)),
            _sel_spec((j,), (1, 2 * D_MODEL)),
        ],
        out_specs=_tok_spec(tm),
        out_shape=jax.ShapeDtypeStruct((t, D_MODEL), F32),
        scratch_shapes=[pltpu.VMEM((D_MODEL, 2 * D_MODEL), BF16)],
        compiler_params=_params(1),
        name="conv_glu",
    )(x, mods, norm_g, w1, b1)


def _conv_tail_kernel(ul_ref, um_ref, ur_ref, x_ref, mod_ref, wdw_ref, bdw_ref, lng_ref, lnb_ref,
                      w2_ref, b2_ref, o_ref, win_ref, cv_ref, w2b_ref, *, tm):
    _cast_once(w2_ref, w2b_ref)
    i = pl.program_id(0)
    last = pl.num_programs(0) - 1
    win_ref[0:CONV_HALO, :] = jnp.where(i > 0, ul_ref[...], 0.0)
    win_ref[CONV_HALO:CONV_HALO + tm, :] = um_ref[...]
    win_ref[CONV_HALO + tm:, :] = jnp.where(i < last, ur_ref[...], 0.0)
    off0 = CONV_HALO - CONV_WIDTH // 2
    rows = min(CONV_ROWS, tm)
    ext = rows + V7X_SUBLANES
    for r in range(tm // rows):
        for c in range(D_MODEL // V7X_LANES):
            cs = slice(c * V7X_LANES, (c + 1) * V7X_LANES)
            acc = None
            for s in range(V7X_SUBLANES):
                p = None
                for k in range(CONV_WIDTH):
                    a, res = divmod(off0 + k, V7X_SUBLANES)
                    if res != s:
                        continue
                    start = r * rows + V7X_SUBLANES * a
                    term = win_ref[start:start + ext, cs] * wdw_ref[k:k + 1, cs]
                    p = term if p is None else p + term
                p = p[s:s + rows]
                acc = p if acc is None else acc + p
            cv_ref[r * rows:(r + 1) * rows, cs] = acc
    cv = cv_ref[...] + bdw_ref[...]
    mu = jnp.mean(cv, axis=-1, keepdims=True)
    cen = cv - mu
    var = jnp.mean(cen * cen, axis=-1, keepdims=True)
    y = cen * lax.rsqrt(var + LN_EPS) * lng_ref[...] + lnb_ref[...]
    out = _dot(jax.nn.silu(y).astype(BF16), w2b_ref[...]) + b2_ref[...]
    o_ref[...] = x_ref[...] + mod_ref[5:6, :] * out


def _conv_tail(u, x, mods, wdw, bdw, lng, lnb, w2, b2, layer, stream, j, tm):
    t = x.shape[0]
    hb = tm // CONV_HALO
    n_halo_blocks = t // CONV_HALO
    vec = _sel_spec((j,), (1, D_MODEL))
    return pl.pallas_call(
        functools.partial(_conv_tail_kernel, tm=tm),
        grid=(t // tm,),
        in_specs=[
            pl.BlockSpec((CONV_HALO, D_MODEL), lambda i: (jnp.maximum(i * hb - 1, 0), 0)),
            _tok_spec(tm),
            pl.BlockSpec((CONV_HALO, D_MODEL),
                         lambda i: (jnp.minimum((i + 1) * hb, n_halo_blocks - 1), 0)),
            _tok_spec(tm),
            _sel_spec((layer, stream), (N_MOD, D_MODEL)),
            _sel_spec((j,), (CONV_WIDTH, D_MODEL)),
            vec, vec, vec,
            _sel_spec((j,), (D_MODEL, D_MODEL)),
            vec,
        ],
        out_specs=_tok_spec(tm),
        out_shape=jax.ShapeDtypeStruct((t, D_MODEL), F32),
        scratch_shapes=[
            pltpu.VMEM((tm + 2 * CONV_HALO, D_MODEL), F32),
            pltpu.VMEM((tm, D_MODEL), F32),
            pltpu.VMEM((D_MODEL, D_MODEL), BF16),
        ],
        compiler_params=_params(1),
        name="conv_tail",
    )(u, u, u, x, mods, wdw, bdw, lng, lnb, w2, b2)


def _qkv_kernel(x_ref, mod_ref, g_ref, w_ref, b_ref, q_ref, k_ref, v_ref, wb_ref):
    _cast_once(w_ref, wb_ref)
    h = _rms_mod(x_ref[...], g_ref[...], mod_ref[3:4, :], mod_ref[4:5, :]).astype(BF16)
    d = D_MODEL
    q = _dot(h, wb_ref[:, 0:d]) + b_ref[:, 0:d]
    q_ref[...] = (q * (HEAD_DIM ** -0.5)).astype(BF16)
    k_ref[...] = _dot(h, wb_ref[:, d:2 * d]) + b_ref[:, d:2 * d]
    v_ref[...] = _dot(h, wb_ref[:, 2 * d:3 * d]) + b_ref[:, 2 * d:3 * d]


def _qkv_proj(x, mods, norm_g, w, b, layer, stream, j, tm):
    t = x.shape[0]
    return pl.pallas_call(
        _qkv_kernel,
        grid=(t // tm,),
        in_specs=[
            _tok_spec(tm),
            _sel_spec((layer, stream), (N_MOD, D_MODEL)),
            _sel_spec((layer, 1), (1, D_MODEL)),
            _sel_spec((j,), (D_MODEL, 3 * D_MODEL)),
            _sel_spec((j,), (1, 3 * D_MODEL)),
        ],
        out_specs=[_tok_spec(tm)] * 3,
        out_shape=[
            jax.ShapeDtypeStruct((t, D_MODEL), BF16),
            jax.ShapeDtypeStruct((t, D_MODEL), F32),
            jax.ShapeDtypeStruct((t, D_MODEL), F32),
        ],
        scratch_shapes=[pltpu.VMEM((D_MODEL, 3 * D_MODEL), BF16)],
        compiler_params=_params(1),
        name="qkv_proj",
    )(x, mods, norm_g, w, b)


def _head_lane_masks():
    lane = lax.broadcasted_iota(jnp.int32, (1, V7X_MXU), 1)
    return [(lane >= h * HEAD_DIM) & (lane < (h + 1) * HEAD_DIM) for h in range(NA_GROUP)]


def _stack_heads(q, masks):
    return jnp.concatenate([q * m.astype(q.dtype) for m in masks], axis=0)


def _softmax_pv(scores, values):
    mx = functools.reduce(jnp.maximum, [jnp.max(s, axis=-1, keepdims=True) for s in scores])
    ps = [jnp.exp(s - mx) for s in scores]
    denom = functools.reduce(jnp.add, [jnp.sum(p, axis=-1, keepdims=True) for p in ps])
    o = functools.reduce(jnp.add, [_dot(p.astype(BF16), v) for p, v in zip(ps, values)])
    return o / denom


def _unstack_heads(o, masks, nq):
    out = o[(NA_GROUP - 1) * nq:]
    for h in range(NA_GROUP - 2, -1, -1):
        out = jnp.where(masks[h], o[h * nq:(h + 1) * nq], out)
    return out


def _build_bias_tables(x_ref, rowmask_ref, tab_ref):
    top = (NA_Q_ROWS - 1) * NA_SPAN
    for n in range(GRID_W // WIN_C):
        for hh in range(NA_GROUP):
            compact = x_ref[hh, n]
            for rho in range(NA_Q_ROWS):
                strip = compact[:, top - rho * NA_SPAN:top - rho * NA_SPAN + NA_NK]
                r0 = hh * NA_NQ + rho * WIN_C
                for cls in range(3):
                    tab_ref[cls, n, r0:r0 + WIN_C, :] = jnp.where(
                        rowmask_ref[cls, rho] > 0.5, strip, MASK_VALUE)


def _na_kernel(q_ref, kt_ref, km_ref, kb_ref, vt_ref, vm_ref, vb_ref, kc_ref, vc_ref, x_ref, rowmask_ref,
               o_ref, tab_ref):
    b = pl.program_id(1)
    last = pl.num_programs(1) - 1

    @pl.when(b == 0)
    def _():
        _build_bias_tables(x_ref, rowmask_ref, tab_ref)

    masks = _head_lane_masks()
    step_rows = NA_STEP_BLOCKS * NA_Q_ROWS
    q3 = q_ref[...].reshape(step_rows, GRID_W, V7X_MXU)
    kw = jnp.concatenate([kt_ref[...], km_ref[...], kb_ref[...]], axis=0)
    vw = jnp.concatenate([vt_ref[...], vm_ref[...], vb_ref[...]], axis=0)
    kw3 = kw.reshape(step_rows + 2 * NA_HALO_ROWS, GRID_W, V7X_MXU)
    vw3 = vw.reshape(step_rows + 2 * NA_HALO_ROWS, GRID_W, V7X_MXU)
    kc = kc_ref[...].astype(BF16)
    vc = vc_ref[...].astype(BF16)
    for sb in range(NA_STEP_BLOCKS):
        cls = 1
        if sb == 0:
            cls = jnp.where(b == 0, 0, cls)
        if sb == NA_STEP_BLOCKS - 1:
            cls = jnp.where(b == last, 2, cls)
        r_lo = sb * NA_Q_ROWS
        for n in range(GRID_W // WIN_C):
            c0 = min(max(n * WIN_C - WIN_C // 2, 0), GRID_W - NA_SPAN)
            qn = q3[r_lo:r_lo + NA_Q_ROWS, n * WIN_C:(n + 1) * WIN_C, :].reshape(NA_NQ, V7X_MXU)
            kn = kw3[r_lo:r_lo + NA_KEY_ROWS, c0:c0 + NA_SPAN, :].reshape(NA_NK, V7X_MXU).astype(BF16)
            vn = vw3[r_lo:r_lo + NA_KEY_ROWS, c0:c0 + NA_SPAN, :].reshape(NA_NK, V7X_MXU).astype(BF16)
            qs = _stack_heads(qn, masks)
            s_lat = _dot_nt(qs, kn) + tab_ref[cls, n]
            s_ctx = _dot_nt(qs, kc)
            o = _unstack_heads(_softmax_pv([s_lat, s_ctx], [vn, vc]), masks, NA_NQ)
            o3 = o.astype(BF16).reshape(NA_Q_ROWS, WIN_C, V7X_MXU)
            for r in range(NA_Q_ROWS):
                row0 = (r_lo + r) * GRID_W + n * WIN_C
                o_ref[row0:row0 + WIN_C, :] = o3[r]


def _na_attention(q, k, v, kc, vc, compact, rowmask):
    t = q.shape[0]
    tq = NA_STEP_BLOCKS * NA_Q_ROWS * GRID_W
    th = NA_HALO_ROWS * GRID_W
    nb = t // tq
    n_halo_blocks = t // th
    hpb = tq // th
    ncb = GRID_W // WIN_C
    main = lambda: pl.BlockSpec((tq, V7X_MXU), lambda g, b: (b, g))
    top = lambda: pl.BlockSpec((th, V7X_MXU), lambda g, b: (jnp.maximum(b * hpb - 1, 0), g))
    bot = lambda: pl.BlockSpec(
        (th, V7X_MXU), lambda g, b: (jnp.minimum((b + 1) * hpb, n_halo_blocks - 1), g))
    ctx = lambda: pl.BlockSpec((kc.shape[0], V7X_MXU), lambda g, b: (0, g))
    return pl.pallas_call(
        _na_kernel,
        grid=(N_HEADS // NA_GROUP, nb),
        in_specs=[main(), top(), main(), bot(), top(), main(), bot(), ctx(), ctx(),
                  pl.BlockSpec((NA_GROUP,) + compact.shape[1:], lambda g, b: (g, 0, 0, 0)),
                  pl.BlockSpec(rowmask.shape, lambda g, b: (0, 0, 0, 0))],
        out_specs=main(),
        out_shape=jax.ShapeDtypeStruct((t, D_MODEL), BF16),
        scratch_shapes=[pltpu.VMEM((3, ncb, NA_GROUP * NA_NQ, NA_NK), F32)],
        compiler_params=_params(2),
        name="neighbourhood_attention",
    )(q, k, k, k, v, v, v, kc, vc, compact, rowmask)


def _ctx_attn_kernel(q_ref, k_ref, v_ref, o_ref):
    masks = _head_lane_masks()
    nq = q_ref.shape[0]
    qs = _stack_heads(q_ref[...], masks)
    k = k_ref[...].astype(BF16)
    v = v_ref[...].astype(BF16)
    o_ref[...] = _unstack_heads(_softmax_pv([_dot_nt(qs, k)], [v]), masks, nq).astype(BF16)


def _ctx_attention(q, k, v):
    t = q.shape[0]
    blk = pl.BlockSpec((t, V7X_MXU), lambda g: (0, g))
    return pl.pallas_call(
        _ctx_attn_kernel,
        grid=(N_HEADS // NA_GROUP,),
        in_specs=[blk, blk, blk],
        out_specs=blk,
        out_shape=jax.ShapeDtypeStruct((t, D_MODEL), BF16),
        compiler_params=_params(1),
        name="context_attention",
    )(q, k, v)


def _out_proj_kernel(o_ref, x_ref, mod_ref, w_ref, b_ref, y_ref, wb_ref):
    _cast_once(w_ref, wb_ref)
    y = _dot(o_ref[...], wb_ref[...]) + b_ref[...]
    y_ref[...] = x_ref[...] + mod_ref[5:6, :] * y


def _out_proj(o, x, mods, w, b, layer, stream, j, tm):
    t = x.shape[0]
    return pl.pallas_call(
        _out_proj_kernel,
        grid=(t // tm,),
        in_specs=[_tok_spec(tm), _tok_spec(tm),
                  _sel_spec((layer, stream), (N_MOD, D_MODEL)),
                  _sel_spec((j,), (D_MODEL, D_MODEL)),
                  _sel_spec((j,), (1, D_MODEL))],
        out_specs=_tok_spec(tm),
        out_shape=jax.ShapeDtypeStruct((t, D_MODEL), F32),
        scratch_shapes=[pltpu.VMEM((D_MODEL, D_MODEL), BF16)],
        compiler_params=_params(1),
        name="attn_out_proj",
    )(o, x, mods, w, b)


def _na_index_tables(rows):
    ncb = GRID_W // WIN_C
    n = np.arange(ncb)
    c0 = np.clip(n * WIN_C - WIN_C // 2, 0, GRID_W - NA_SPAN)
    q_col = n[:, None] * WIN_C + np.arange(WIN_C)[None, :]
    k_col = c0[:, None] + np.arange(NA_SPAN)[None, :]
    q_start = np.clip(q_col - WIN_C // 2, 0, GRID_W - WIN_C)
    col_valid = ((k_col[:, None, :] >= q_start[:, :, None])
                 & (k_col[:, None, :] < q_start[:, :, None] + WIN_C))
    col_idx = np.clip(k_col[:, None, :] - q_col[:, :, None], -(WIN_C - 1), WIN_C - 1) + WIN_C - 1
    wr = min(WIN_R, rows)
    rho = np.arange(NA_Q_ROWS)
    kappa = np.arange(NA_KEY_ROWS)
    row_valid = []
    for r_start in (0, NA_Q_ROWS, rows - NA_Q_ROWS):
        r = r_start + rho
        j = r_start - NA_HALO_ROWS + kappa
        r0 = np.clip(r - wr // 2, 0, rows - wr)
        row_valid.append((j[None, :] >= r0[:, None]) & (j[None, :] < r0[:, None] + wr))
    row_valid = np.repeat(np.stack(row_valid), NA_SPAN, axis=-1)
    rowmask = row_valid.astype(np.float32).reshape(3, NA_Q_ROWS, 1, NA_NK)
    return col_idx.astype(np.int32), col_valid, rowmask


def _na_compact_bias(rpb, col_idx, col_valid):
    ncb = GRID_W // WIN_C
    by_col = jnp.take(rpb, jnp.asarray(col_idx.reshape(-1)), axis=2)
    by_col = by_col.reshape(N_HEADS, 2 * WIN_R - 1, ncb, WIN_C, NA_SPAN).transpose(0, 2, 3, 1, 4)
    by_col = jnp.where(jnp.asarray(col_valid)[None, :, :, None, :], by_col, MASK_VALUE)
    shift = WIN_R - 1 - NA_HALO_ROWS
    front = NA_Q_ROWS - 1 - shift
    total = NA_Q_ROWS + NA_KEY_ROWS
    back = total - front - (2 * WIN_R - 1)
    padded = jnp.pad(by_col, ((0, 0), (0, 0), (0, 0), (front, back), (0, 0)))
    return padded.reshape(N_HEADS, ncb, WIN_C, total * NA_SPAN)


def kernel(x, c, ctx, c_ctx, mod_w, mod_b, norm_g, ffn_w_gate, ffn_w_up, ffn_w_down, conv_w_pw1, conv_b_pw1, conv_w_dw, conv_b_dw, conv_ln_g, conv_ln_b, conv_w_pw2, conv_b_pw2, na_w_qkv, na_b_qkv, na_rpb, na_w_o, na_b_o, final_g):
    b, t, d = x.shape
    assert b == 1 and d == D_MODEL and t % (NA_STEP_BLOCKS * NA_Q_ROWS * GRID_W) == 0
    n_ctx = ctx.shape[1]
    tm = {0: 512, 1: n_ctx}
    tm_mm = {0: 1024, 1: n_ctx}
    col_idx, col_valid, rowmask = _na_index_tables(t // GRID_W)
    rowmask = jnp.asarray(rowmask)
    xs = {0: x.reshape(t, d), 1: ctx.reshape(n_ctx, d)}

    cc = jnp.zeros((V7X_SUBLANES, d), F32).at[0].set(c[0]).at[1].set(c_ctx)
    mods = _modulations(cc, mod_w, mod_b).reshape(DEPTH, V7X_SUBLANES, N_MOD, d)
    ng = norm_g.reshape(DEPTH, 3, 1, d)
    row = lambda v: v.reshape(v.shape[0], 1, v.shape[1])
    conv_vecs = tuple(row(v) for v in (conv_b_dw, conv_ln_g, conv_ln_b))
    ffn_w32 = tuple(w.reshape(DEPTH, 2, d, D_FF) for w in (ffn_w_gate, ffn_w_up, ffn_w_down))
    as_ffn = lambda w3: (w3[0], w3[1], w3[2].reshape(D_FF, d))
    w_cur = as_ffn(tuple(w[0, 0].astype(BF16) for w in ffn_w32))

    for i in range(DEPTH):
        mixer, j = i % 2, i // 2
        last = i == DEPTH - 1
        for half in (0, 1):
            if half == 1:
                if mixer == 0:
                    for s in ((0,) if last else (0, 1)):
                        u = _conv_glu(xs[s], mods, ng, conv_w_pw1, row(conv_b_pw1), i, s, j, tm_mm[s])
                        xs[s] = _conv_tail(u, xs[s], mods, conv_w_dw, *conv_vecs, conv_w_pw2,
                                           row(conv_b_pw2), i, s, j, tm[s])
                else:
                    compact = _na_compact_bias(na_rpb[j], col_idx, col_valid)
                    qkv = {s: _qkv_proj(xs[s], mods, ng, na_w_qkv, row(na_b_qkv), i, s, j, tm_mm[s])
                           for s in (0, 1)}
                    o = {0: _na_attention(*qkv[0], qkv[1][1], qkv[1][2], compact, rowmask)}
                    if not last:
                        o[1] = _ctx_attention(*qkv[1])
                    for s in o:
                        xs[s] = _out_proj(o[s], xs[s], mods, na_w_o, row(na_b_o), i, s, j, tm_mm[s])
            final = last and half == 1
            nxt = None if final else (ffn_w32, i + half, 1 - half)
            xs[0], w_next = _ffn_half(xs[0], mods, ng, w_cur, i, 0, half, tm[0],
                                      final_g=final_g if final else None, next_w=nxt)
            if not final:
                xs[1], _ = _ffn_half(xs[1], mods, ng, w_cur, i, 1, half, tm[1])
                w_cur = as_ffn(w_next)
    return xs[0].reshape(b, t, d)
```

```python
import functools

import numpy as np
import jax
import jax.numpy as jnp
from jax import lax
from jax.experimental import pallas as pl
from jax.experimental.pallas import tpu as pltpu

D_MODEL = 1024
DEPTH = 4
GRID_W = 64
N_HEADS = 16
HEAD_DIM = D_MODEL // N_HEADS
D_FF = 2816
CONV_WIDTH = 31
WIN_R = 8
WIN_C = 16
N_MOD = 9
RMS_EPS = 1e-6
LN_EPS = 1e-5
MASK_VALUE = -1e30

F32 = jnp.float32
BF16 = jnp.bfloat16

V7X_LANES = 128
V7X_SUBLANES = 8
V7X_MXU = 256
FF_CHUNK = V7X_MXU
FFN_PREP_PIECES = 8
CONV_HALO = 16
CONV_ROWS = 128
NA_Q_ROWS = 8
NA_STEP_BLOCKS = 2
NA_HALO_ROWS = WIN_R // 2
NA_KEY_ROWS = NA_Q_ROWS + 2 * NA_HALO_ROWS
NA_SPAN = 2 * WIN_C
NA_GROUP = V7X_MXU // HEAD_DIM
NA_NQ = NA_Q_ROWS * WIN_C
NA_NK = NA_KEY_ROWS * NA_SPAN
VMEM_LIMIT = 56 * 1024 * 1024


def _sel_spec(prefix, tail):
    prefix, tail = tuple(prefix), tuple(tail)
    return pl.BlockSpec((None,) * len(prefix) + tail, lambda *_: prefix + (0,) * len(tail),
                        pipeline_mode=pl.Buffered(1))


def _tok_spec(tm):
    return pl.BlockSpec((tm, D_MODEL), lambda i: (i, 0))


def _params(n_grid_dims):
    return pltpu.CompilerParams(
        dimension_semantics=("arbitrary",) * n_grid_dims, vmem_limit_bytes=VMEM_LIMIT)


def _dot(a, b):
    return jnp.dot(a, b, preferred_element_type=F32)


def _dot_nt(a, b):
    return lax.dot_general(a, b, (((1,), (1,)), ((), ())), preferred_element_type=F32)


def _rmsnorm(x, g):
    return x * lax.rsqrt(jnp.mean(x * x, axis=-1, keepdims=True) + RMS_EPS) * g


def _rms_mod(x, g, shift, scale):
    return _rmsnorm(x, g) * (1.0 + scale) + shift


def _cast_once(w_ref, wb_ref):
    @pl.when(pl.program_id(0) == 0)
    def _():
        wb_ref[...] = w_ref[...].astype(BF16)


def _next_tok_spec(tm, n_blocks):
    return pl.BlockSpec((tm, D_MODEL), lambda i: (jnp.minimum(i + 1, n_blocks - 1), 0))


def _prime_norm(x_ref, h_ref, norm):
    i = pl.program_id(0)

    @pl.when(i == 0)
    def _():
        h_ref[0] = norm(x_ref[...])

    return i % 2


def _prep_next(xn_ref, h_ref, slot, k, n_pieces, norm):
    piece = xn_ref.shape[0] // n_pieces
    rows = slice(k * piece, (k + 1) * piece)
    h_ref[1 - slot, rows, :] = norm(xn_ref[rows, :])


def _mod_kernel(cc_ref, w_ref, b_ref, o_ref):
    s = jax.nn.silu(cc_ref[...]).astype(BF16)
    o_ref[...] = _dot(s, w_ref[...].astype(BF16)) + b_ref[...]


def _modulations(cc, mod_w, mod_b):
    rows = cc.shape[0]
    return pl.pallas_call(
        _mod_kernel,
        grid=(DEPTH, N_MOD),
        in_specs=[
            pl.BlockSpec((rows, D_MODEL), lambda l, j: (0, 0)),
            pl.BlockSpec((None, D_MODEL, D_MODEL), lambda l, j: (l, 0, j)),
            pl.BlockSpec((None, 1, D_MODEL), lambda l, j: (l, 0, j)),
        ],
        out_specs=pl.BlockSpec((None, rows, D_MODEL), lambda l, j: (l, 0, j)),
        out_shape=jax.ShapeDtypeStruct((DEPTH, rows, N_MOD * D_MODEL), F32),
        compiler_params=_params(2),
        name="modulations",
    )(cc, mod_w, mod_b.reshape(DEPTH, 1, N_MOD * D_MODEL))


def _ffn_kernel(x_ref, xn_ref, mod_ref, g_ref, wg_ref, wu_ref, wd_ref, *rest, mod0, final, n_cast):
    cast_in, rest = rest[:n_cast], rest[n_cast:]
    final_ref = rest[0] if final else None
    o_ref = rest[1 if final else 0]
    cast_out = rest[(2 if final else 1):-1]
    h_ref = rest[-1]
    shift, scale, gate = (mod_ref[mod0 + k:mod0 + k + 1, :] for k in range(3))
    norm = lambda v: _rms_mod(v, g_ref[...], shift, scale).astype(BF16)
    slot = _prime_norm(x_ref, h_ref, norm)
    x = x_ref[...]
    acc = jnp.zeros(x.shape, F32)
    for f in range(D_FF // FF_CHUNK):
        sl = slice(f * FF_CHUNK, (f + 1) * FF_CHUNK)
        h = h_ref[slot]
        a = jax.nn.silu(_dot(h, wg_ref[:, sl])) * _dot(h, wu_ref[:, sl])
        acc = acc + _dot(a.astype(BF16), wd_ref[sl, :])
        if f < FFN_PREP_PIECES:
            _prep_next(xn_ref, h_ref, slot, f, FFN_PREP_PIECES, norm)
    out = x + 0.5 * gate * acc
    if final:
        out = _rmsnorm(out, final_ref[...])
    o_ref[...] = out
    for src, dst in zip(cast_in, cast_out):
        dst[...] = src[...].astype(BF16)


def _ffn_half(x, mods, norm_g, w_bf16, layer, stream, half, tm, final_g=None, next_w=None):
    t = x.shape[0]
    final = final_g is not None
    n_blocks = t // tm
    wg, wu, wd = w_bf16
    in_specs = [
        _tok_spec(tm),
        _next_tok_spec(tm, n_blocks),
        _sel_spec((layer, stream), (N_MOD, D_MODEL)),
        _sel_spec((layer, 2 * half), (1, D_MODEL)),
        _sel_spec((), (D_MODEL, D_FF)),
        _sel_spec((), (D_MODEL, D_FF)),
        _sel_spec((), (D_FF, D_MODEL)),
    ]
    args = [x, x, mods, norm_g, wg, wu, wd]
    out_specs = [_tok_spec(tm)]
    out_shape = [jax.ShapeDtypeStruct((t, D_MODEL), F32)]
    n_cast = 0
    if next_w is not None:
        stacked, nl, nh = next_w
        n_cast = len(stacked)
        for w in stacked:
            rows, cols = w.shape[2:]
            slab = next(s for s in range(16, rows + 1, 16) if rows % s == 0 and rows // s <= n_blocks)
            slab_of = lambda i, n_slabs=rows // slab: jnp.minimum(i, n_slabs - 1)
            in_specs.append(pl.BlockSpec((None, None, slab, cols),
                                         lambda i, slab_of=slab_of: (nl, nh, slab_of(i), 0)))
            args.append(w)
            out_specs.append(pl.BlockSpec((slab, cols), lambda i, slab_of=slab_of: (slab_of(i), 0)))
            out_shape.append(jax.ShapeDtypeStruct((rows, cols), BF16))
    if final:
        in_specs.append(_sel_spec((), (1, D_MODEL)))
        args.append(final_g.reshape(1, D_MODEL))
    outs = pl.pallas_call(
        functools.partial(_ffn_kernel, mod0=6 * half, final=final, n_cast=n_cast),
        grid=(n_blocks,),
        in_specs=in_specs,
        out_specs=out_specs,
        out_shape=out_shape,
        scratch_shapes=[pltpu.VMEM((2, tm, D_MODEL), BF16)],
        compiler_params=_params(1),
        name="ffn_half",
    )(*args)
    return outs[0], tuple(outs[1:])


def _conv_glu_kernel(x_ref, mod_ref, g_ref, w1_ref, b1_ref, u_ref, w1b_ref):
    _cast_once(w1_ref, w1b_ref)
    h = _rms_mod(x_ref[...], g_ref[...], mod_ref[3:4, :], mod_ref[4:5, :]).astype(BF16)
    a = _dot(h, w1b_ref[:, :D_MODEL]) + b1_ref[:, :D_MODEL]
    gt = _dot(h, w1b_ref[:, D_MODEL:]) + b1_ref[:, D_MODEL:]
    u_ref[...] = a * jax.nn.sigmoid(gt)


def _conv_glu(x, mods, norm_g, w1, b1, layer, stream, j, tm):
    t = x.shape[0]
    return pl.pallas_call(
        _conv_glu_kernel,
        grid=(t // tm,),
        in_specs=[
            _tok_spec(tm),
            _sel_spec((layer, stream), (N_MOD, D_MODEL)),
            _sel_spec((layer, 1), (1, D_MODEL)),
            _sel_spec((j,), (D_MODEL, 2 * D_MODEL)),
            _sel_spec((j,), (1, 2 * D_MODEL)),
        ],
        out_specs=_tok_spec(tm),
        out_shape=jax.ShapeDtypeStruct((t, D_MODEL), F32),
        scratch_shapes=[pltpu.VMEM((D_MODEL, 2 * D_MODEL), BF16)],
        compiler_params=_params(1),
        name="conv_glu",
    )(x, mods, norm_g, w1, b1)


def _conv_tail_kernel(ul_ref, um_ref, ur_ref, x_ref, mod_ref, wdw_ref, bdw_ref, lng_ref, lnb_ref,
                      w2_ref, b2_ref, o_ref, win_ref, cv_ref, w2b_ref, *, tm):
    _cast_once(w2_ref, w2b_ref)
    i = pl.program_id(0)
    last = pl.num_programs(0) - 1
    win_ref[0:CONV_HALO, :] = jnp.where(i > 0, ul_ref[...], 0.0)
    win_ref[CONV_HALO:CONV_HALO + tm, :] = um_ref[...]
    win_ref[CONV_HALO + tm:, :] = jnp.where(i < last, ur_ref[...], 0.0)
    off0 = CONV_HALO - CONV_WIDTH // 2
    rows = min(CONV_ROWS, tm)
    ext = rows + V7X_SUBLANES
    for r in range(tm // rows):
        for c in range(D_MODEL // V7X_LANES):
            cs = slice(c * V7X_LANES, (c + 1) * V7X_LANES)
            acc = None
            for s in range(V7X_SUBLANES):
                p = None
                for k in range(CONV_WIDTH):
                    a, res = divmod(off0 + k, V7X_SUBLANES)
                    if res != s:
                        continue
                    start = r * rows + V7X_SUBLANES * a
                    term = win_ref[start:start + ext, cs] * wdw_ref[k:k + 1, cs]
                    p = term if p is None else p + term
                p = p[s:s + rows]
                acc = p if acc is None else acc + p
            cv_ref[r * rows:(r + 1) * rows, cs] = acc
    cv = cv_ref[...] + bdw_ref[...]
    mu = jnp.mean(cv, axis=-1, keepdims=True)
    cen = cv - mu
    var = jnp.mean(cen * cen, axis=-1, keepdims=True)
    y = cen * lax.rsqrt(var + LN_EPS) * lng_ref[...] + lnb_ref[...]
    out = _dot(jax.nn.silu(y).astype(BF16), w2b_ref[...]) + b2_ref[...]
    o_ref[...] = x_ref[...] + mod_ref[5:6, :] * out


def _conv_tail(u, x, mods, wdw, bdw, lng, lnb, w2, b2, layer, stream, j, tm):
    t = x.shape[0]
    hb = tm // CONV_HALO
    n_halo_blocks = t // CONV_HALO
    vec = _sel_spec((j,), (1, D_MODEL))
    return pl.pallas_call(
        functools.partial(_conv_tail_kernel, tm=tm),
        grid=(t // tm,),
        in_specs=[
            pl.BlockSpec((CONV_HALO, D_MODEL), lambda i: (jnp.maximum(i * hb - 1, 0), 0)),
            _tok_spec(tm),
            pl.BlockSpec((CONV_HALO, D_MODEL),
                         lambda i: (jnp.minimum((i + 1) * hb, n_halo_blocks - 1), 0)),
            _tok_spec(tm),
            _sel_spec((layer, stream), (N_MOD, D_MODEL)),
            _sel_spec((j,), (CONV_WIDTH, D_MODEL)),
            vec, vec, vec,
            _sel_spec((j,), (D_MODEL, D_MODEL)),
            vec,
        ],
        out_specs=_tok_spec(tm),
        out_shape=jax.ShapeDtypeStruct((t, D_MODEL), F32),
        scratch_shapes=[
            pltpu.VMEM((tm + 2 * CONV_HALO, D_MODEL), F32),
            pltpu.VMEM((tm, D_MODEL), F32),
            pltpu.VMEM((D_MODEL, D_MODEL), BF16),
        ],
        compiler_params=_params(1),
        name="conv_tail",
    )(u, u, u, x, mods, wdw, bdw, lng, lnb, w2, b2)


def _qkv_kernel(x_ref, mod_ref, g_ref, w_ref, b_ref, q_ref, k_ref, v_ref, wb_ref):
    _cast_once(w_ref, wb_ref)
    h = _rms_mod(x_ref[...], g_ref[...], mod_ref[3:4, :], mod_ref[4:5, :]).astype(BF16)
    d = D_MODEL
    q = _dot(h, wb_ref[:, 0:d]) + b_ref[:, 0:d]
    q_ref[...] = (q * (HEAD_DIM ** -0.5)).astype(BF16)
    k_ref[...] = _dot(h, wb_ref[:, d:2 * d]) + b_ref[:, d:2 * d]
    v_ref[...] = _dot(h, wb_ref[:, 2 * d:3 * d]) + b_ref[:, 2 * d:3 * d]


def _qkv_proj(x, mods, norm_g, w, b, layer, stream, j, tm):
    t = x.shape[0]
    return pl.pallas_call(
        _qkv_kernel,
        grid=(t // tm,),
        in_specs=[
            _tok_spec(tm),
            _sel_spec((layer, stream), (N_MOD, D_MODEL)),
            _sel_spec((layer, 1), (1, D_MODEL)),
            _sel_spec((j,), (D_MODEL, 3 * D_MODEL)),
            _sel_spec((j,), (1, 3 * D_MODEL)),
        ],
        out_specs=[_tok_spec(tm)] * 3,
        out_shape=[
            jax.ShapeDtypeStruct((t, D_MODEL), BF16),
            jax.ShapeDtypeStruct((t, D_MODEL), F32),
            jax.ShapeDtypeStruct((t, D_MODEL), F32),
        ],
        scratch_shapes=[pltpu.VMEM((D_MODEL, 3 * D_MODEL), BF16)],
        compiler_params=_params(1),
        name="qkv_proj",
    )(x, mods, norm_g, w, b)


def _head_lane_masks():
    lane = lax.broadcasted_iota(jnp.int32, (1, V7X_MXU), 1)
    return [(lane >= h * HEAD_DIM) & (lane < (h + 1) * HEAD_DIM) for h in range(NA_GROUP)]


def _stack_heads(q, masks):
    return jnp.concatenate([q * m.astype(q.dtype) for m in masks], axis=0)


def _softmax_pv(scores, values):
    mx = functools.reduce(jnp.maximum, [jnp.max(s, axis=-1, keepdims=True) for s in scores])
    ps = [jnp.exp(s - mx) for s in scores]
    denom = functools.reduce(jnp.add, [jnp.sum(p, axis=-1, keepdims=True) for p in ps])
    o = functools.reduce(jnp.add, [_dot(p.astype(BF16), v) for p, v in zip(ps, values)])
    return o / denom


def _unstack_heads(o, masks, nq):
    out = o[(NA_GROUP - 1) * nq:]
    for h in range(NA_GROUP - 2, -1, -1):
        out = jnp.where(masks[h], o[h * nq:(h + 1) * nq], out)
    return out


def _build_bias_tables(x_ref, rowmask_ref, tab_ref):
    top = (NA_Q_ROWS - 1) * NA_SPAN
    for n in range(GRID_W // WIN_C):
        for hh in range(NA_GROUP):
            compact = x_ref[hh, n]
            for rho in range(NA_Q_ROWS):
                strip = compact[:, top - rho * NA_SPAN:top - rho * NA_SPAN + NA_NK]
                r0 = hh * NA_NQ + rho * WIN_C
                for cls in range(3):
                    tab_ref[cls, n, r0:r0 + WIN_C, :] = jnp.where(
                        rowmask_ref[cls, rho] > 0.5, strip, MASK_VALUE)


def _na_kernel(q_ref, kt_ref, km_ref, kb_ref, vt_ref, vm_ref, vb_ref, kc_ref, vc_ref, x_ref, rowmask_ref,
               o_ref, tab_ref):
    b = pl.program_id(1)
    last = pl.num_programs(1) - 1

    @pl.when(b == 0)
    def _():
        _build_bias_tables(x_ref, rowmask_ref, tab_ref)

    masks = _head_lane_masks()
    step_rows = NA_STEP_BLOCKS * NA_Q_ROWS
    q3 = q_ref[...].reshape(step_rows, GRID_W, V7X_MXU)
    kw = jnp.concatenate([kt_ref[...], km_ref[...], kb_ref[...]], axis=0)
    vw = jnp.concatenate([vt_ref[...], vm_ref[...], vb_ref[...]], axis=0)
    kw3 = kw.reshape(step_rows + 2 * NA_HALO_ROWS, GRID_W, V7X_MXU)
    vw3 = vw.reshape(step_rows + 2 * NA_HALO_ROWS, GRID_W, V7X_MXU)
    kc = kc_ref[...].astype(BF16)
    vc = vc_ref[...].astype(BF16)
    for sb in range(NA_STEP_BLOCKS):
        cls = 1
        if sb == 0:
            cls = jnp.where(b == 0, 0, cls)
        if sb == NA_STEP_BLOCKS - 1:
            cls = jnp.where(b == last, 2, cls)
        r_lo = sb * NA_Q_ROWS
        for n in range(GRID_W // WIN_C):
            c0 = min(max(n * WIN_C - WIN_C // 2, 0), GRID_W - NA_SPAN)
            qn = q3[r_lo:r_lo + NA_Q_ROWS, n * WIN_C:(n + 1) * WIN_C, :].reshape(NA_NQ, V7X_MXU)
            kn = kw3[r_lo:r_lo + NA_KEY_ROWS, c0:c0 + NA_SPAN, :].reshape(NA_NK, V7X_MXU).astype(BF16)
            vn = vw3[r_lo:r_lo + NA_KEY_ROWS, c0:c0 + NA_SPAN, :].reshape(NA_NK, V7X_MXU).astype(BF16)
            qs = _stack_heads(qn, masks)
            s_lat = _dot_nt(qs, kn) + tab_ref[cls, n]
            s_ctx = _dot_nt(qs, kc)
            o = _unstack_heads(_softmax_pv([s_lat, s_ctx], [vn, vc]), masks, NA_NQ)
            o3 = o.astype(BF16).reshape(NA_Q_ROWS, WIN_C, V7X_MXU)
            for r in range(NA_Q_ROWS):
                row0 = (r_lo + r) * GRID_W + n * WIN_C
                o_ref[row0:row0 + WIN_C, :] = o3[r]


def _na_attention(q, k, v, kc, vc, compact, rowmask):
    t = q.shape[0]
    tq = NA_STEP_BLOCKS * NA_Q_ROWS * GRID_W
    th = NA_HALO_ROWS * GRID_W
    nb = t // tq
    n_halo_blocks = t // th
    hpb = tq // th
    ncb = GRID_W // WIN_C
    main = lambda: pl.BlockSpec((tq, V7X_MXU), lambda g, b: (b, g))
    top = lambda: pl.BlockSpec((th, V7X_MXU), lambda g, b: (jnp.maximum(b * hpb - 1, 0), g))
    bot = lambda: pl.BlockSpec(
        (th, V7X_MXU), lambda g, b: (jnp.minimum((b + 1) * hpb, n_halo_blocks - 1), g))
    ctx = lambda: pl.BlockSpec((kc.shape[0], V7X_MXU), lambda g, b: (0, g))
    return pl.pallas_call(
        _na_kernel,
        grid=(N_HEADS // NA_GROUP, nb),
        in_specs=[main(), top(), main(), bot(), top(), main(), bot(), ctx(), ctx(),
                  pl.BlockSpec((NA_GROUP,) + compact.shape[1:], lambda g, b: (g, 0, 0, 0)),
                  pl.BlockSpec(rowmask.shape, lambda g, b: (0, 0, 0, 0))],
        out_specs=main(),
        out_shape=jax.ShapeDtypeStruct((t, D_MODEL), BF16),
        scratch_shapes=[pltpu.VMEM((3, ncb, NA_GROUP * NA_NQ, NA_NK), F32)],
        compiler_params=_params(2),
        name="neighbourhood_attention",
    )(q, k, k, k, v, v, v, kc, vc, compact, rowmask)


def _ctx_attn_kernel(q_ref, k_ref, v_ref, o_ref):
    masks = _head_lane_masks()
    nq = q_ref.shape[0]
    qs = _stack_heads(q_ref[...], masks)
    k = k_ref[...].astype(BF16)
    v = v_ref[...].astype(BF16)
    o_ref[...] = _unstack_heads(_softmax_pv([_dot_nt(qs, k)], [v]), masks, nq).astype(BF16)


def _ctx_attention(q, k, v):
    t = q.shape[0]
    blk = pl.BlockSpec((t, V7X_MXU), lambda g: (0, g))
    return pl.pallas_call(
        _ctx_attn_kernel,
        grid=(N_HEADS // NA_GROUP,),
        in_specs=[blk, blk, blk],
        out_specs=blk,
        out_shape=jax.ShapeDtypeStruct((t, D_MODEL), BF16),
        compiler_params=_params(1),
        name="context_attention",
    )(q, k, v)


def _out_proj_kernel(o_ref, x_ref, mod_ref, w_ref, b_ref, y_ref, wb_ref):
    _cast_once(w_ref, wb_ref)
    y = _dot(o_ref[...], wb_ref[...]) + b_ref[...]
    y_ref[...] = x_ref[...] + mod_ref[5:6, :] * y


def _out_proj(o, x, mods, w, b, layer, stream, j, tm):
    t = x.shape[0]
    return pl.pallas_call(
        _out_proj_kernel,
        grid=(t // tm,),
        in_specs=[_tok_spec(tm), _tok_spec(tm),
                  _sel_spec((layer, stream), (N_MOD, D_MODEL)),
                  _sel_spec((j,), (D_MODEL, D_MODEL)),
                  _sel_spec((j,), (1, D_MODEL))],
        out_specs=_tok_spec(tm),
        out_shape=jax.ShapeDtypeStruct((t, D_MODEL), F32),
        scratch_shapes=[pltpu.VMEM((D_MODEL, D_MODEL), BF16)],
        compiler_params=_params(1),
        name="attn_out_proj",
    )(o, x, mods, w, b)


def _na_index_tables(rows):
    ncb = GRID_W // WIN_C
    n = np.arange(ncb)
    c0 = np.clip(n * WIN_C - WIN_C // 2, 0, GRID_W - NA_SPAN)
    q_col = n[:, None] * WIN_C + np.arange(WIN_C)[None, :]
    k_col = c0[:, None] + np.arange(NA_SPAN)[None, :]
    q_start = np.clip(q_col - WIN_C // 2, 0, GRID_W - WIN_C)
    col_valid = ((k_col[:, None, :] >= q_start[:, :, None])
                 & (k_col[:, None, :] < q_start[:, :, None] + WIN_C))
    col_idx = np.clip(k_col[:, None, :] - q_col[:, :, None], -(WIN_C - 1), WIN_C - 1) + WIN_C - 1
    wr = min(WIN_R, rows)
    rho = np.arange(NA_Q_ROWS)
    kappa = np.arange(NA_KEY_ROWS)
    row_valid = []
    for r_start in (0, NA_Q_ROWS, rows - NA_Q_ROWS):
        r = r_start + rho
        j = r_start - NA_HALO_ROWS + kappa
        r0 = np.clip(r - wr // 2, 0, rows - wr)
        row_valid.append((j[None, :] >= r0[:, None]) & (j[None, :] < r0[:, None] + wr))
    row_valid = np.repeat(np.stack(row_valid), NA_SPAN, axis=-1)
    rowmask = row_valid.astype(np.float32).reshape(3, NA_Q_ROWS, 1, NA_NK)
    return col_idx.astype(np.int32), col_valid, rowmask


def _na_compact_bias(rpb, col_idx, col_valid):
    ncb = GRID_W // WIN_C
    by_col = jnp.take(rpb, jnp.asarray(col_idx.reshape(-1)), axis=2)
    by_col = by_col.reshape(N_HEADS, 2 * WIN_R - 1, ncb, WIN_C, NA_SPAN).transpose(0, 2, 3, 1, 4)
    by_col = jnp.where(jnp.asarray(col_valid)[None, :, :, None, :], by_col, MASK_VALUE)
    shift = WIN_R - 1 - NA_HALO_ROWS
    front = NA_Q_ROWS - 1 - shift
    total = NA_Q_ROWS + NA_KEY_ROWS
    back = total - front - (2 * WIN_R - 1)
    padded = jnp.pad(by_col, ((0, 0), (0, 0), (0, 0), (front, back), (0, 0)))
    return padded.reshape(N_HEADS, ncb, WIN_C, total * NA_SPAN)


def kernel(x, c, ctx, c_ctx, mod_w, mod_b, norm_g, ffn_w_gate, ffn_w_up, ffn_w_down, conv_w_pw1, conv_b_pw1, conv_w_dw, conv_b_dw, conv_ln_g, conv_ln_b, conv_w_pw2, conv_b_pw2, na_w_qkv, na_b_qkv, na_rpb, na_w_o, na_b_o, final_g):
    b, t, d = x.shape
    assert b == 1 and d == D_MODEL and t % (NA_STEP_BLOCKS * NA_Q_ROWS * GRID_W) == 0
    n_ctx = ctx.shape[1]
    tm = {0: 512, 1: n_ctx}
    tm_mm = {0: 1024, 1: n_ctx}
    col_idx, col_valid, rowmask = _na_index_tables(t // GRID_W)
    rowmask = jnp.asarray(rowmask)
    xs = {0: x.reshape(t, d), 1: ctx.reshape(n_ctx, d)}

    cc = jnp.zeros((V7X_SUBLANES, d), F32).at[0].set(c[0]).at[1].set(c_ctx)
    mods = _modulations(cc, mod_w, mod_b).reshape(DEPTH, V7X_SUBLANES, N_MOD, d)
    ng = norm_g.reshape(DEPTH, 3, 1, d)
    row = lambda v: v.reshape(v.shape[0], 1, v.shape[1])
    conv_vecs = tuple(row(v) for v in (conv_b_dw, conv_ln_g, conv_ln_b))
    ffn_w32 = (ffn_w_gate, ffn_w_up, ffn_w_down)
    w_cur = tuple(w[0, 0].astype(BF16) for w in ffn_w32)

    for i in range(DEPTH):
        mixer, j = i % 2, i // 2
        last = i == DEPTH - 1
        for half in (0, 1):
            if half == 1:
                if mixer == 0:
                    for s in ((0,) if last else (0, 1)):
                        u = _conv_glu(xs[s], mods, ng, conv_w_pw1, row(conv_b_pw1), i, s, j, tm_mm[s])
                        xs[s] = _conv_tail(u, xs[s], mods, conv_w_dw, *conv_vecs, conv_w_pw2,
                                           row(conv_b_pw2), i, s, j, tm[s])
                else:
                    compact = _na_compact_bias(na_rpb[j], col_idx, col_valid)
                    qkv = {s: _qkv_proj(xs[s], mods, ng, na_w_qkv, row(na_b_qkv), i, s, j, tm_mm[s])
                           for s in (0, 1)}
                    o = {0: _na_attention(*qkv[0], qkv[1][1], qkv[1][2], compact, rowmask)}
                    if not last:
                        o[1] = _ctx_attention(*qkv[1])
                    for s in o:
                        xs[s] = _out_proj(o[s], xs[s], mods, na_w_o, row(na_b_o), i, s, j, tm_mm[s])
            final = last and half == 1
            nxt = None if final else (ffn_w32, i + half, 1 - half)
            xs[0], w_next = _ffn_half(xs[0], mods, ng, w_cur, i, 0, half, tm[0],
                                      final_g=final_g if final else None, next_w=nxt)
            if not final:
                xs[1], _ = _ffn_half(xs[1], mods, ng, w_cur, i, 1, half, tm[1])
                w_cur = w_next
    return xs[0].reshape(b, t, d)
```

```python
import functools

import numpy as np
import jax
import jax.numpy as jnp
from jax import lax
from jax.experimental import pallas as pl
from jax.experimental.pallas import tpu as pltpu

D_MODEL = 1024
DEPTH = 4
GRID_W = 64
N_HEADS = 16
HEAD_DIM = D_MODEL // N_HEADS
D_FF = 2816
CONV_WIDTH = 31
WIN_R = 8
WIN_C = 16
N_MOD = 9
RMS_EPS = 1e-6
LN_EPS = 1e-5
MASK_VALUE = -1e30

F32 = jnp.float32
BF16 = jnp.bfloat16

V7X_LANES = 128
V7X_SUBLANES = 8
V7X_MXU = 256
FF_CHUNK = V7X_MXU
FFN_PREP_PIECES = 8
CONV_HALO = 16
CONV_STRIDE = 4
CONV_TAPS_PER_PASS = 16
NA_Q_ROWS = 8
NA_STEP_BLOCKS = 2
NA_HALO_ROWS = WIN_R // 2
NA_KEY_ROWS = NA_Q_ROWS + 2 * NA_HALO_ROWS
NA_SPAN = 2 * WIN_C
NA_GROUP = V7X_MXU // HEAD_DIM
NA_NQ = NA_Q_ROWS * WIN_C
NA_NK = NA_KEY_ROWS * NA_SPAN
VMEM_LIMIT = 56 * 1024 * 1024


def _sel_spec(prefix, tail):
    prefix, tail = tuple(prefix), tuple(tail)
    return pl.BlockSpec((None,) * len(prefix) + tail, lambda *_: prefix + (0,) * len(tail),
                        pipeline_mode=pl.Buffered(1))


def _tok_spec(tm):
    return pl.BlockSpec((tm, D_MODEL), lambda i: (i, 0))


def _params(n_grid_dims):
    return pltpu.CompilerParams(
        dimension_semantics=("arbitrary",) * n_grid_dims, vmem_limit_bytes=VMEM_LIMIT)


def _dot(a, b):
    return jnp.dot(a, b, preferred_element_type=F32)


def _dot_nt(a, b):
    return lax.dot_general(a, b, (((1,), (1,)), ((), ())), preferred_element_type=F32)


def _rmsnorm(x, g):
    return x * lax.rsqrt(jnp.mean(x * x, axis=-1, keepdims=True) + RMS_EPS) * g


def _rms_mod(x, g, shift, scale):
    return _rmsnorm(x, g) * (1.0 + scale) + shift


def _cast_once(w_ref, wb_ref):
    @pl.when(pl.program_id(0) == 0)
    def _():
        wb_ref[...] = w_ref[...].astype(BF16)


def _next_tok_spec(tm, n_blocks):
    return pl.BlockSpec((tm, D_MODEL), lambda i: (jnp.minimum(i + 1, n_blocks - 1), 0))


def _prime_norm(x_ref, h_ref, norm):
    i = pl.program_id(0)

    @pl.when(i == 0)
    def _():
        h_ref[0] = norm(x_ref[...])

    return i % 2


def _prep_next(xn_ref, h_ref, slot, k, n_pieces, norm):
    piece = xn_ref.shape[0] // n_pieces
    rows = slice(k * piece, (k + 1) * piece)
    h_ref[1 - slot, rows, :] = norm(xn_ref[rows, :])


def _mod_kernel(cc_ref, w_ref, b_ref, o_ref):
    s = jax.nn.silu(cc_ref[...]).astype(BF16)
    o_ref[...] = _dot(s, w_ref[...].astype(BF16)) + b_ref[...]


def _modulations(cc, mod_w, mod_b):
    rows = cc.shape[0]
    return pl.pallas_call(
        _mod_kernel,
        grid=(DEPTH, N_MOD),
        in_specs=[
            pl.BlockSpec((rows, D_MODEL), lambda l, j: (0, 0)),
            pl.BlockSpec((None, D_MODEL, D_MODEL), lambda l, j: (l, 0, j)),
            pl.BlockSpec((None, 1, D_MODEL), lambda l, j: (l, 0, j)),
        ],
        out_specs=pl.BlockSpec((None, rows, D_MODEL), lambda l, j: (l, 0, j)),
        out_shape=jax.ShapeDtypeStruct((DEPTH, rows, N_MOD * D_MODEL), F32),
        compiler_params=_params(2),
        name="modulations",
    )(cc, mod_w, mod_b.reshape(DEPTH, 1, N_MOD * D_MODEL))


def _ffn_kernel(x_ref, xn_ref, mod_ref, g_ref, wg_ref, wu_ref, wd_ref, *rest, mod0, final, n_cast):
    cast_in, rest = rest[:n_cast], rest[n_cast:]
    final_ref = rest[0] if final else None
    o_ref = rest[1 if final else 0]
    cast_out = rest[(2 if final else 1):-1]
    h_ref = rest[-1]
    shift, scale, gate = (mod_ref[mod0 + k:mod0 + k + 1, :] for k in range(3))
    norm = lambda v: _rms_mod(v, g_ref[...], shift, scale).astype(BF16)
    slot = _prime_norm(x_ref, h_ref, norm)
    x = x_ref[...]
    acc = jnp.zeros(x.shape, F32)
    for f in range(D_FF // FF_CHUNK):
        sl = slice(f * FF_CHUNK, (f + 1) * FF_CHUNK)
        h = h_ref[slot]
        a = jax.nn.silu(_dot(h, wg_ref[:, sl])) * _dot(h, wu_ref[:, sl])
        acc = acc + _dot(a.astype(BF16), wd_ref[sl, :])
        if f < FFN_PREP_PIECES:
            _prep_next(xn_ref, h_ref, slot, f, FFN_PREP_PIECES, norm)
    out = x + 0.5 * gate * acc
    if final:
        out = _rmsnorm(out, final_ref[...])
    o_ref[...] = out
    for src, dst in zip(cast_in, cast_out):
        dst[...] = src[...].astype(BF16)


def _ffn_half(x, mods, norm_g, w_bf16, layer, stream, half, tm, final_g=None, next_w=None):
    t = x.shape[0]
    final = final_g is not None
    n_blocks = t // tm
    wg, wu, wd = w_bf16
    in_specs = [
        _tok_spec(tm),
        _next_tok_spec(tm, n_blocks),
        _sel_spec((layer, stream), (N_MOD, D_MODEL)),
        _sel_spec((layer, 2 * half), (1, D_MODEL)),
        _sel_spec((), (D_MODEL, D_FF)),
        _sel_spec((), (D_MODEL, D_FF)),
        _sel_spec((), (D_FF, D_MODEL)),
    ]
    args = [x, x, mods, norm_g, wg, wu, wd]
    out_specs = [_tok_spec(tm)]
    out_shape = [jax.ShapeDtypeStruct((t, D_MODEL), F32)]
    n_cast = 0
    if next_w is not None:
        stacked, nl, nh = next_w
        n_cast = len(stacked)
        for w in stacked:
            rows, cols = w.shape[2:]
            slab = next(s for s in range(16, rows + 1, 16) if rows % s == 0 and rows // s <= n_blocks)
            slab_of = lambda i, n_slabs=rows // slab: jnp.minimum(i, n_slabs - 1)
            in_specs.append(pl.BlockSpec((None, None, slab, cols),
                                         lambda i, slab_of=slab_of: (nl, nh, slab_of(i), 0)))
            args.append(w)
            out_specs.append(pl.BlockSpec((slab, cols), lambda i, slab_of=slab_of: (slab_of(i), 0)))
            out_shape.append(jax.ShapeDtypeStruct((rows, cols), BF16))
    if final:
        in_specs.append(_sel_spec((), (1, D_MODEL)))
        args.append(final_g.reshape(1, D_MODEL))
    outs = pl.pallas_call(
        functools.partial(_ffn_kernel, mod0=6 * half, final=final, n_cast=n_cast),
        grid=(n_blocks,),
        in_specs=in_specs,
        out_specs=out_specs,
        out_shape=out_shape,
        scratch_shapes=[pltpu.VMEM((2, tm, D_MODEL), BF16)],
        compiler_params=_params(1),
        name="ffn_half",
    )(*args)
    return outs[0], tuple(outs[1:])


def _conv_glu_kernel(x_ref, mod_ref, g_ref, w1_ref, b1_ref, u_ref, w1b_ref):
    _cast_once(w1_ref, w1b_ref)
    h = _rms_mod(x_ref[...], g_ref[...], mod_ref[3:4, :], mod_ref[4:5, :]).astype(BF16)
    a = _dot(h, w1b_ref[:, :D_MODEL]) + b1_ref[:, :D_MODEL]
    gt = _dot(h, w1b_ref[:, D_MODEL:]) + b1_ref[:, D_MODEL:]
    u_ref[...] = a * jax.nn.sigmoid(gt)


def _conv_glu(x, mods, norm_g, w1, b1, layer, stream, j, tm):
    t = x.shape[0]
    return pl.pallas_call(
        _conv_glu_kernel,
        grid=(t // tm,),
        in_specs=[
            _tok_spec(tm),
            _sel_spec((layer, stream), (N_MOD, D_MODEL)),
            _sel_spec((layer, 1), (1, D_MODEL)),
            _sel_spec((j,), (D_MODEL, 2 * D_MODEL)),
            _sel_spec((j,), (1, 2 * D_MODEL)),
        ],
        out_specs=_tok_spec(tm),
        out_shape=jax.ShapeDtypeStruct((t, D_MODEL), F32),
        scratch_shapes=[pltpu.VMEM((D_MODEL, 2 * D_MODEL), BF16)],
        compiler_params=_params(1),
        name="conv_glu",
    )(x, mods, norm_g, w1, b1)


def _conv_tail_kernel(ul_ref, um_ref, ur_ref, x_ref, mod_ref, wdw_ref, bdw_ref, lng_ref, lnb_ref,
                      w2_ref, b2_ref, o_ref, win_ref, cv_ref, w2b_ref, *, tm):
    _cast_once(w2_ref, w2b_ref)
    i = pl.program_id(0)
    last = pl.num_programs(0) - 1
    n_slabs = D_MODEL // V7X_LANES
    for c in range(n_slabs):
        cs = slice(c * V7X_LANES, (c + 1) * V7X_LANES)
        win_ref[c, 0:CONV_HALO, :] = jnp.where(i > 0, ul_ref[:, cs], 0.0)
        win_ref[c, CONV_HALO:CONV_HALO + tm, :] = um_ref[:, cs]
        win_ref[c, CONV_HALO + tm:, :] = jnp.where(i < last, ur_ref[:, cs], 0.0)
    off0 = CONV_HALO - CONV_WIDTH // 2
    group = V7X_SUBLANES * CONV_STRIDE
    for c in range(n_slabs):
        cs = slice(c * V7X_LANES, (c + 1) * V7X_LANES)
        strided = lambda start: pl.ds(start, V7X_SUBLANES, stride=CONV_STRIDE)
        for k0 in range(0, CONV_WIDTH, CONV_TAPS_PER_PASS):
            taps = [wdw_ref[k:k + 1, cs] for k in range(k0, min(k0 + CONV_TAPS_PER_PASS, CONV_WIDTH))]
            for g in range(tm // group):
                v = [win_ref[c, strided(g * group + off0 + k0 + m), :]
                     for m in range(len(taps) + CONV_STRIDE - 1)]
                for j in range(CONV_STRIDE):
                    acc = taps[0] * v[j]
                    for k in range(1, len(taps)):
                        acc = acc + taps[k] * v[j + k]
                    if k0 > 0:
                        acc = acc + cv_ref[c, strided(g * group + j), :]
                    cv_ref[c, strided(g * group + j), :] = acc
    cv = jnp.concatenate([cv_ref[c] for c in range(n_slabs)], axis=1) + bdw_ref[...]
    mu = jnp.mean(cv, axis=-1, keepdims=True)
    cen = cv - mu
    var = jnp.mean(cen * cen, axis=-1, keepdims=True)
    y = cen * lax.rsqrt(var + LN_EPS) * lng_ref[...] + lnb_ref[...]
    out = _dot(jax.nn.silu(y).astype(BF16), w2b_ref[...]) + b2_ref[...]
    o_ref[...] = x_ref[...] + mod_ref[5:6, :] * out


def _conv_tail(u, x, mods, wdw, bdw, lng, lnb, w2, b2, layer, stream, j, tm):
    t = x.shape[0]
    hb = tm // CONV_HALO
    n_halo_blocks = t // CONV_HALO
    vec = _sel_spec((j,), (1, D_MODEL))
    return pl.pallas_call(
        functools.partial(_conv_tail_kernel, tm=tm),
        grid=(t // tm,),
        in_specs=[
            pl.BlockSpec((CONV_HALO, D_MODEL), lambda i: (jnp.maximum(i * hb - 1, 0), 0)),
            _tok_spec(tm),
            pl.BlockSpec((CONV_HALO, D_MODEL),
                         lambda i: (jnp.minimum((i + 1) * hb, n_halo_blocks - 1), 0)),
            _tok_spec(tm),
            _sel_spec((layer, stream), (N_MOD, D_MODEL)),
            _sel_spec((j,), (CONV_WIDTH, D_MODEL)),
            vec, vec, vec,
            _sel_spec((j,), (D_MODEL, D_MODEL)),
            vec,
        ],
        out_specs=_tok_spec(tm),
        out_shape=jax.ShapeDtypeStruct((t, D_MODEL), F32),
        scratch_shapes=[
            pltpu.VMEM((D_MODEL // V7X_LANES, tm + 2 * CONV_HALO, V7X_LANES), F32),
            pltpu.VMEM((D_MODEL // V7X_LANES, tm, V7X_LANES), F32),
            pltpu.VMEM((D_MODEL, D_MODEL), BF16),
        ],
        compiler_params=_params(1),
        name="conv_tail",
    )(u, u, u, x, mods, wdw, bdw, lng, lnb, w2, b2)


def _qkv_kernel(x_ref, mod_ref, g_ref, w_ref, b_ref, q_ref, k_ref, v_ref, wb_ref):
    _cast_once(w_ref, wb_ref)
    h = _rms_mod(x_ref[...], g_ref[...], mod_ref[3:4, :], mod_ref[4:5, :]).astype(BF16)
    d = D_MODEL
    q = _dot(h, wb_ref[:, 0:d]) + b_ref[:, 0:d]
    q_ref[...] = (q * (HEAD_DIM ** -0.5)).astype(BF16)
    k_ref[...] = _dot(h, wb_ref[:, d:2 * d]) + b_ref[:, d:2 * d]
    v_ref[...] = _dot(h, wb_ref[:, 2 * d:3 * d]) + b_ref[:, 2 * d:3 * d]


def _qkv_proj(x, mods, norm_g, w, b, layer, stream, j, tm):
    t = x.shape[0]
    return pl.pallas_call(
        _qkv_kernel,
        grid=(t // tm,),
        in_specs=[
            _tok_spec(tm),
            _sel_spec((layer, stream), (N_MOD, D_MODEL)),
            _sel_spec((layer, 1), (1, D_MODEL)),
            _sel_spec((j,), (D_MODEL, 3 * D_MODEL)),
            _sel_spec((j,), (1, 3 * D_MODEL)),
        ],
        out_specs=[_tok_spec(tm)] * 3,
        out_shape=[
            jax.ShapeDtypeStruct((t, D_MODEL), BF16),
            jax.ShapeDtypeStruct((t, D_MODEL), F32),
            jax.ShapeDtypeStruct((t, D_MODEL), F32),
        ],
        scratch_shapes=[pltpu.VMEM((D_MODEL, 3 * D_MODEL), BF16)],
        compiler_params=_params(1),
        name="qkv_proj",
    )(x, mods, norm_g, w, b)


def _head_lane_masks():
    lane = lax.broadcasted_iota(jnp.int32, (1, V7X_MXU), 1)
    return [(lane >= h * HEAD_DIM) & (lane < (h + 1) * HEAD_DIM) for h in range(NA_GROUP)]


def _stack_heads(q, masks):
    return jnp.concatenate([q * m.astype(q.dtype) for m in masks], axis=0)


def _softmax_pv(scores, values):
    mx = functools.reduce(jnp.maximum, [jnp.max(s, axis=-1, keepdims=True) for s in scores])
    ps = [jnp.exp(s - mx) for s in scores]
    denom = functools.reduce(jnp.add, [jnp.sum(p, axis=-1, keepdims=True) for p in ps])
    o = functools.reduce(jnp.add, [_dot(p.astype(BF16), v) for p, v in zip(ps, values)])
    return o / denom


def _unstack_heads(o, masks, nq):
    out = o[(NA_GROUP - 1) * nq:]
    for h in range(NA_GROUP - 2, -1, -1):
        out = jnp.where(masks[h], o[h * nq:(h + 1) * nq], out)
    return out


def _build_bias_tables(x_ref, rowmask_ref, tab_ref):
    top = (NA_Q_ROWS - 1) * NA_SPAN
    for n in range(GRID_W // WIN_C):
        for hh in range(NA_GROUP):
            compact = x_ref[hh, n]
            for rho in range(NA_Q_ROWS):
                strip = compact[:, top - rho * NA_SPAN:top - rho * NA_SPAN + NA_NK]
                r0 = hh * NA_NQ + rho * WIN_C
                for cls in range(3):
                    tab_ref[cls, n, r0:r0 + WIN_C, :] = jnp.where(
                        rowmask_ref[cls, rho] > 0.5, strip, MASK_VALUE)


def _na_kernel(q_ref, kt_ref, km_ref, kb_ref, vt_ref, vm_ref, vb_ref, kc_ref, vc_ref, x_ref, rowmask_ref,
               o_ref, tab_ref):
    b = pl.program_id(1)
    last = pl.num_programs(1) - 1

    @pl.when(b == 0)
    def _():
        _build_bias_tables(x_ref, rowmask_ref, tab_ref)

    masks = _head_lane_masks()
    step_rows = NA_STEP_BLOCKS * NA_Q_ROWS
    q3 = q_ref[...].reshape(step_rows, GRID_W, V7X_MXU)
    kw = jnp.concatenate([kt_ref[...], km_ref[...], kb_ref[...]], axis=0)
    vw = jnp.concatenate([vt_ref[...], vm_ref[...], vb_ref[...]], axis=0)
    kw3 = kw.reshape(step_rows + 2 * NA_HALO_ROWS, GRID_W, V7X_MXU)
    vw3 = vw.reshape(step_rows + 2 * NA_HALO_ROWS, GRID_W, V7X_MXU)
    kc = kc_ref[...].astype(BF16)
    vc = vc_ref[...].astype(BF16)
    for sb in range(NA_STEP_BLOCKS):
        cls = 1
        if sb == 0:
            cls = jnp.where(b == 0, 0, cls)
        if sb == NA_STEP_BLOCKS - 1:
            cls = jnp.where(b == last, 2, cls)
        r_lo = sb * NA_Q_ROWS
        for n in range(GRID_W // WIN_C):
            c0 = min(max(n * WIN_C - WIN_C // 2, 0), GRID_W - NA_SPAN)
            qn = q3[r_lo:r_lo + NA_Q_ROWS, n * WIN_C:(n + 1) * WIN_C, :].reshape(NA_NQ, V7X_MXU)
            kn = kw3[r_lo:r_lo + NA_KEY_ROWS, c0:c0 + NA_SPAN, :].reshape(NA_NK, V7X_MXU).astype(BF16)
            vn = vw3[r_lo:r_lo + NA_KEY_ROWS, c0:c0 + NA_SPAN, :].reshape(NA_NK, V7X_MXU).astype(BF16)
            qs = _stack_heads(qn, masks)
            s_lat = _dot_nt(qs, kn) + tab_ref[cls, n]
            s_ctx = _dot_nt(qs, kc)
            o = _unstack_heads(_softmax_pv([s_lat, s_ctx], [vn, vc]), masks, NA_NQ)
            o3 = o.astype(BF16).reshape(NA_Q_ROWS, WIN_C, V7X_MXU)
            for r in range(NA_Q_ROWS):
                row0 = (r_lo + r) * GRID_W + n * WIN_C
                o_ref[row0:row0 + WIN_C, :] = o3[r]


def _na_attention(q, k, v, kc, vc, compact, rowmask):
    t = q.shape[0]
    tq = NA_STEP_BLOCKS * NA_Q_ROWS * GRID_W
    th = NA_HALO_ROWS * GRID_W
    nb = t // tq
    n_halo_blocks = t // th
    hpb = tq // th
    ncb = GRID_W // WIN_C
    main = lambda: pl.BlockSpec((tq, V7X_MXU), lambda g, b: (b, g))
    top = lambda: pl.BlockSpec((th, V7X_MXU), lambda g, b: (jnp.maximum(b * hpb - 1, 0), g))
    bot = lambda: pl.BlockSpec(
        (th, V7X_MXU), lambda g, b: (jnp.minimum((b + 1) * hpb, n_halo_blocks - 1), g))
    ctx = lambda: pl.BlockSpec((kc.shape[0], V7X_MXU), lambda g, b: (0, g))
    return pl.pallas_call(
        _na_kernel,
        grid=(N_HEADS // NA_GROUP, nb),
        in_specs=[main(), top(), main(), bot(), top(), main(), bot(), ctx(), ctx(),
                  pl.BlockSpec((NA_GROUP,) + compact.shape[1:], lambda g, b: (g, 0, 0, 0)),
                  pl.BlockSpec(rowmask.shape, lambda g, b: (0, 0, 0, 0))],
        out_specs=main(),
        out_shape=jax.ShapeDtypeStruct((t, D_MODEL), BF16),
        scratch_shapes=[pltpu.VMEM((3, ncb, NA_GROUP * NA_NQ, NA_NK), F32)],
        compiler_params=_params(2),
        name="neighbourhood_attention",
    )(q, k, k, k, v, v, v, kc, vc, compact, rowmask)


def _ctx_attn_kernel(q_ref, k_ref, v_ref, o_ref):
    masks = _head_lane_masks()
    nq = q_ref.shape[0]
    qs = _stack_heads(q_ref[...], masks)
    k = k_ref[...].astype(BF16)
    v = v_ref[...].astype(BF16)
    o_ref[...] = _unstack_heads(_softmax_pv([_dot_nt(qs, k)], [v]), masks, nq).astype(BF16)


def _ctx_attention(q, k, v):
    t = q.shape[0]
    blk = pl.BlockSpec((t, V7X_MXU), lambda g: (0, g))
    return pl.pallas_call(
        _ctx_attn_kernel,
        grid=(N_HEADS // NA_GROUP,),
        in_specs=[blk, blk, blk],
        out_specs=blk,
        out_shape=jax.ShapeDtypeStruct((t, D_MODEL), BF16),
        compiler_params=_params(1),
        name="context_attention",
    )(q, k, v)


def _out_proj_kernel(o_ref, x_ref, mod_ref, w_ref, b_ref, y_ref, wb_ref):
    _cast_once(w_ref, wb_ref)
    y = _dot(o_ref[...], wb_ref[...]) + b_ref[...]
    y_ref[...] = x_ref[...] + mod_ref[5:6, :] * y


def _out_proj(o, x, mods, w, b, layer, stream, j, tm):
    t = x.shape[0]
    return pl.pallas_call(
        _out_proj_kernel,
        grid=(t // tm,),
        in_specs=[_tok_spec(tm), _tok_spec(tm),
                  _sel_spec((layer, stream), (N_MOD, D_MODEL)),
                  _sel_spec((j,), (D_MODEL, D_MODEL)),
                  _sel_spec((j,), (1, D_MODEL))],
        out_specs=_tok_spec(tm),
        out_shape=jax.ShapeDtypeStruct((t, D_MODEL), F32),
        scratch_shapes=[pltpu.VMEM((D_MODEL, D_MODEL), BF16)],
        compiler_params=_params(1),
        name="attn_out_proj",
    )(o, x, mods, w, b)


def _na_index_tables(rows):
    ncb = GRID_W // WIN_C
    n = np.arange(ncb)
    c0 = np.clip(n * WIN_C - WIN_C // 2, 0, GRID_W - NA_SPAN)
    q_col = n[:, None] * WIN_C + np.arange(WIN_C)[None, :]
    k_col = c0[:, None] + np.arange(NA_SPAN)[None, :]
    q_start = np.clip(q_col - WIN_C // 2, 0, GRID_W - WIN_C)
    col_valid = ((k_col[:, None, :] >= q_start[:, :, None])
                 & (k_col[:, None, :] < q_start[:, :, None] + WIN_C))
    col_idx = np.clip(k_col[:, None, :] - q_col[:, :, None], -(WIN_C - 1), WIN_C - 1) + WIN_C - 1
    wr = min(WIN_R, rows)
    rho = np.arange(NA_Q_ROWS)
    kappa = np.arange(NA_KEY_ROWS)
    row_valid = []
    for r_start in (0, NA_Q_ROWS, rows - NA_Q_ROWS):
        r = r_start + rho
        j = r_start - NA_HALO_ROWS + kappa
        r0 = np.clip(r - wr // 2, 0, rows - wr)
        row_valid.append((j[None, :] >= r0[:, None]) & (j[None, :] < r0[:, None] + wr))
    row_valid = np.repeat(np.stack(row_valid), NA_SPAN, axis=-1)
    rowmask = row_valid.astype(np.float32).reshape(3, NA_Q_ROWS, 1, NA_NK)
    return col_idx.astype(np.int32), col_valid, rowmask


def _na_compact_bias(rpb, col_idx, col_valid):
    ncb = GRID_W // WIN_C
    by_col = jnp.take(rpb, jnp.asarray(col_idx.reshape(-1)), axis=2)
    by_col = by_col.reshape(N_HEADS, 2 * WIN_R - 1, ncb, WIN_C, NA_SPAN).transpose(0, 2, 3, 1, 4)
    by_col = jnp.where(jnp.asarray(col_valid)[None, :, :, None, :], by_col, MASK_VALUE)
    shift = WIN_R - 1 - NA_HALO_ROWS
    front = NA_Q_ROWS - 1 - shift
    total = NA_Q_ROWS + NA_KEY_ROWS
    back = total - front - (2 * WIN_R - 1)
    padded = jnp.pad(by_col, ((0, 0), (0, 0), (0, 0), (front, back), (0, 0)))
    return padded.reshape(N_HEADS, ncb, WIN_C, total * NA_SPAN)


def kernel(x, c, ctx, c_ctx, mod_w, mod_b, norm_g, ffn_w_gate, ffn_w_up, ffn_w_down, conv_w_pw1, conv_b_pw1, conv_w_dw, conv_b_dw, conv_ln_g, conv_ln_b, conv_w_pw2, conv_b_pw2, na_w_qkv, na_b_qkv, na_rpb, na_w_o, na_b_o, final_g):
    b, t, d = x.shape
    assert b == 1 and d == D_MODEL and t % (NA_STEP_BLOCKS * NA_Q_ROWS * GRID_W) == 0
    n_ctx = ctx.shape[1]
    tm = {0: 512, 1: n_ctx}
    tm_mm = {0: 1024, 1: n_ctx}
    col_idx, col_valid, rowmask = _na_index_tables(t // GRID_W)
    rowmask = jnp.asarray(rowmask)
    xs = {0: x.reshape(t, d), 1: ctx.reshape(n_ctx, d)}

    cc = jnp.zeros((V7X_SUBLANES, d), F32).at[0].set(c[0]).at[1].set(c_ctx)
    mods = _modulations(cc, mod_w, mod_b).reshape(DEPTH, V7X_SUBLANES, N_MOD, d)
    ng = norm_g.reshape(DEPTH, 3, 1, d)
    row = lambda v: v.reshape(v.shape[0], 1, v.shape[1])
    conv_vecs = tuple(row(v) for v in (conv_b_dw, conv_ln_g, conv_ln_b))
    ffn_w32 = (ffn_w_gate, ffn_w_up, ffn_w_down)
    w_cur = tuple(w[0, 0].astype(BF16) for w in ffn_w32)

    for i in range(DEPTH):
        mixer, j = i % 2, i // 2
        last = i == DEPTH - 1
        for half in (0, 1):
            if half == 1:
                if mixer == 0:
                    for s in ((0,) if last else (0, 1)):
                        u = _conv_glu(xs[s], mods, ng, conv_w_pw1, row(conv_b_pw1), i, s, j, tm_mm[s])
                        xs[s] = _conv_tail(u, xs[s], mods, conv_w_dw, *conv_vecs, conv_w_pw2,
                                           row(conv_b_pw2), i, s, j, tm[s])
                else:
                    compact = _na_compact_bias(na_rpb[j], col_idx, col_valid)
                    qkv = {s: _qkv_proj(xs[s], mods, ng, na_w_qkv, row(na_b_qkv), i, s, j, tm_mm[s])
                           for s in (0, 1)}
                    o = {0: _na_attention(*qkv[0], qkv[1][1], qkv[1][2], compact, rowmask)}
                    if not last:
                        o[1] = _ctx_attention(*qkv[1])
                    for s in o:
                        xs[s] = _out_proj(o[s], xs[s], mods, na_w_o, row(na_b_o), i, s, j, tm_mm[s])
            final = last and half == 1
            nxt = None if final else (ffn_w32, i + half, 1 - half)
            xs[0], w_next = _ffn_half(xs[0], mods, ng, w_cur, i, 0, half, tm[0],
                                      final_g=final_g if final else None, next_w=nxt)
            if not final:
                xs[1], _ = _ffn_half(xs[1], mods, ng, w_cur, i, 1, half, tm[1])
                w_cur = w_next
    return xs[0].reshape(b, t, d)
```

```python
import functools

import numpy as np
import jax
import jax.numpy as jnp
from jax import lax
from jax.experimental import pallas as pl
from jax.experimental.pallas import tpu as pltpu

D_MODEL = 1024
DEPTH = 4
GRID_W = 64
N_HEADS = 16
HEAD_DIM = D_MODEL // N_HEADS
D_FF = 2816
CONV_WIDTH = 31
WIN_R = 8
WIN_C = 16
N_MOD = 9
RMS_EPS = 1e-6
LN_EPS = 1e-5
MASK_VALUE = -1e30
LOG2_E = 1.4426950408889634
SCORE_SCALE = HEAD_DIM ** -0.5 * LOG2_E

F32 = jnp.float32
BF16 = jnp.bfloat16

V7X_LANES = 128
V7X_SUBLANES = 8
V7X_MXU = 256
FF_CHUNK = V7X_MXU
FFN_PREP_PIECES = 8
CONV_HALO = 16
CONV_STRIDE = 4
CONV_TAPS_PER_PASS = 16
NA_Q_ROWS = 8
NA_STEP_BLOCKS = 4
NA_HALO_ROWS = WIN_R // 2
NA_KEY_ROWS = NA_Q_ROWS + 2 * NA_HALO_ROWS
NA_SPAN = 2 * WIN_C
NA_GROUP = V7X_MXU // HEAD_DIM
NA_NQ = NA_Q_ROWS * WIN_C
NA_NK = NA_KEY_ROWS * NA_SPAN
VMEM_LIMIT = 56 * 1024 * 1024


def _sel_spec(prefix, tail):
    prefix, tail = tuple(prefix), tuple(tail)
    return pl.BlockSpec((None,) * len(prefix) + tail, lambda *_: prefix + (0,) * len(tail),
                        pipeline_mode=pl.Buffered(1))


def _tok_spec(tm):
    return pl.BlockSpec((tm, D_MODEL), lambda i: (i, 0))


def _params(n_grid_dims):
    return pltpu.CompilerParams(
        dimension_semantics=("arbitrary",) * n_grid_dims, vmem_limit_bytes=VMEM_LIMIT)


def _dot(a, b):
    return jnp.dot(a, b, preferred_element_type=F32)


def _dot_nt(a, b):
    return lax.dot_general(a, b, (((1,), (1,)), ((), ())), preferred_element_type=F32)


def _rmsnorm(x, g):
    return x * lax.rsqrt(jnp.mean(x * x, axis=-1, keepdims=True) + RMS_EPS) * g


def _rms_mod(x, g, shift, scale):
    return _rmsnorm(x, g) * (1.0 + scale) + shift


def _cast_once(w_ref, wb_ref):
    @pl.when(pl.program_id(0) == 0)
    def _():
        wb_ref[...] = w_ref[...].astype(BF16)


def _next_tok_spec(tm, n_blocks):
    return pl.BlockSpec((tm, D_MODEL), lambda i: (jnp.minimum(i + 1, n_blocks - 1), 0))


def _prime_norm(x_ref, h_ref, norm):
    i = pl.program_id(0)

    @pl.when(i == 0)
    def _():
        h_ref[0] = norm(x_ref[...])

    return i % 2


def _prep_next(xn_ref, h_ref, slot, k, n_pieces, norm):
    piece = xn_ref.shape[0] // n_pieces
    rows = slice(k * piece, (k + 1) * piece)
    h_ref[1 - slot, rows, :] = norm(xn_ref[rows, :])


def _mod_kernel(cc_ref, w_ref, b_ref, o_ref):
    s = jax.nn.silu(cc_ref[...]).astype(BF16)
    o_ref[...] = _dot(s, w_ref[...].astype(BF16)) + b_ref[...]


def _modulations(cc, mod_w, mod_b):
    rows = cc.shape[0]
    return pl.pallas_call(
        _mod_kernel,
        grid=(DEPTH, N_MOD),
        in_specs=[
            pl.BlockSpec((rows, D_MODEL), lambda l, j: (0, 0)),
            pl.BlockSpec((None, D_MODEL, D_MODEL), lambda l, j: (l, 0, j)),
            pl.BlockSpec((None, 1, D_MODEL), lambda l, j: (l, 0, j)),
        ],
        out_specs=pl.BlockSpec((None, rows, D_MODEL), lambda l, j: (l, 0, j)),
        out_shape=jax.ShapeDtypeStruct((DEPTH, rows, N_MOD * D_MODEL), F32),
        compiler_params=_params(2),
        name="modulations",
    )(cc, mod_w, mod_b.reshape(DEPTH, 1, N_MOD * D_MODEL))


def _ffn_kernel(x_ref, xn_ref, mod_ref, g_ref, wg_ref, wu_ref, wd_ref, *rest, mod0, final, n_cast):
    cast_in, rest = rest[:n_cast], rest[n_cast:]
    final_ref = rest[0] if final else None
    o_ref = rest[1 if final else 0]
    cast_out = rest[(2 if final else 1):-1]
    h_ref = rest[-1]
    shift, scale, gate = (mod_ref[mod0 + k:mod0 + k + 1, :] for k in range(3))
    norm = lambda v: _rms_mod(v, g_ref[...], shift, scale).astype(BF16)
    slot = _prime_norm(x_ref, h_ref, norm)
    x = x_ref[...]
    acc = jnp.zeros(x.shape, F32)
    for f in range(D_FF // FF_CHUNK):
        sl = slice(f * FF_CHUNK, (f + 1) * FF_CHUNK)
        h = h_ref[slot]
        a = jax.nn.silu(_dot(h, wg_ref[:, sl])) * _dot(h, wu_ref[:, sl])
        acc = acc + _dot(a.astype(BF16), wd_ref[sl, :])
        if f < FFN_PREP_PIECES:
            _prep_next(xn_ref, h_ref, slot, f, FFN_PREP_PIECES, norm)
    out = x + 0.5 * gate * acc
    if final:
        out = _rmsnorm(out, final_ref[...])
    o_ref[...] = out
    for src, dst in zip(cast_in, cast_out):
        dst[...] = src[...].astype(BF16)


def _ffn_half(x, mods, norm_g, w_bf16, layer, stream, half, tm, final_g=None, next_w=None):
    t = x.shape[0]
    final = final_g is not None
    n_blocks = t // tm
    wg, wu, wd = w_bf16
    in_specs = [
        _tok_spec(tm),
        _next_tok_spec(tm, n_blocks),
        _sel_spec((layer, stream), (N_MOD, D_MODEL)),
        _sel_spec((layer, 2 * half), (1, D_MODEL)),
        _sel_spec((), (D_MODEL, D_FF)),
        _sel_spec((), (D_MODEL, D_FF)),
        _sel_spec((), (D_FF, D_MODEL)),
    ]
    args = [x, x, mods, norm_g, wg, wu, wd]
    out_specs = [_tok_spec(tm)]
    out_shape = [jax.ShapeDtypeStruct((t, D_MODEL), F32)]
    n_cast = 0
    if next_w is not None:
        stacked, nl, nh = next_w
        n_cast = len(stacked)
        for w in stacked:
            rows, cols = w.shape[2:]
            slab = next(s for s in range(16, rows + 1, 16) if rows % s == 0 and rows // s <= n_blocks)
            slab_of = lambda i, n_slabs=rows // slab: jnp.minimum(i, n_slabs - 1)
            in_specs.append(pl.BlockSpec((None, None, slab, cols),
                                         lambda i, slab_of=slab_of: (nl, nh, slab_of(i), 0)))
            args.append(w)
            out_specs.append(pl.BlockSpec((slab, cols), lambda i, slab_of=slab_of: (slab_of(i), 0)))
            out_shape.append(jax.ShapeDtypeStruct((rows, cols), BF16))
    if final:
        in_specs.append(_sel_spec((), (1, D_MODEL)))
        args.append(final_g.reshape(1, D_MODEL))
    outs = pl.pallas_call(
        functools.partial(_ffn_kernel, mod0=6 * half, final=final, n_cast=n_cast),
        grid=(n_blocks,),
        in_specs=in_specs,
        out_specs=out_specs,
        out_shape=out_shape,
        scratch_shapes=[pltpu.VMEM((2, tm, D_MODEL), BF16)],
        compiler_params=_params(1),
        name="ffn_half",
    )(*args)
    return outs[0], tuple(outs[1:])


def _conv_glu_kernel(x_ref, mod_ref, g_ref, w1_ref, b1_ref, u_ref, w1b_ref):
    _cast_once(w1_ref, w1b_ref)
    h = _rms_mod(x_ref[...], g_ref[...], mod_ref[3:4, :], mod_ref[4:5, :]).astype(BF16)
    a = _dot(h, w1b_ref[:, :D_MODEL]) + b1_ref[:, :D_MODEL]
    gt = _dot(h, w1b_ref[:, D_MODEL:]) + b1_ref[:, D_MODEL:]
    u_ref[...] = a * jax.nn.sigmoid(gt)


def _conv_glu(x, mods, norm_g, w1, b1, layer, stream, j, tm):
    t = x.shape[0]
    return pl.pallas_call(
        _conv_glu_kernel,
        grid=(t // tm,),
        in_specs=[
            _tok_spec(tm),
            _sel_spec((layer, stream), (N_MOD, D_MODEL)),
            _sel_spec((layer, 1), (1, D_MODEL)),
            _sel_spec((j,), (D_MODEL, 2 * D_MODEL)),
            _sel_spec((j,), (1, 2 * D_MODEL)),
        ],
        out_specs=_tok_spec(tm),
        out_shape=jax.ShapeDtypeStruct((t, D_MODEL), F32),
        scratch_shapes=[pltpu.VMEM((D_MODEL, 2 * D_MODEL), BF16)],
        compiler_params=_params(1),
        name="conv_glu",
    )(x, mods, norm_g, w1, b1)


def _conv_tail_kernel(ul_ref, um_ref, ur_ref, x_ref, mod_ref, wdw_ref, bdw_ref, lng_ref, lnb_ref,
                      w2_ref, b2_ref, o_ref, win_ref, cv_ref, w2b_ref, *, tm):
    _cast_once(w2_ref, w2b_ref)
    i = pl.program_id(0)
    last = pl.num_programs(0) - 1
    n_slabs = D_MODEL // V7X_LANES
    for c in range(n_slabs):
        cs = slice(c * V7X_LANES, (c + 1) * V7X_LANES)
        win_ref[c, 0:CONV_HALO, :] = jnp.where(i > 0, ul_ref[:, cs], 0.0)
        win_ref[c, CONV_HALO:CONV_HALO + tm, :] = um_ref[:, cs]
        win_ref[c, CONV_HALO + tm:, :] = jnp.where(i < last, ur_ref[:, cs], 0.0)
    off0 = CONV_HALO - CONV_WIDTH // 2
    group = V7X_SUBLANES * CONV_STRIDE
    for c in range(n_slabs):
        cs = slice(c * V7X_LANES, (c + 1) * V7X_LANES)
        strided = lambda start: pl.ds(start, V7X_SUBLANES, stride=CONV_STRIDE)
        for k0 in range(0, CONV_WIDTH, CONV_TAPS_PER_PASS):
            taps = [wdw_ref[k:k + 1, cs] for k in range(k0, min(k0 + CONV_TAPS_PER_PASS, CONV_WIDTH))]
            for g in range(tm // group):
                v = [win_ref[c, strided(g * group + off0 + k0 + m), :]
                     for m in range(len(taps) + CONV_STRIDE - 1)]
                for j in range(CONV_STRIDE):
                    acc = taps[0] * v[j]
                    for k in range(1, len(taps)):
                        acc = acc + taps[k] * v[j + k]
                    if k0 > 0:
                        acc = acc + cv_ref[c, strided(g * group + j), :]
                    cv_ref[c, strided(g * group + j), :] = acc
    cv = jnp.concatenate([cv_ref[c] for c in range(n_slabs)], axis=1) + bdw_ref[...]
    mu = jnp.mean(cv, axis=-1, keepdims=True)
    cen = cv - mu
    var = jnp.mean(cen * cen, axis=-1, keepdims=True)
    y = cen * lax.rsqrt(var + LN_EPS) * lng_ref[...] + lnb_ref[...]
    out = _dot(jax.nn.silu(y).astype(BF16), w2b_ref[...]) + b2_ref[...]
    o_ref[...] = x_ref[...] + mod_ref[5:6, :] * out


def _conv_tail(u, x, mods, wdw, bdw, lng, lnb, w2, b2, layer, stream, j, tm):
    t = x.shape[0]
    hb = tm // CONV_HALO
    n_halo_blocks = t // CONV_HALO
    vec = _sel_spec((j,), (1, D_MODEL))
    return pl.pallas_call(
        functools.partial(_conv_tail_kernel, tm=tm),
        grid=(t // tm,),
        in_specs=[
            pl.BlockSpec((CONV_HALO, D_MODEL), lambda i: (jnp.maximum(i * hb - 1, 0), 0)),
            _tok_spec(tm),
            pl.BlockSpec((CONV_HALO, D_MODEL),
                         lambda i: (jnp.minimum((i + 1) * hb, n_halo_blocks - 1), 0)),
            _tok_spec(tm),
            _sel_spec((layer, stream), (N_MOD, D_MODEL)),
            _sel_spec((j,), (CONV_WIDTH, D_MODEL)),
            vec, vec, vec,
            _sel_spec((j,), (D_MODEL, D_MODEL)),
            vec,
        ],
        out_specs=_tok_spec(tm),
        out_shape=jax.ShapeDtypeStruct((t, D_MODEL), F32),
        scratch_shapes=[
            pltpu.VMEM((D_MODEL // V7X_LANES, tm + 2 * CONV_HALO, V7X_LANES), F32),
            pltpu.VMEM((D_MODEL // V7X_LANES, tm, V7X_LANES), F32),
            pltpu.VMEM((D_MODEL, D_MODEL), BF16),
        ],
        compiler_params=_params(1),
        name="conv_tail",
    )(u, u, u, x, mods, wdw, bdw, lng, lnb, w2, b2)


def _qkv_kernel(x_ref, mod_ref, g_ref, w_ref, b_ref, q_ref, k_ref, v_ref, wb_ref):
    _cast_once(w_ref, wb_ref)
    h = _rms_mod(x_ref[...], g_ref[...], mod_ref[3:4, :], mod_ref[4:5, :]).astype(BF16)
    d = D_MODEL
    q = _dot(h, wb_ref[:, 0:d]) + b_ref[:, 0:d]
    q_ref[...] = (q * SCORE_SCALE).astype(BF16)
    k_ref[...] = _dot(h, wb_ref[:, d:2 * d]) + b_ref[:, d:2 * d]
    v_ref[...] = _dot(h, wb_ref[:, 2 * d:3 * d]) + b_ref[:, 2 * d:3 * d]


def _qkv_proj(x, mods, norm_g, w, b, layer, stream, j, tm):
    t = x.shape[0]
    return pl.pallas_call(
        _qkv_kernel,
        grid=(t // tm,),
        in_specs=[
            _tok_spec(tm),
            _sel_spec((layer, stream), (N_MOD, D_MODEL)),
            _sel_spec((layer, 1), (1, D_MODEL)),
            _sel_spec((j,), (D_MODEL, 3 * D_MODEL)),
            _sel_spec((j,), (1, 3 * D_MODEL)),
        ],
        out_specs=[_tok_spec(tm)] * 3,
        out_shape=[
            jax.ShapeDtypeStruct((t, D_MODEL), BF16),
            jax.ShapeDtypeStruct((t, D_MODEL), F32),
            jax.ShapeDtypeStruct((t, D_MODEL), F32),
        ],
        scratch_shapes=[pltpu.VMEM((D_MODEL, 3 * D_MODEL), BF16)],
        compiler_params=_params(1),
        name="qkv_proj",
    )(x, mods, norm_g, w, b)


def _head_lane_masks():
    lane = lax.broadcasted_iota(jnp.int32, (1, V7X_MXU), 1)
    return [(lane >= h * HEAD_DIM) & (lane < (h + 1) * HEAD_DIM) for h in range(NA_GROUP)]


def _stack_heads(q, masks):
    return jnp.concatenate([q * m.astype(q.dtype) for m in masks], axis=0)


def _softmax_pv(scores, values):
    mx = functools.reduce(jnp.maximum, [jnp.max(s, axis=-1, keepdims=True) for s in scores])
    ps = [jnp.exp2(s - mx) for s in scores]
    denom = functools.reduce(jnp.add, [jnp.sum(p, axis=-1, keepdims=True) for p in ps])
    o = functools.reduce(jnp.add, [_dot(p.astype(BF16), v) for p, v in zip(ps, values)])
    return o / denom


def _unstack_heads(o, masks, nq):
    out = o[(NA_GROUP - 1) * nq:]
    for h in range(NA_GROUP - 2, -1, -1):
        out = jnp.where(masks[h], o[h * nq:(h + 1) * nq], out)
    return out


def _build_bias_tables(x_ref, rowmask_ref, tab_ref):
    top = (NA_Q_ROWS - 1) * NA_SPAN
    for n in range(GRID_W // WIN_C):
        for hh in range(NA_GROUP):
            compact = x_ref[hh, n]
            for rho in range(NA_Q_ROWS):
                strip = compact[:, top - rho * NA_SPAN:top - rho * NA_SPAN + NA_NK]
                r0 = hh * NA_NQ + rho * WIN_C
                for cls in range(3):
                    tab_ref[cls, n, r0:r0 + WIN_C, :] = jnp.where(
                        rowmask_ref[cls, rho] > 0.5, strip, MASK_VALUE)


def _na_kernel(q_ref, kt_ref, km_ref, kb_ref, vt_ref, vm_ref, vb_ref, kc_ref, vc_ref, x_ref, rowmask_ref,
               o_ref, tab_ref):
    b = pl.program_id(1)
    last = pl.num_programs(1) - 1

    @pl.when(b == 0)
    def _():
        _build_bias_tables(x_ref, rowmask_ref, tab_ref)

    masks = _head_lane_masks()
    step_rows = NA_STEP_BLOCKS * NA_Q_ROWS
    q3 = q_ref[...].reshape(step_rows, GRID_W, V7X_MXU)
    kw = jnp.concatenate([kt_ref[...], km_ref[...], kb_ref[...]], axis=0)
    vw = jnp.concatenate([vt_ref[...], vm_ref[...], vb_ref[...]], axis=0)
    kw3 = kw.reshape(step_rows + 2 * NA_HALO_ROWS, GRID_W, V7X_MXU)
    vw3 = vw.reshape(step_rows + 2 * NA_HALO_ROWS, GRID_W, V7X_MXU)
    kc = kc_ref[...].astype(BF16)
    vc = vc_ref[...].astype(BF16)
    for sb in range(NA_STEP_BLOCKS):
        cls = 1
        if sb == 0:
            cls = jnp.where(b == 0, 0, cls)
        if sb == NA_STEP_BLOCKS - 1:
            cls = jnp.where(b == last, 2, cls)
        r_lo = sb * NA_Q_ROWS
        for n in range(GRID_W // WIN_C):
            c0 = min(max(n * WIN_C - WIN_C // 2, 0), GRID_W - NA_SPAN)
            qn = q3[r_lo:r_lo + NA_Q_ROWS, n * WIN_C:(n + 1) * WIN_C, :].reshape(NA_NQ, V7X_MXU)
            kn = kw3[r_lo:r_lo + NA_KEY_ROWS, c0:c0 + NA_SPAN, :].reshape(NA_NK, V7X_MXU).astype(BF16)
            vn = vw3[r_lo:r_lo + NA_KEY_ROWS, c0:c0 + NA_SPAN, :].reshape(NA_NK, V7X_MXU).astype(BF16)
            qs = _stack_heads(qn, masks)
            s_lat = _dot_nt(qs, kn) + tab_ref[cls, n]
            s_ctx = _dot_nt(qs, kc)
            o = _unstack_heads(_softmax_pv([s_lat, s_ctx], [vn, vc]), masks, NA_NQ)
            o3 = o.astype(BF16).reshape(NA_Q_ROWS, WIN_C, V7X_MXU)
            for r in range(NA_Q_ROWS):
                row0 = (r_lo + r) * GRID_W + n * WIN_C
                o_ref[row0:row0 + WIN_C, :] = o3[r]


def _na_attention(q, k, v, kc, vc, compact, rowmask, j):
    t = q.shape[0]
    tq = NA_STEP_BLOCKS * NA_Q_ROWS * GRID_W
    th = NA_HALO_ROWS * GRID_W
    nb = t // tq
    n_halo_blocks = t // th
    hpb = tq // th
    ncb = GRID_W // WIN_C
    main = lambda: pl.BlockSpec((tq, V7X_MXU), lambda g, b: (b, g))
    top = lambda: pl.BlockSpec((th, V7X_MXU), lambda g, b: (jnp.maximum(b * hpb - 1, 0), g))
    bot = lambda: pl.BlockSpec(
        (th, V7X_MXU), lambda g, b: (jnp.minimum((b + 1) * hpb, n_halo_blocks - 1), g))
    ctx = lambda: pl.BlockSpec((kc.shape[0], V7X_MXU), lambda g, b: (0, g))
    return pl.pallas_call(
        _na_kernel,
        grid=(N_HEADS // NA_GROUP, nb),
        in_specs=[main(), top(), main(), bot(), top(), main(), bot(), ctx(), ctx(),
                  pl.BlockSpec((None, NA_GROUP) + compact.shape[2:], lambda g, b: (j, g, 0, 0, 0)),
                  pl.BlockSpec(rowmask.shape, lambda g, b: (0, 0, 0, 0))],
        out_specs=main(),
        out_shape=jax.ShapeDtypeStruct((t, D_MODEL), BF16),
        scratch_shapes=[pltpu.VMEM((3, ncb, NA_GROUP * NA_NQ, NA_NK), F32)],
        compiler_params=_params(2),
        name="neighbourhood_attention",
    )(q, k, k, k, v, v, v, kc, vc, compact, rowmask)


def _ctx_attn_kernel(q_ref, k_ref, v_ref, o_ref):
    masks = _head_lane_masks()
    nq = q_ref.shape[0]
    qs = _stack_heads(q_ref[...], masks)
    k = k_ref[...].astype(BF16)
    v = v_ref[...].astype(BF16)
    o_ref[...] = _unstack_heads(_softmax_pv([_dot_nt(qs, k)], [v]), masks, nq).astype(BF16)


def _ctx_attention(q, k, v):
    t = q.shape[0]
    blk = pl.BlockSpec((t, V7X_MXU), lambda g: (0, g))
    return pl.pallas_call(
        _ctx_attn_kernel,
        grid=(N_HEADS // NA_GROUP,),
        in_specs=[blk, blk, blk],
        out_specs=blk,
        out_shape=jax.ShapeDtypeStruct((t, D_MODEL), BF16),
        compiler_params=_params(1),
        name="context_attention",
    )(q, k, v)


def _out_proj_kernel(o_ref, x_ref, mod_ref, w_ref, b_ref, y_ref, wb_ref):
    _cast_once(w_ref, wb_ref)
    y = _dot(o_ref[...], wb_ref[...]) + b_ref[...]
    y_ref[...] = x_ref[...] + mod_ref[5:6, :] * y


def _out_proj(o, x, mods, w, b, layer, stream, j, tm):
    t = x.shape[0]
    return pl.pallas_call(
        _out_proj_kernel,
        grid=(t // tm,),
        in_specs=[_tok_spec(tm), _tok_spec(tm),
                  _sel_spec((layer, stream), (N_MOD, D_MODEL)),
                  _sel_spec((j,), (D_MODEL, D_MODEL)),
                  _sel_spec((j,), (1, D_MODEL))],
        out_specs=_tok_spec(tm),
        out_shape=jax.ShapeDtypeStruct((t, D_MODEL), F32),
        scratch_shapes=[pltpu.VMEM((D_MODEL, D_MODEL), BF16)],
        compiler_params=_params(1),
        name="attn_out_proj",
    )(o, x, mods, w, b)


def _na_index_tables(rows):
    ncb = GRID_W // WIN_C
    n = np.arange(ncb)
    c0 = np.clip(n * WIN_C - WIN_C // 2, 0, GRID_W - NA_SPAN)
    q_col = n[:, None] * WIN_C + np.arange(WIN_C)[None, :]
    k_col = c0[:, None] + np.arange(NA_SPAN)[None, :]
    q_start = np.clip(q_col - WIN_C // 2, 0, GRID_W - WIN_C)
    col_valid = ((k_col[:, None, :] >= q_start[:, :, None])
                 & (k_col[:, None, :] < q_start[:, :, None] + WIN_C))
    col_idx = np.clip(k_col[:, None, :] - q_col[:, :, None], -(WIN_C - 1), WIN_C - 1) + WIN_C - 1
    wr = min(WIN_R, rows)
    rho = np.arange(NA_Q_ROWS)
    kappa = np.arange(NA_KEY_ROWS)
    row_valid = []
    for r_start in (0, NA_Q_ROWS, rows - NA_Q_ROWS):
        r = r_start + rho
        j = r_start - NA_HALO_ROWS + kappa
        r0 = np.clip(r - wr // 2, 0, rows - wr)
        row_valid.append((j[None, :] >= r0[:, None]) & (j[None, :] < r0[:, None] + wr))
    row_valid = np.repeat(np.stack(row_valid), NA_SPAN, axis=-1)
    rowmask = row_valid.astype(np.float32).reshape(3, NA_Q_ROWS, 1, NA_NK)
    return col_idx.astype(np.int32), col_valid, rowmask


def _na_compact_bias(rpb, col_idx, col_valid):
    ncb = GRID_W // WIN_C
    n_rows, n_cols = 2 * WIN_R - 1, 2 * WIN_C - 1
    shift = WIN_R - 1 - NA_HALO_ROWS
    front = NA_Q_ROWS - 1 - shift
    total = NA_Q_ROWS + NA_KEY_ROWS
    d = np.arange(total) - front
    flat = (np.clip(d, 0, n_rows - 1)[None, None, :, None] * n_cols + col_idx[:, :, None, :])
    valid = ((d >= 0) & (d < n_rows))[None, None, :, None] & col_valid[:, :, None, :]
    lead = rpb.shape[:2]
    picked = jnp.take(rpb.reshape(lead + (n_rows * n_cols,)), jnp.asarray(flat.reshape(-1), jnp.int32), axis=2)
    picked = picked.reshape(lead + (ncb, WIN_C, total * NA_SPAN)) * LOG2_E
    return jnp.where(jnp.asarray(valid.reshape(ncb, WIN_C, total * NA_SPAN)), picked, MASK_VALUE)


def kernel(x, c, ctx, c_ctx, mod_w, mod_b, norm_g, ffn_w_gate, ffn_w_up, ffn_w_down, conv_w_pw1, conv_b_pw1, conv_w_dw, conv_b_dw, conv_ln_g, conv_ln_b, conv_w_pw2, conv_b_pw2, na_w_qkv, na_b_qkv, na_rpb, na_w_o, na_b_o, final_g):
    b, t, d = x.shape
    assert b == 1 and d == D_MODEL and t % (NA_STEP_BLOCKS * NA_Q_ROWS * GRID_W) == 0
    n_ctx = ctx.shape[1]
    tm = {0: 512, 1: n_ctx}
    tm_mm = {0: 1024, 1: n_ctx}
    col_idx, col_valid, rowmask = _na_index_tables(t // GRID_W)
    rowmask = jnp.asarray(rowmask)
    compact = _na_compact_bias(na_rpb, col_idx, col_valid)
    xs = {0: x.reshape(t, d), 1: ctx.reshape(n_ctx, d)}

    cc = jnp.zeros((V7X_SUBLANES, d), F32).at[0].set(c[0]).at[1].set(c_ctx)
    mods = _modulations(cc, mod_w, mod_b).reshape(DEPTH, V7X_SUBLANES, N_MOD, d)
    ng = norm_g.reshape(DEPTH, 3, 1, d)
    row = lambda v: v.reshape(v.shape[0], 1, v.shape[1])
    conv_vecs = tuple(row(v) for v in (conv_b_dw, conv_ln_g, conv_ln_b))
    ffn_w32 = (ffn_w_gate, ffn_w_up, ffn_w_down)
    w_cur = tuple(w[0, 0].astype(BF16) for w in ffn_w32)

    for i in range(DEPTH):
        mixer, j = i % 2, i // 2
        last = i == DEPTH - 1
        for half in (0, 1):
            if half == 1:
                if mixer == 0:
                    for s in ((0,) if last else (0, 1)):
                        u = _conv_glu(xs[s], mods, ng, conv_w_pw1, row(conv_b_pw1), i, s, j, tm_mm[s])
                        xs[s] = _conv_tail(u, xs[s], mods, conv_w_dw, *conv_vecs, conv_w_pw2,
                                           row(conv_b_pw2), i, s, j, tm[s])
                else:
                    qkv = {s: _qkv_proj(xs[s], mods, ng, na_w_qkv, row(na_b_qkv), i, s, j, tm_mm[s])
                           for s in (0, 1)}
                    o = {0: _na_attention(*qkv[0], qkv[1][1], qkv[1][2], compact, rowmask, j)}
                    if not last:
                        o[1] = _ctx_attention(*qkv[1])
                    for s in o:
                        xs[s] = _out_proj(o[s], xs[s], mods, na_w_o, row(na_b_o), i, s, j, tm_mm[s])
            final = last and half == 1
            nxt = None if final else (ffn_w32, i + half, 1 - half)
            xs[0], w_next = _ffn_half(xs[0], mods, ng, w_cur, i, 0, half, tm[0],
                                      final_g=final_g if final else None, next_w=nxt)
            if not final:
                xs[1], _ = _ffn_half(xs[1], mods, ng, w_cur, i, 1, half, tm[1])
                w_cur = w_next
    return xs[0].reshape(b, t, d)
```

```python
import functools

import numpy as np
import jax
import jax.numpy as jnp
from jax import lax
from jax.experimental import pallas as pl
from jax.experimental.pallas import tpu as pltpu

D_MODEL = 1024
DEPTH = 4
GRID_W = 64
N_HEADS = 16
HEAD_DIM = D_MODEL // N_HEADS
D_FF = 2816
CONV_WIDTH = 31
WIN_R = 8
WIN_C = 16
N_MOD = 9
RMS_EPS = 1e-6
LN_EPS = 1e-5
MASK_VALUE = -1e30
LOG2_E = 1.4426950408889634
SCORE_SCALE = HEAD_DIM ** -0.5 * LOG2_E

F32 = jnp.float32
BF16 = jnp.bfloat16

V7X_LANES = 128
V7X_SUBLANES = 8
V7X_MXU = 256
FF_CHUNK = V7X_MXU
FFN_PREP_PIECES = 8
CONV_HALO = 16
CONV_STRIDE = 4
CONV_TAPS_PER_PASS = 16
NA_Q_ROWS = 8
NA_STEP_BLOCKS = 4
NA_HALO_ROWS = WIN_R // 2
NA_KEY_ROWS = NA_Q_ROWS + 2 * NA_HALO_ROWS
NA_SPAN = 2 * WIN_C
NA_GROUP = V7X_MXU // HEAD_DIM
NA_NQ = NA_Q_ROWS * WIN_C
NA_NK = NA_KEY_ROWS * NA_SPAN
VMEM_LIMIT = 56 * 1024 * 1024


def _sel_spec(prefix, tail):
    prefix, tail = tuple(prefix), tuple(tail)
    return pl.BlockSpec((None,) * len(prefix) + tail, lambda *_: prefix + (0,) * len(tail),
                        pipeline_mode=pl.Buffered(1))


def _tok_spec(tm):
    return pl.BlockSpec((tm, D_MODEL), lambda i: (i, 0))


def _params(n_grid_dims):
    return pltpu.CompilerParams(
        dimension_semantics=("arbitrary",) * n_grid_dims, vmem_limit_bytes=VMEM_LIMIT)


def _dot(a, b):
    return jnp.dot(a, b, preferred_element_type=F32)


def _dot_nt(a, b):
    return lax.dot_general(a, b, (((1,), (1,)), ((), ())), preferred_element_type=F32)


def _rmsnorm(x, g):
    return x * lax.rsqrt(jnp.mean(x * x, axis=-1, keepdims=True) + RMS_EPS) * g


def _rms_mod(x, g, shift, scale):
    return _rmsnorm(x, g) * (1.0 + scale) + shift


def _cast_once(w_ref, wb_ref):
    @pl.when(pl.program_id(0) == 0)
    def _():
        wb_ref[...] = w_ref[...].astype(BF16)


def _next_tok_spec(tm, n_blocks):
    return pl.BlockSpec((tm, D_MODEL), lambda i: (jnp.minimum(i + 1, n_blocks - 1), 0))


def _prime_norm(x_ref, h_ref, norm):
    i = pl.program_id(0)

    @pl.when(i == 0)
    def _():
        h_ref[0] = norm(x_ref[...])

    return i % 2


def _prep_next(xn_ref, h_ref, slot, k, n_pieces, norm):
    piece = xn_ref.shape[0] // n_pieces
    rows = slice(k * piece, (k + 1) * piece)
    h_ref[1 - slot, rows, :] = norm(xn_ref[rows, :])


def _mod_kernel(cc_ref, w_ref, b_ref, o_ref):
    s = jax.nn.silu(cc_ref[...]).astype(BF16)
    o_ref[...] = _dot(s, w_ref[...].astype(BF16)) + b_ref[...]


def _modulations(cc, mod_w, mod_b):
    rows = cc.shape[0]
    return pl.pallas_call(
        _mod_kernel,
        grid=(DEPTH, N_MOD),
        in_specs=[
            pl.BlockSpec((rows, D_MODEL), lambda l, j: (0, 0)),
            pl.BlockSpec((None, D_MODEL, D_MODEL), lambda l, j: (l, 0, j)),
            pl.BlockSpec((None, 1, D_MODEL), lambda l, j: (l, 0, j)),
        ],
        out_specs=pl.BlockSpec((None, rows, D_MODEL), lambda l, j: (l, 0, j)),
        out_shape=jax.ShapeDtypeStruct((DEPTH, rows, N_MOD * D_MODEL), F32),
        compiler_params=_params(2),
        name="modulations",
    )(cc, mod_w, mod_b.reshape(DEPTH, 1, N_MOD * D_MODEL))


def _ffn_kernel(x_ref, xn_ref, mod_ref, g_ref, wg_ref, wu_ref, wd_ref, *rest, mod0, final, n_cast):
    cast_in, rest = rest[:n_cast], rest[n_cast:]
    final_ref = rest[0] if final else None
    o_ref = rest[1 if final else 0]
    cast_out = rest[(2 if final else 1):-1]
    h_ref = rest[-1]
    shift, scale, gate = (mod_ref[mod0 + k:mod0 + k + 1, :] for k in range(3))
    norm = lambda v: _rms_mod(v, g_ref[...], shift, scale).astype(BF16)
    slot = _prime_norm(x_ref, h_ref, norm)
    x = x_ref[...]
    acc = jnp.zeros(x.shape, F32)
    for f in range(D_FF // FF_CHUNK):
        sl = slice(f * FF_CHUNK, (f + 1) * FF_CHUNK)
        h = h_ref[slot]
        a = jax.nn.silu(_dot(h, wg_ref[:, sl])) * _dot(h, wu_ref[:, sl])
        acc = acc + _dot(a.astype(BF16), wd_ref[sl, :])
        if f < FFN_PREP_PIECES:
            _prep_next(xn_ref, h_ref, slot, f, FFN_PREP_PIECES, norm)
    out = x + 0.5 * gate * acc
    if final:
        out = _rmsnorm(out, final_ref[...])
    o_ref[...] = out
    for src, dst in zip(cast_in, cast_out):
        dst[...] = src[...].astype(BF16)


def _ffn_half(x, mods, norm_g, w_bf16, layer, stream, half, tm, final_g=None, next_w=None):
    t = x.shape[0]
    final = final_g is not None
    n_blocks = t // tm
    wg, wu, wd = w_bf16
    in_specs = [
        _tok_spec(tm),
        _next_tok_spec(tm, n_blocks),
        _sel_spec((layer, stream), (N_MOD, D_MODEL)),
        _sel_spec((layer, 2 * half), (1, D_MODEL)),
        _sel_spec((), (D_MODEL, D_FF)),
        _sel_spec((), (D_MODEL, D_FF)),
        _sel_spec((), (D_FF, D_MODEL)),
    ]
    args = [x, x, mods, norm_g, wg, wu, wd]
    out_specs = [_tok_spec(tm)]
    out_shape = [jax.ShapeDtypeStruct((t, D_MODEL), F32)]
    n_cast = 0
    if next_w is not None:
        stacked, nl, nh = next_w
        n_cast = len(stacked)
        for w in stacked:
            rows, cols = w.shape[2:]
            slab = next(s for s in range(16, rows + 1, 16) if rows % s == 0 and rows // s <= n_blocks)
            slab_of = lambda i, n_slabs=rows // slab: jnp.minimum(i, n_slabs - 1)
            in_specs.append(pl.BlockSpec((None, None, slab, cols),
                                         lambda i, slab_of=slab_of: (nl, nh, slab_of(i), 0)))
            args.append(w)
            out_specs.append(pl.BlockSpec((slab, cols), lambda i, slab_of=slab_of: (slab_of(i), 0)))
            out_shape.append(jax.ShapeDtypeStruct((rows, cols), BF16))
    if final:
        in_specs.append(_sel_spec((), (1, D_MODEL)))
        args.append(final_g.reshape(1, D_MODEL))
    outs = pl.pallas_call(
        functools.partial(_ffn_kernel, mod0=6 * half, final=final, n_cast=n_cast),
        grid=(n_blocks,),
        in_specs=in_specs,
        out_specs=out_specs,
        out_shape=out_shape,
        scratch_shapes=[pltpu.VMEM((2, tm, D_MODEL), BF16)],
        compiler_params=_params(1),
        name="ffn_half",
    )(*args)
    return outs[0], tuple(outs[1:])


def _conv_glu_kernel(x_ref, mod_ref, g_ref, w1_ref, b1_ref, u_ref, w1b_ref):
    _cast_once(w1_ref, w1b_ref)
    h = _rms_mod(x_ref[...], g_ref[...], mod_ref[3:4, :], mod_ref[4:5, :]).astype(BF16)
    a = _dot(h, w1b_ref[:, :D_MODEL]) + b1_ref[:, :D_MODEL]
    gt = _dot(h, w1b_ref[:, D_MODEL:]) + b1_ref[:, D_MODEL:]
    u_ref[...] = a * jax.nn.sigmoid(gt)


def _conv_glu(x, mods, norm_g, w1, b1, layer, stream, j, tm):
    t = x.shape[0]
    return pl.pallas_call(
        _conv_glu_kernel,
        grid=(t // tm,),
        in_specs=[
            _tok_spec(tm),
            _sel_spec((layer, stream), (N_MOD, D_MODEL)),
            _sel_spec((layer, 1), (1, D_MODEL)),
            _sel_spec((j,), (D_MODEL, 2 * D_MODEL)),
            _sel_spec((j,), (1, 2 * D_MODEL)),
        ],
        out_specs=_tok_spec(tm),
        out_shape=jax.ShapeDtypeStruct((t, D_MODEL), F32),
        scratch_shapes=[pltpu.VMEM((D_MODEL, 2 * D_MODEL), BF16)],
        compiler_params=_params(1),
        name="conv_glu",
    )(x, mods, norm_g, w1, b1)


def _conv_tail_kernel(ul_ref, um_ref, ur_ref, x_ref, mod_ref, wdw_ref, bdw_ref, lng_ref, lnb_ref,
                      w2_ref, b2_ref, o_ref, win_ref, cv_ref, w2b_ref, *, tm):
    _cast_once(w2_ref, w2b_ref)
    i = pl.program_id(0)
    last = pl.num_programs(0) - 1
    n_slabs = D_MODEL // V7X_LANES
    for c in range(n_slabs):
        cs = slice(c * V7X_LANES, (c + 1) * V7X_LANES)
        win_ref[c, 0:CONV_HALO, :] = jnp.where(i > 0, ul_ref[:, cs], 0.0)
        win_ref[c, CONV_HALO:CONV_HALO + tm, :] = um_ref[:, cs]
        win_ref[c, CONV_HALO + tm:, :] = jnp.where(i < last, ur_ref[:, cs], 0.0)
    off0 = CONV_HALO - CONV_WIDTH // 2
    group = V7X_SUBLANES * CONV_STRIDE
    for c in range(n_slabs):
        cs = slice(c * V7X_LANES, (c + 1) * V7X_LANES)
        strided = lambda start: pl.ds(start, V7X_SUBLANES, stride=CONV_STRIDE)
        for k0 in range(0, CONV_WIDTH, CONV_TAPS_PER_PASS):
            taps = [wdw_ref[k:k + 1, cs] for k in range(k0, min(k0 + CONV_TAPS_PER_PASS, CONV_WIDTH))]
            for g in range(tm // group):
                v = [win_ref[c, strided(g * group + off0 + k0 + m), :]
                     for m in range(len(taps) + CONV_STRIDE - 1)]
                for j in range(CONV_STRIDE):
                    acc = taps[0] * v[j]
                    for k in range(1, len(taps)):
                        acc = acc + taps[k] * v[j + k]
                    if k0 > 0:
                        acc = acc + cv_ref[c, strided(g * group + j), :]
                    cv_ref[c, strided(g * group + j), :] = acc
    cv = jnp.concatenate([cv_ref[c] for c in range(n_slabs)], axis=1) + bdw_ref[...]
    mu = jnp.mean(cv, axis=-1, keepdims=True)
    cen = cv - mu
    var = jnp.mean(cen * cen, axis=-1, keepdims=True)
    y = cen * lax.rsqrt(var + LN_EPS) * lng_ref[...] + lnb_ref[...]
    out = _dot(jax.nn.silu(y).astype(BF16), w2b_ref[...]) + b2_ref[...]
    o_ref[...] = x_ref[...] + mod_ref[5:6, :] * out


def _conv_tail(u, x, mods, wdw, bdw, lng, lnb, w2, b2, layer, stream, j, tm):
    t = x.shape[0]
    hb = tm // CONV_HALO
    n_halo_blocks = t // CONV_HALO
    vec = _sel_spec((j,), (1, D_MODEL))
    return pl.pallas_call(
        functools.partial(_conv_tail_kernel, tm=tm),
        grid=(t // tm,),
        in_specs=[
            pl.BlockSpec((CONV_HALO, D_MODEL), lambda i: (jnp.maximum(i * hb - 1, 0), 0)),
            _tok_spec(tm),
            pl.BlockSpec((CONV_HALO, D_MODEL),
                         lambda i: (jnp.minimum((i + 1) * hb, n_halo_blocks - 1), 0)),
            _tok_spec(tm),
            _sel_spec((layer, stream), (N_MOD, D_MODEL)),
            _sel_spec((j,), (CONV_WIDTH, D_MODEL)),
            vec, vec, vec,
            _sel_spec((j,), (D_MODEL, D_MODEL)),
            vec,
        ],
        out_specs=_tok_spec(tm),
        out_shape=jax.ShapeDtypeStruct((t, D_MODEL), F32),
        scratch_shapes=[
            pltpu.VMEM((D_MODEL // V7X_LANES, tm + 2 * CONV_HALO, V7X_LANES), F32),
            pltpu.VMEM((D_MODEL // V7X_LANES, tm, V7X_LANES), F32),
            pltpu.VMEM((D_MODEL, D_MODEL), BF16),
        ],
        compiler_params=_params(1),
        name="conv_tail",
    )(u, u, u, x, mods, wdw, bdw, lng, lnb, w2, b2)


def _qkv_kernel(x_ref, mod_ref, g_ref, w_ref, b_ref, q_ref, k_ref, v_ref, wb_ref):
    _cast_once(w_ref, wb_ref)
    h = _rms_mod(x_ref[...], g_ref[...], mod_ref[3:4, :], mod_ref[4:5, :]).astype(BF16)
    d = D_MODEL
    q = _dot(h, wb_ref[:, 0:d]) + b_ref[:, 0:d]
    q_ref[...] = (q * SCORE_SCALE).astype(BF16)
    k_ref[...] = _dot(h, wb_ref[:, d:2 * d]) + b_ref[:, d:2 * d]
    v_ref[...] = _dot(h, wb_ref[:, 2 * d:3 * d]) + b_ref[:, 2 * d:3 * d]


def _qkv_proj(x, mods, norm_g, w, b, layer, stream, j, tm):
    t = x.shape[0]
    return pl.pallas_call(
        _qkv_kernel,
        grid=(t // tm,),
        in_specs=[
            _tok_spec(tm),
            _sel_spec((layer, stream), (N_MOD, D_MODEL)),
            _sel_spec((layer, 1), (1, D_MODEL)),
            _sel_spec((j,), (D_MODEL, 3 * D_MODEL)),
            _sel_spec((j,), (1, 3 * D_MODEL)),
        ],
        out_specs=[_tok_spec(tm)] * 3,
        out_shape=[
            jax.ShapeDtypeStruct((t, D_MODEL), BF16),
            jax.ShapeDtypeStruct((t, D_MODEL), F32),
            jax.ShapeDtypeStruct((t, D_MODEL), F32),
        ],
        scratch_shapes=[pltpu.VMEM((D_MODEL, 3 * D_MODEL), BF16)],
        compiler_params=_params(1),
        name="qkv_proj",
    )(x, mods, norm_g, w, b)


def _head_lane_masks():
    lane = lax.broadcasted_iota(jnp.int32, (1, V7X_MXU), 1)
    return [(lane >= h * HEAD_DIM) & (lane < (h + 1) * HEAD_DIM) for h in range(NA_GROUP)]


def _stack_heads(q, masks):
    return jnp.concatenate([q * m.astype(q.dtype) for m in masks], axis=0)


def _softmax_pv(scores, values):
    mx = functools.reduce(jnp.maximum, [jnp.max(s, axis=-1, keepdims=True) for s in scores])
    ps = [jnp.exp2(s - mx) for s in scores]
    denom = functools.reduce(jnp.add, [jnp.sum(p, axis=-1, keepdims=True) for p in ps])
    o = functools.reduce(jnp.add, [_dot(p.astype(BF16), v) for p, v in zip(ps, values)])
    return o / denom


def _unstack_heads(o, masks, nq):
    out = o[(NA_GROUP - 1) * nq:]
    for h in range(NA_GROUP - 2, -1, -1):
        out = jnp.where(masks[h], o[h * nq:(h + 1) * nq], out)
    return out


def _build_bias_tables(x_ref, rowmask_ref, tab_ref):
    top = (NA_Q_ROWS - 1) * NA_SPAN
    for n in range(GRID_W // WIN_C):
        for hh in range(NA_GROUP):
            compact = x_ref[hh, n]
            for rho in range(NA_Q_ROWS):
                strip = compact[:, top - rho * NA_SPAN:top - rho * NA_SPAN + NA_NK]
                r0 = hh * NA_NQ + rho * WIN_C
                for cls in range(3):
                    tab_ref[cls, n, r0:r0 + WIN_C, :] = jnp.where(
                        rowmask_ref[cls, rho] > 0.5, strip, MASK_VALUE)


def _na_kernel(q_ref, kt_ref, km_ref, kb_ref, vt_ref, vm_ref, vb_ref, kc_ref, vc_ref, x_ref, rowmask_ref,
               o_ref, tab_ref):
    b = pl.program_id(1)
    last = pl.num_programs(1) - 1

    @pl.when(b == 0)
    def _():
        _build_bias_tables(x_ref, rowmask_ref, tab_ref)

    masks = _head_lane_masks()
    step_rows = NA_STEP_BLOCKS * NA_Q_ROWS
    q3 = q_ref[...].reshape(step_rows, GRID_W, V7X_MXU)
    kw = jnp.concatenate([kt_ref[...], km_ref[...], kb_ref[...]], axis=0)
    vw = jnp.concatenate([vt_ref[...], vm_ref[...], vb_ref[...]], axis=0)
    kw3 = kw.reshape(step_rows + 2 * NA_HALO_ROWS, GRID_W, V7X_MXU)
    vw3 = vw.reshape(step_rows + 2 * NA_HALO_ROWS, GRID_W, V7X_MXU)
    kc = kc_ref[...].astype(BF16)
    vc = vc_ref[...].astype(BF16)
    for sb in range(NA_STEP_BLOCKS):
        cls = 1
        if sb == 0:
            cls = jnp.where(b == 0, 0, cls)
        if sb == NA_STEP_BLOCKS - 1:
            cls = jnp.where(b == last, 2, cls)
        r_lo = sb * NA_Q_ROWS
        for n in range(GRID_W // WIN_C):
            c0 = min(max(n * WIN_C - WIN_C // 2, 0), GRID_W - NA_SPAN)
            qn = q3[r_lo:r_lo + NA_Q_ROWS, n * WIN_C:(n + 1) * WIN_C, :].reshape(NA_NQ, V7X_MXU)
            kn = kw3[r_lo:r_lo + NA_KEY_ROWS, c0:c0 + NA_SPAN, :].reshape(NA_NK, V7X_MXU).astype(BF16)
            vn = vw3[r_lo:r_lo + NA_KEY_ROWS, c0:c0 + NA_SPAN, :].reshape(NA_NK, V7X_MXU).astype(BF16)
            qs = _stack_heads(qn, masks)
            s_lat = _dot_nt(qs, kn) + tab_ref[cls, n]
            s_ctx = _dot_nt(qs, kc)
            o = _unstack_heads(_softmax_pv([s_lat, s_ctx], [vn, vc]), masks, NA_NQ)
            o3 = o.astype(BF16).reshape(NA_Q_ROWS, WIN_C, V7X_MXU)
            for r in range(NA_Q_ROWS):
                row0 = (r_lo + r) * GRID_W + n * WIN_C
                o_ref[row0:row0 + WIN_C, :] = o3[r]


def _na_attention(q, k, v, kc, vc, compact, rowmask, j):
    t = q.shape[0]
    tq = NA_STEP_BLOCKS * NA_Q_ROWS * GRID_W
    th = NA_HALO_ROWS * GRID_W
    nb = t // tq
    n_halo_blocks = t // th
    hpb = tq // th
    ncb = GRID_W // WIN_C
    main = lambda: pl.BlockSpec((tq, V7X_MXU), lambda g, b: (b, g))
    top = lambda: pl.BlockSpec((th, V7X_MXU), lambda g, b: (jnp.maximum(b * hpb - 1, 0), g))
    bot = lambda: pl.BlockSpec(
        (th, V7X_MXU), lambda g, b: (jnp.minimum((b + 1) * hpb, n_halo_blocks - 1), g))
    ctx = lambda: pl.BlockSpec((kc.shape[0], V7X_MXU), lambda g, b: (0, g))
    return pl.pallas_call(
        _na_kernel,
        grid=(N_HEADS // NA_GROUP, nb),
        in_specs=[main(), top(), main(), bot(), top(), main(), bot(), ctx(), ctx(),
                  pl.BlockSpec((None, NA_GROUP) + compact.shape[2:], lambda g, b: (j, g, 0, 0, 0)),
                  pl.BlockSpec(rowmask.shape, lambda g, b: (0, 0, 0, 0))],
        out_specs=main(),
        out_shape=jax.ShapeDtypeStruct((t, D_MODEL), BF16),
        scratch_shapes=[pltpu.VMEM((3, ncb, NA_GROUP * NA_NQ, NA_NK), F32)],
        compiler_params=_params(2),
        name="neighbourhood_attention",
    )(q, k, k, k, v, v, v, kc, vc, compact, rowmask)


def _ctx_attn_kernel(q_ref, k_ref, v_ref, o_ref):
    masks = _head_lane_masks()
    nq = q_ref.shape[0]
    qs = _stack_heads(q_ref[...], masks)
    k = k_ref[...].astype(BF16)
    v = v_ref[...].astype(BF16)
    o_ref[...] = _unstack_heads(_softmax_pv([_dot_nt(qs, k)], [v]), masks, nq).astype(BF16)


def _ctx_attention(q, k, v):
    t = q.shape[0]
    blk = pl.BlockSpec((t, V7X_MXU), lambda g: (0, g))
    return pl.pallas_call(
        _ctx_attn_kernel,
        grid=(N_HEADS // NA_GROUP,),
        in_specs=[blk, blk, blk],
        out_specs=blk,
        out_shape=jax.ShapeDtypeStruct((t, D_MODEL), BF16),
        compiler_params=_params(1),
        name="context_attention",
    )(q, k, v)


def _out_proj_kernel(o_ref, x_ref, mod_ref, w_ref, b_ref, y_ref, wb_ref):
    _cast_once(w_ref, wb_ref)
    y = _dot(o_ref[...], wb_ref[...]) + b_ref[...]
    y_ref[...] = x_ref[...] + mod_ref[5:6, :] * y


def _out_proj(o, x, mods, w, b, layer, stream, j, tm):
    t = x.shape[0]
    return pl.pallas_call(
        _out_proj_kernel,
        grid=(t // tm,),
        in_specs=[_tok_spec(tm), _tok_spec(tm),
                  _sel_spec((layer, stream), (N_MOD, D_MODEL)),
                  _sel_spec((j,), (D_MODEL, D_MODEL)),
                  _sel_spec((j,), (1, D_MODEL))],
        out_specs=_tok_spec(tm),
        out_shape=jax.ShapeDtypeStruct((t, D_MODEL), F32),
        scratch_shapes=[pltpu.VMEM((D_MODEL, D_MODEL), BF16)],
        compiler_params=_params(1),
        name="attn_out_proj",
    )(o, x, mods, w, b)


def _na_index_tables(rows):
    ncb = GRID_W // WIN_C
    n = np.arange(ncb)
    c0 = np.clip(n * WIN_C - WIN_C // 2, 0, GRID_W - NA_SPAN)
    q_col = n[:, None] * WIN_C + np.arange(WIN_C)[None, :]
    k_col = c0[:, None] + np.arange(NA_SPAN)[None, :]
    q_start = np.clip(q_col - WIN_C // 2, 0, GRID_W - WIN_C)
    col_valid = ((k_col[:, None, :] >= q_start[:, :, None])
                 & (k_col[:, None, :] < q_start[:, :, None] + WIN_C))
    col_idx = np.clip(k_col[:, None, :] - q_col[:, :, None], -(WIN_C - 1), WIN_C - 1) + WIN_C - 1
    wr = min(WIN_R, rows)
    rho = np.arange(NA_Q_ROWS)
    kappa = np.arange(NA_KEY_ROWS)
    row_valid = []
    for r_start in (0, NA_Q_ROWS, rows - NA_Q_ROWS):
        r = r_start + rho
        j = r_start - NA_HALO_ROWS + kappa
        r0 = np.clip(r - wr // 2, 0, rows - wr)
        row_valid.append((j[None, :] >= r0[:, None]) & (j[None, :] < r0[:, None] + wr))
    row_valid = np.repeat(np.stack(row_valid), NA_SPAN, axis=-1)
    rowmask = row_valid.astype(np.float32).reshape(3, NA_Q_ROWS, 1, NA_NK)
    return col_idx.astype(np.int32), col_valid, rowmask


def _na_compact_bias(rpb, col_idx, col_valid):
    ncb = GRID_W // WIN_C
    n_rows = 2 * WIN_R - 1
    lead = rpb.shape[:2]
    by_col = jnp.take(rpb * LOG2_E, jnp.asarray(col_idx.reshape(-1)), axis=3)
    by_col = by_col.reshape(lead + (n_rows, ncb, WIN_C, NA_SPAN)).transpose(0, 1, 3, 4, 2, 5)
    by_col = jnp.where(jnp.asarray(col_valid)[:, :, None, :], by_col, MASK_VALUE)
    shift = WIN_R - 1 - NA_HALO_ROWS
    front = NA_Q_ROWS - 1 - shift
    total = NA_Q_ROWS + NA_KEY_ROWS
    back = total - front - n_rows
    padded = jnp.pad(by_col, ((0, 0),) * 4 + ((front, back), (0, 0)))
    return padded.reshape(lead + (ncb, WIN_C, total * NA_SPAN))


def kernel(x, c, ctx, c_ctx, mod_w, mod_b, norm_g, ffn_w_gate, ffn_w_up, ffn_w_down, conv_w_pw1, conv_b_pw1, conv_w_dw, conv_b_dw, conv_ln_g, conv_ln_b, conv_w_pw2, conv_b_pw2, na_w_qkv, na_b_qkv, na_rpb, na_w_o, na_b_o, final_g):
    b, t, d = x.shape
    assert b == 1 and d == D_MODEL and t % (NA_STEP_BLOCKS * NA_Q_ROWS * GRID_W) == 0
    n_ctx = ctx.shape[1]
    tm = {0: 512, 1: n_ctx}
    tm_mm = {0: 1024, 1: n_ctx}
    col_idx, col_valid, rowmask = _na_index_tables(t // GRID_W)
    rowmask = jnp.asarray(rowmask)
    compact = _na_compact_bias(na_rpb, col_idx, col_valid)
    xs = {0: x.reshape(t, d), 1: ctx.reshape(n_ctx, d)}

    cc = jnp.zeros((V7X_SUBLANES, d), F32).at[0].set(c[0]).at[1].set(c_ctx)
    mods = _modulations(cc, mod_w, mod_b).reshape(DEPTH, V7X_SUBLANES, N_MOD, d)
    ng = norm_g.reshape(DEPTH, 3, 1, d)
    row = lambda v: v.reshape(v.shape[0], 1, v.shape[1])
    conv_vecs = tuple(row(v) for v in (conv_b_dw, conv_ln_g, conv_ln_b))
    ffn_w32 = (ffn_w_gate, ffn_w_up, ffn_w_down)
    w_cur = tuple(w[0, 0].astype(BF16) for w in ffn_w32)

    for i in range(DEPTH):
        mixer, j = i % 2, i // 2
        last = i == DEPTH - 1
        for half in (0, 1):
            if half == 1:
                if mixer == 0:
                    for s in ((0,) if last else (0, 1)):
                        u = _conv_glu(xs[s], mods, ng, conv_w_pw1, row(conv_b_pw1), i, s, j, tm_mm[s])
                        xs[s] = _conv_tail(u, xs[s], mods, conv_w_dw, *conv_vecs, conv_w_pw2,
                                           row(conv_b_pw2), i, s, j, tm[s])
                else:
                    qkv = {s: _qkv_proj(xs[s], mods, ng, na_w_qkv, row(na_b_qkv), i, s, j, tm_mm[s])
                           for s in (0, 1)}
                    o = {0: _na_attention(*qkv[0], qkv[1][1], qkv[1][2], compact, rowmask, j)}
                    if not last:
                        o[1] = _ctx_attention(*qkv[1])
                    for s in o:
                        xs[s] = _out_proj(o[s], xs[s], mods, na_w_o, row(na_b_o), i, s, j, tm_mm[s])
            final = last and half == 1
            nxt = None if final else (ffn_w32, i + half, 1 - half)
            xs[0], w_next = _ffn_half(xs[0], mods, ng, w_cur, i, 0, half, tm[0],
                                      final_g=final_g if final else None, next_w=nxt)
            if not final:
                xs[1], _ = _ffn_half(xs[1], mods, ng, w_cur, i, 1, half, tm[1])
                w_cur = w_next
    return xs[0].reshape(b, t, d)
```

```python
import functools

import numpy as np
import jax
import jax.numpy as jnp
from jax import lax
from jax.experimental import pallas as pl
from jax.experimental.pallas import tpu as pltpu

D_MODEL = 1024
DEPTH = 4
GRID_W = 64
N_HEADS = 16
HEAD_DIM = D_MODEL // N_HEADS
D_FF = 2816
CONV_WIDTH = 31
WIN_R = 8
WIN_C = 16
N_MOD = 9
RMS_EPS = 1e-6
LN_EPS = 1e-5
MASK_VALUE = -1e30
LOG2_E = 1.4426950408889634
SCORE_SCALE = HEAD_DIM ** -0.5 * LOG2_E

F32 = jnp.float32
BF16 = jnp.bfloat16

V7X_LANES = 128
V7X_SUBLANES = 8
V7X_MXU = 256
FF_CHUNK = V7X_MXU
FFN_PREP_PIECES = 8
CONV_HALO = 16
CONV_STRIDE = 4
CONV_TAPS_PER_PASS = 16
NA_Q_ROWS = 8
NA_STEP_BLOCKS = 4
NA_HALO_ROWS = WIN_R // 2
NA_KEY_ROWS = NA_Q_ROWS + 2 * NA_HALO_ROWS
NA_SPAN = 2 * WIN_C
NA_GROUP = V7X_MXU // HEAD_DIM
NA_NQ = NA_Q_ROWS * WIN_C
NA_NK = NA_KEY_ROWS * NA_SPAN
VMEM_LIMIT = 56 * 1024 * 1024


def _sel_spec(prefix, tail):
    prefix, tail = tuple(prefix), tuple(tail)
    return pl.BlockSpec((None,) * len(prefix) + tail, lambda *_: prefix + (0,) * len(tail),
                        pipeline_mode=pl.Buffered(1))


def _tok_spec(tm):
    return pl.BlockSpec((tm, D_MODEL), lambda i: (i, 0))


def _params(n_grid_dims):
    return pltpu.CompilerParams(
        dimension_semantics=("arbitrary",) * n_grid_dims, vmem_limit_bytes=VMEM_LIMIT)


def _dot(a, b):
    return jnp.dot(a, b, preferred_element_type=F32)


def _dot_nt(a, b):
    return lax.dot_general(a, b, (((1,), (1,)), ((), ())), preferred_element_type=F32)


def _rmsnorm(x, g):
    return x * lax.rsqrt(jnp.mean(x * x, axis=-1, keepdims=True) + RMS_EPS) * g


def _rms_mod(x, g, shift, scale):
    return _rmsnorm(x, g) * (1.0 + scale) + shift


def _cast_once(w_ref, wb_ref):
    @pl.when(pl.program_id(0) == 0)
    def _():
        wb_ref[...] = w_ref[...].astype(BF16)


def _next_tok_spec(tm, n_blocks):
    return pl.BlockSpec((tm, D_MODEL), lambda i: (jnp.minimum(i + 1, n_blocks - 1), 0))


def _prime_norm(x_ref, h_ref, norm):
    i = pl.program_id(0)

    @pl.when(i == 0)
    def _():
        h_ref[0] = norm(x_ref[...])

    return i % 2


def _prep_next(xn_ref, h_ref, slot, k, n_pieces, norm):
    piece = xn_ref.shape[0] // n_pieces
    rows = slice(k * piece, (k + 1) * piece)
    h_ref[1 - slot, rows, :] = norm(xn_ref[rows, :])


def _mod_kernel(cc_ref, w_ref, b_ref, o_ref):
    s = jax.nn.silu(cc_ref[...]).astype(BF16)
    o_ref[...] = _dot(s, w_ref[...].astype(BF16)) + b_ref[...]


def _modulations(cc, mod_w, mod_b):
    rows = cc.shape[0]
    return pl.pallas_call(
        _mod_kernel,
        grid=(DEPTH, N_MOD),
        in_specs=[
            pl.BlockSpec((rows, D_MODEL), lambda l, j: (0, 0)),
            pl.BlockSpec((None, D_MODEL, D_MODEL), lambda l, j: (l, 0, j)),
            pl.BlockSpec((None, 1, D_MODEL), lambda l, j: (l, 0, j)),
        ],
        out_specs=pl.BlockSpec((None, rows, D_MODEL), lambda l, j: (l, 0, j)),
        out_shape=jax.ShapeDtypeStruct((DEPTH, rows, N_MOD * D_MODEL), F32),
        compiler_params=_params(2),
        name="modulations",
    )(cc, mod_w, mod_b.reshape(DEPTH, 1, N_MOD * D_MODEL))


def _ffn_kernel(x_ref, xn_ref, mod_ref, g_ref, wg_ref, wu_ref, wd_ref, *rest, mod0, final, n_cast):
    cast_in, rest = rest[:n_cast], rest[n_cast:]
    final_ref = rest[0] if final else None
    o_ref = rest[1 if final else 0]
    cast_out = rest[(2 if final else 1):-1]
    h_ref = rest[-1]
    shift, scale, gate = (mod_ref[mod0 + k:mod0 + k + 1, :] for k in range(3))
    norm = lambda v: _rms_mod(v, g_ref[...], shift, scale).astype(BF16)
    slot = _prime_norm(x_ref, h_ref, norm)
    x = x_ref[...]
    acc = jnp.zeros(x.shape, F32)
    for f in range(D_FF // FF_CHUNK):
        sl = slice(f * FF_CHUNK, (f + 1) * FF_CHUNK)
        h = h_ref[slot]
        a = jax.nn.silu(_dot(h, wg_ref[:, sl])) * _dot(h, wu_ref[:, sl])
        acc = acc + _dot(a.astype(BF16), wd_ref[sl, :])
        if f < FFN_PREP_PIECES:
            _prep_next(xn_ref, h_ref, slot, f, FFN_PREP_PIECES, norm)
    out = x + 0.5 * gate * acc
    if final:
        out = _rmsnorm(out, final_ref[...])
    o_ref[...] = out
    for src, dst in zip(cast_in, cast_out):
        dst[...] = src[...].astype(BF16)


def _ffn_half(x, mods, norm_g, w_bf16, layer, stream, half, tm, final_g=None, next_w=None):
    t = x.shape[0]
    final = final_g is not None
    n_blocks = t // tm
    wg, wu, wd = w_bf16
    in_specs = [
        _tok_spec(tm),
        _next_tok_spec(tm, n_blocks),
        _sel_spec((layer, stream), (N_MOD, D_MODEL)),
        _sel_spec((layer, 2 * half), (1, D_MODEL)),
        _sel_spec((), (D_MODEL, D_FF)),
        _sel_spec((), (D_MODEL, D_FF)),
        _sel_spec((), (D_FF, D_MODEL)),
    ]
    args = [x, x, mods, norm_g, wg, wu, wd]
    out_specs = [_tok_spec(tm)]
    out_shape = [jax.ShapeDtypeStruct((t, D_MODEL), F32)]
    n_cast = 0
    if next_w is not None:
        stacked, nl, nh = next_w
        n_cast = len(stacked)
        for w in stacked:
            rows, cols = w.shape[2:]
            slab = next(s for s in range(16, rows + 1, 16) if rows % s == 0 and rows // s <= n_blocks)
            slab_of = lambda i, n_slabs=rows // slab: jnp.minimum(i, n_slabs - 1)
            in_specs.append(pl.BlockSpec((None, None, slab, cols),
                                         lambda i, slab_of=slab_of: (nl, nh, slab_of(i), 0)))
            args.append(w)
            out_specs.append(pl.BlockSpec((slab, cols), lambda i, slab_of=slab_of: (slab_of(i), 0)))
            out_shape.append(jax.ShapeDtypeStruct((rows, cols), BF16))
    if final:
        in_specs.append(_sel_spec((), (1, D_MODEL)))
        args.append(final_g.reshape(1, D_MODEL))
    outs = pl.pallas_call(
        functools.partial(_ffn_kernel, mod0=6 * half, final=final, n_cast=n_cast),
        grid=(n_blocks,),
        in_specs=in_specs,
        out_specs=out_specs,
        out_shape=out_shape,
        scratch_shapes=[pltpu.VMEM((2, tm, D_MODEL), BF16)],
        compiler_params=_params(1),
        name="ffn_half",
    )(*args)
    return outs[0], tuple(outs[1:])


def _conv_glu_kernel(x_ref, mod_ref, g_ref, w1_ref, b1_ref, u_ref, w1b_ref):
    _cast_once(w1_ref, w1b_ref)
    h = _rms_mod(x_ref[...], g_ref[...], mod_ref[3:4, :], mod_ref[4:5, :]).astype(BF16)
    a = _dot(h, w1b_ref[:, :D_MODEL]) + b1_ref[:, :D_MODEL]
    gt = _dot(h, w1b_ref[:, D_MODEL:]) + b1_ref[:, D_MODEL:]
    u_ref[...] = a * jax.nn.sigmoid(gt)


def _conv_glu(x, mods, norm_g, w1, b1, layer, stream, j, tm):
    t = x.shape[0]
    return pl.pallas_call(
        _conv_glu_kernel,
        grid=(t // tm,),
        in_specs=[
            _tok_spec(tm),
            _sel_spec((layer, stream), (N_MOD, D_MODEL)),
            _sel_spec((layer, 1), (1, D_MODEL)),
            _sel_spec((j,), (D_MODEL, 2 * D_MODEL)),
            _sel_spec((j,), (1, 2 * D_MODEL)),
        ],
        out_specs=_tok_spec(tm),
        out_shape=jax.ShapeDtypeStruct((t, D_MODEL), F32),
        scratch_shapes=[pltpu.VMEM((D_MODEL, 2 * D_MODEL), BF16)],
        compiler_params=_params(1),
        name="conv_glu",
    )(x, mods, norm_g, w1, b1)


def _conv_tail_kernel(ul_ref, um_ref, ur_ref, x_ref, mod_ref, wdw_ref, bdw_ref, lng_ref, lnb_ref,
                      w2_ref, b2_ref, o_ref, win_ref, cv_ref, w2b_ref, *, tm):
    _cast_once(w2_ref, w2b_ref)
    i = pl.program_id(0)
    last = pl.num_programs(0) - 1
    n_slabs = D_MODEL // V7X_LANES
    for c in range(n_slabs):
        cs = slice(c * V7X_LANES, (c + 1) * V7X_LANES)
        win_ref[c, 0:CONV_HALO, :] = jnp.where(i > 0, ul_ref[:, cs], 0.0)
        win_ref[c, CONV_HALO:CONV_HALO + tm, :] = um_ref[:, cs]
        win_ref[c, CONV_HALO + tm:, :] = jnp.where(i < last, ur_ref[:, cs], 0.0)
    off0 = CONV_HALO - CONV_WIDTH // 2
    group = V7X_SUBLANES * CONV_STRIDE
    for c in range(n_slabs):
        cs = slice(c * V7X_LANES, (c + 1) * V7X_LANES)
        strided = lambda start: pl.ds(start, V7X_SUBLANES, stride=CONV_STRIDE)
        for k0 in range(0, CONV_WIDTH, CONV_TAPS_PER_PASS):
            taps = [wdw_ref[k:k + 1, cs] for k in range(k0, min(k0 + CONV_TAPS_PER_PASS, CONV_WIDTH))]
            for g in range(tm // group):
                v = [win_ref[c, strided(g * group + off0 + k0 + m), :]
                     for m in range(len(taps) + CONV_STRIDE - 1)]
                for j in range(CONV_STRIDE):
                    acc = taps[0] * v[j]
                    for k in range(1, len(taps)):
                        acc = acc + taps[k] * v[j + k]
                    if k0 > 0:
                        acc = acc + cv_ref[c, strided(g * group + j), :]
                    cv_ref[c, strided(g * group + j), :] = acc
    cv = jnp.concatenate([cv_ref[c] for c in range(n_slabs)], axis=1) + bdw_ref[...]
    mu = jnp.mean(cv, axis=-1, keepdims=True)
    cen = cv - mu
    var = jnp.mean(cen * cen, axis=-1, keepdims=True)
    y = cen * lax.rsqrt(var + LN_EPS) * lng_ref[...] + lnb_ref[...]
    out = _dot(jax.nn.silu(y).astype(BF16), w2b_ref[...]) + b2_ref[...]
    o_ref[...] = x_ref[...] + mod_ref[5:6, :] * out


def _conv_tail(u, x, mods, wdw, bdw, lng, lnb, w2, b2, layer, stream, j, tm):
    t = x.shape[0]
    hb = tm // CONV_HALO
    n_halo_blocks = t // CONV_HALO
    vec = _sel_spec((j,), (1, D_MODEL))
    return pl.pallas_call(
        functools.partial(_conv_tail_kernel, tm=tm),
        grid=(t // tm,),
        in_specs=[
            pl.BlockSpec((CONV_HALO, D_MODEL), lambda i: (jnp.maximum(i * hb - 1, 0), 0)),
            _tok_spec(tm),
            pl.BlockSpec((CONV_HALO, D_MODEL),
                         lambda i: (jnp.minimum((i + 1) * hb, n_halo_blocks - 1), 0)),
            _tok_spec(tm),
            _sel_spec((layer, stream), (N_MOD, D_MODEL)),
            _sel_spec((j,), (CONV_WIDTH, D_MODEL)),
            vec, vec, vec,
            _sel_spec((j,), (D_MODEL, D_MODEL)),
            vec,
        ],
        out_specs=_tok_spec(tm),
        out_shape=jax.ShapeDtypeStruct((t, D_MODEL), F32),
        scratch_shapes=[
            pltpu.VMEM((D_MODEL // V7X_LANES, tm + 2 * CONV_HALO, V7X_LANES), F32),
            pltpu.VMEM((D_MODEL // V7X_LANES, tm, V7X_LANES), F32),
            pltpu.VMEM((D_MODEL, D_MODEL), BF16),
        ],
        compiler_params=_params(1),
        name="conv_tail",
    )(u, u, u, x, mods, wdw, bdw, lng, lnb, w2, b2)


def _qkv_kernel(x_ref, mod_ref, g_ref, w_ref, b_ref, q_ref, k_ref, v_ref, wb_ref):
    _cast_once(w_ref, wb_ref)
    h = _rms_mod(x_ref[...], g_ref[...], mod_ref[3:4, :], mod_ref[4:5, :]).astype(BF16)
    d = D_MODEL
    q = _dot(h, wb_ref[:, 0:d]) + b_ref[:, 0:d]
    q_ref[...] = (q * SCORE_SCALE).astype(BF16)
    k_ref[...] = _dot(h, wb_ref[:, d:2 * d]) + b_ref[:, d:2 * d]
    v_ref[...] = _dot(h, wb_ref[:, 2 * d:3 * d]) + b_ref[:, 2 * d:3 * d]


def _qkv_proj(x, mods, norm_g, w, b, layer, stream, j, tm):
    t = x.shape[0]
    return pl.pallas_call(
        _qkv_kernel,
        grid=(t // tm,),
        in_specs=[
            _tok_spec(tm),
            _sel_spec((layer, stream), (N_MOD, D_MODEL)),
            _sel_spec((layer, 1), (1, D_MODEL)),
            _sel_spec((j,), (D_MODEL, 3 * D_MODEL)),
            _sel_spec((j,), (1, 3 * D_MODEL)),
        ],
        out_specs=[_tok_spec(tm)] * 3,
        out_shape=[
            jax.ShapeDtypeStruct((t, D_MODEL), BF16),
            jax.ShapeDtypeStruct((t, D_MODEL), F32),
            jax.ShapeDtypeStruct((t, D_MODEL), F32),
        ],
        scratch_shapes=[pltpu.VMEM((D_MODEL, 3 * D_MODEL), BF16)],
        compiler_params=_params(1),
        name="qkv_proj",
    )(x, mods, norm_g, w, b)


def _head_lane_masks():
    lane = lax.broadcasted_iota(jnp.int32, (1, V7X_MXU), 1)
    return [(lane >= h * HEAD_DIM) & (lane < (h + 1) * HEAD_DIM) for h in range(NA_GROUP)]


def _stack_heads(q, masks):
    return jnp.concatenate([q * m.astype(q.dtype) for m in masks], axis=0)


def _softmax_pv(scores, values):
    mx = functools.reduce(jnp.maximum, [jnp.max(s, axis=-1, keepdims=True) for s in scores])
    ps = [jnp.exp2(s - mx) for s in scores]
    denom = functools.reduce(jnp.add, [jnp.sum(p, axis=-1, keepdims=True) for p in ps])
    o = functools.reduce(jnp.add, [_dot(p.astype(BF16), v) for p, v in zip(ps, values)])
    return o / denom


def _unstack_heads(o, masks, nq):
    out = o[(NA_GROUP - 1) * nq:]
    for h in range(NA_GROUP - 2, -1, -1):
        out = jnp.where(masks[h], o[h * nq:(h + 1) * nq], out)
    return out


def _build_bias_tables(rpb_ref, colmask_ref, rowmask_ref, tab_ref):
    n_rows = 2 * WIN_R - 1
    front = NA_Q_ROWS - WIN_R + NA_HALO_ROWS
    back = NA_Q_ROWS + NA_KEY_ROWS - front - n_rows
    masked = jnp.full((WIN_C, NA_SPAN), MASK_VALUE, F32)
    top = (NA_Q_ROWS - 1) * NA_SPAN
    for n in range(GRID_W // WIN_C):
        c0 = min(max(n * WIN_C - WIN_C // 2, 0), GRID_W - NA_SPAN)
        lane0 = c0 - n * WIN_C + WIN_C - 1
        keep = colmask_ref[n] > 0.5
        for hh in range(NA_GROUP):
            pieces = []
            for d in range(n_rows):
                row = jnp.broadcast_to(rpb_ref[hh, d:d + 1, :] * LOG2_E, (WIN_C, V7X_LANES))
                rolled = pltpu.roll(row, (V7X_LANES - lane0) % V7X_LANES, 1, stride=1, stride_axis=0)
                pieces.append(jnp.where(keep, rolled[:, :NA_SPAN], MASK_VALUE))
            compact = jnp.concatenate([masked] * front + pieces + [masked] * back, axis=1)
            for rho in range(NA_Q_ROWS):
                strip = compact[:, top - rho * NA_SPAN:top - rho * NA_SPAN + NA_NK]
                r0 = hh * NA_NQ + rho * WIN_C
                for cls in range(3):
                    tab_ref[cls, n, r0:r0 + WIN_C, :] = jnp.where(
                        rowmask_ref[cls, rho] > 0.5, strip, MASK_VALUE)


def _na_kernel(q_ref, kt_ref, km_ref, kb_ref, vt_ref, vm_ref, vb_ref, kc_ref, vc_ref, rpb_ref, colmask_ref,
               rowmask_ref, o_ref, tab_ref):
    b = pl.program_id(1)
    last = pl.num_programs(1) - 1

    @pl.when(b == 0)
    def _():
        _build_bias_tables(rpb_ref, colmask_ref, rowmask_ref, tab_ref)

    masks = _head_lane_masks()
    step_rows = NA_STEP_BLOCKS * NA_Q_ROWS
    q3 = q_ref[...].reshape(step_rows, GRID_W, V7X_MXU)
    kw = jnp.concatenate([kt_ref[...], km_ref[...], kb_ref[...]], axis=0)
    vw = jnp.concatenate([vt_ref[...], vm_ref[...], vb_ref[...]], axis=0)
    kw3 = kw.reshape(step_rows + 2 * NA_HALO_ROWS, GRID_W, V7X_MXU)
    vw3 = vw.reshape(step_rows + 2 * NA_HALO_ROWS, GRID_W, V7X_MXU)
    kc = kc_ref[...].astype(BF16)
    vc = vc_ref[...].astype(BF16)
    for sb in range(NA_STEP_BLOCKS):
        cls = 1
        if sb == 0:
            cls = jnp.where(b == 0, 0, cls)
        if sb == NA_STEP_BLOCKS - 1:
            cls = jnp.where(b == last, 2, cls)
        r_lo = sb * NA_Q_ROWS
        for n in range(GRID_W // WIN_C):
            c0 = min(max(n * WIN_C - WIN_C // 2, 0), GRID_W - NA_SPAN)
            qn = q3[r_lo:r_lo + NA_Q_ROWS, n * WIN_C:(n + 1) * WIN_C, :].reshape(NA_NQ, V7X_MXU)
            kn = kw3[r_lo:r_lo + NA_KEY_ROWS, c0:c0 + NA_SPAN, :].reshape(NA_NK, V7X_MXU).astype(BF16)
            vn = vw3[r_lo:r_lo + NA_KEY_ROWS, c0:c0 + NA_SPAN, :].reshape(NA_NK, V7X_MXU).astype(BF16)
            qs = _stack_heads(qn, masks)
            s_lat = _dot_nt(qs, kn) + tab_ref[cls, n]
            s_ctx = _dot_nt(qs, kc)
            o = _unstack_heads(_softmax_pv([s_lat, s_ctx], [vn, vc]), masks, NA_NQ)
            o3 = o.astype(BF16).reshape(NA_Q_ROWS, WIN_C, V7X_MXU)
            for r in range(NA_Q_ROWS):
                row0 = (r_lo + r) * GRID_W + n * WIN_C
                o_ref[row0:row0 + WIN_C, :] = o3[r]


def _na_attention(q, k, v, kc, vc, rpb, colmask, rowmask, j):
    t = q.shape[0]
    tq = NA_STEP_BLOCKS * NA_Q_ROWS * GRID_W
    th = NA_HALO_ROWS * GRID_W
    nb = t // tq
    n_halo_blocks = t // th
    hpb = tq // th
    ncb = GRID_W // WIN_C
    main = lambda: pl.BlockSpec((tq, V7X_MXU), lambda g, b: (b, g))
    top = lambda: pl.BlockSpec((th, V7X_MXU), lambda g, b: (jnp.maximum(b * hpb - 1, 0), g))
    bot = lambda: pl.BlockSpec(
        (th, V7X_MXU), lambda g, b: (jnp.minimum((b + 1) * hpb, n_halo_blocks - 1), g))
    ctx = lambda: pl.BlockSpec((kc.shape[0], V7X_MXU), lambda g, b: (0, g))
    return pl.pallas_call(
        _na_kernel,
        grid=(N_HEADS // NA_GROUP, nb),
        in_specs=[main(), top(), main(), bot(), top(), main(), bot(), ctx(), ctx(),
                  pl.BlockSpec((None, NA_GROUP) + rpb.shape[2:], lambda g, b: (j, g, 0, 0)),
                  pl.BlockSpec(colmask.shape, lambda g, b: (0, 0, 0)),
                  pl.BlockSpec(rowmask.shape, lambda g, b: (0, 0, 0, 0))],
        out_specs=main(),
        out_shape=jax.ShapeDtypeStruct((t, D_MODEL), BF16),
        scratch_shapes=[pltpu.VMEM((3, ncb, NA_GROUP * NA_NQ, NA_NK), F32)],
        compiler_params=_params(2),
        name="neighbourhood_attention",
    )(q, k, k, k, v, v, v, kc, vc, rpb, colmask, rowmask)


def _ctx_attn_kernel(q_ref, k_ref, v_ref, o_ref):
    masks = _head_lane_masks()
    nq = q_ref.shape[0]
    qs = _stack_heads(q_ref[...], masks)
    k = k_ref[...].astype(BF16)
    v = v_ref[...].astype(BF16)
    o_ref[...] = _unstack_heads(_softmax_pv([_dot_nt(qs, k)], [v]), masks, nq).astype(BF16)


def _ctx_attention(q, k, v):
    t = q.shape[0]
    blk = pl.BlockSpec((t, V7X_MXU), lambda g: (0, g))
    return pl.pallas_call(
        _ctx_attn_kernel,
        grid=(N_HEADS // NA_GROUP,),
        in_specs=[blk, blk, blk],
        out_specs=blk,
        out_shape=jax.ShapeDtypeStruct((t, D_MODEL), BF16),
        compiler_params=_params(1),
        name="context_attention",
    )(q, k, v)


def _out_proj_kernel(o_ref, x_ref, mod_ref, w_ref, b_ref, y_ref, wb_ref):
    _cast_once(w_ref, wb_ref)
    y = _dot(o_ref[...], wb_ref[...]) + b_ref[...]
    y_ref[...] = x_ref[...] + mod_ref[5:6, :] * y


def _out_proj(o, x, mods, w, b, layer, stream, j, tm):
    t = x.shape[0]
    return pl.pallas_call(
        _out_proj_kernel,
        grid=(t // tm,),
        in_specs=[_tok_spec(tm), _tok_spec(tm),
                  _sel_spec((layer, stream), (N_MOD, D_MODEL)),
                  _sel_spec((j,), (D_MODEL, D_MODEL)),
                  _sel_spec((j,), (1, D_MODEL))],
        out_specs=_tok_spec(tm),
        out_shape=jax.ShapeDtypeStruct((t, D_MODEL), F32),
        scratch_shapes=[pltpu.VMEM((D_MODEL, D_MODEL), BF16)],
        compiler_params=_params(1),
        name="attn_out_proj",
    )(o, x, mods, w, b)


def _na_index_tables(rows):
    ncb = GRID_W // WIN_C
    n = np.arange(ncb)
    c0 = np.clip(n * WIN_C - WIN_C // 2, 0, GRID_W - NA_SPAN)
    q_col = n[:, None] * WIN_C + np.arange(WIN_C)[None, :]
    k_col = c0[:, None] + np.arange(NA_SPAN)[None, :]
    q_start = np.clip(q_col - WIN_C // 2, 0, GRID_W - WIN_C)
    col_valid = ((k_col[:, None, :] >= q_start[:, :, None])
                 & (k_col[:, None, :] < q_start[:, :, None] + WIN_C))
    wr = min(WIN_R, rows)
    rho = np.arange(NA_Q_ROWS)
    kappa = np.arange(NA_KEY_ROWS)
    row_valid = []
    for r_start in (0, NA_Q_ROWS, rows - NA_Q_ROWS):
        r = r_start + rho
        j = r_start - NA_HALO_ROWS + kappa
        r0 = np.clip(r - wr // 2, 0, rows - wr)
        row_valid.append((j[None, :] >= r0[:, None]) & (j[None, :] < r0[:, None] + wr))
    row_valid = np.repeat(np.stack(row_valid), NA_SPAN, axis=-1)
    rowmask = row_valid.astype(np.float32).reshape(3, NA_Q_ROWS, 1, NA_NK)
    return col_valid, rowmask


def kernel(x, c, ctx, c_ctx, mod_w, mod_b, norm_g, ffn_w_gate, ffn_w_up, ffn_w_down, conv_w_pw1, conv_b_pw1, conv_w_dw, conv_b_dw, conv_ln_g, conv_ln_b, conv_w_pw2, conv_b_pw2, na_w_qkv, na_b_qkv, na_rpb, na_w_o, na_b_o, final_g):
    b, t, d = x.shape
    assert b == 1 and d == D_MODEL and t % (NA_STEP_BLOCKS * NA_Q_ROWS * GRID_W) == 0
    n_ctx = ctx.shape[1]
    tm = {0: 512, 1: n_ctx}
    tm_mm = {0: 1024, 1: n_ctx}
    col_valid, rowmask = _na_index_tables(t // GRID_W)
    rowmask = jnp.asarray(rowmask)
    colmask = jnp.asarray(col_valid, F32)
    rpb = jnp.pad(na_rpb, ((0, 0),) * 3 + ((0, V7X_LANES - na_rpb.shape[-1]),))
    xs = {0: x.reshape(t, d), 1: ctx.reshape(n_ctx, d)}

    cc = jnp.zeros((V7X_SUBLANES, d), F32).at[0].set(c[0]).at[1].set(c_ctx)
    mods = _modulations(cc, mod_w, mod_b).reshape(DEPTH, V7X_SUBLANES, N_MOD, d)
    ng = norm_g.reshape(DEPTH, 3, 1, d)
    row = lambda v: v.reshape(v.shape[0], 1, v.shape[1])
    conv_vecs = tuple(row(v) for v in (conv_b_dw, conv_ln_g, conv_ln_b))
    ffn_w32 = (ffn_w_gate, ffn_w_up, ffn_w_down)
    w_cur = tuple(w[0, 0].astype(BF16) for w in ffn_w32)

    for i in range(DEPTH):
        mixer, j = i % 2, i // 2
        last = i == DEPTH - 1
        for half in (0, 1):
            if half == 1:
                if mixer == 0:
                    for s in ((0,) if last else (0, 1)):
                        u = _conv_glu(xs[s], mods, ng, conv_w_pw1, row(conv_b_pw1), i, s, j, tm_mm[s])
                        xs[s] = _conv_tail(u, xs[s], mods, conv_w_dw, *conv_vecs, conv_w_pw2,
                                           row(conv_b_pw2), i, s, j, tm[s])
                else:
                    qkv = {s: _qkv_proj(xs[s], mods, ng, na_w_qkv, row(na_b_qkv), i, s, j, tm_mm[s])
                           for s in (0, 1)}
                    o = {0: _na_attention(*qkv[0], qkv[1][1], qkv[1][2], rpb, colmask, rowmask, j)}
                    if not last:
                        o[1] = _ctx_attention(*qkv[1])
                    for s in o:
                        xs[s] = _out_proj(o[s], xs[s], mods, na_w_o, row(na_b_o), i, s, j, tm_mm[s])
            final = last and half == 1
            nxt = None if final else (ffn_w32, i + half, 1 - half)
            xs[0], w_next = _ffn_half(xs[0], mods, ng, w_cur, i, 0, half, tm[0],
                                      final_g=final_g if final else None, next_w=nxt)
            if not final:
                xs[1], _ = _ffn_half(xs[1], mods, ng, w_cur, i, 1, half, tm[1])
                w_cur = w_next
    return xs[0].reshape(b, t, d)
```

```python
import functools

import numpy as np
import jax
import jax.numpy as jnp
from jax import lax
from jax.experimental import pallas as pl
from jax.experimental.pallas import tpu as pltpu

D_MODEL = 1024
DEPTH = 4
GRID_W = 64
N_HEADS = 16
HEAD_DIM = D_MODEL // N_HEADS
D_FF = 2816
CONV_WIDTH = 31
WIN_R = 8
WIN_C = 16
N_MOD = 9
RMS_EPS = 1e-6
LN_EPS = 1e-5
MASK_VALUE = -1e30
LOG2_E = 1.4426950408889634
SCORE_SCALE = HEAD_DIM ** -0.5 * LOG2_E

F32 = jnp.float32
BF16 = jnp.bfloat16

V7X_LANES = 128
V7X_SUBLANES = 8
V7X_MXU = 256
FF_CHUNK = V7X_MXU
FFN_PREP_ROWS = 64
MOD_PER_STEP = 3
CONV_HALO = 16
CONV_STRIDE = 4
CONV_TAPS_PER_PASS = 16
NA_Q_ROWS = 8
NA_STEP_BLOCKS = 4
NA_HALO_ROWS = WIN_R // 2
NA_KEY_ROWS = NA_Q_ROWS + 2 * NA_HALO_ROWS
NA_SPAN = 2 * WIN_C
NA_GROUP = V7X_MXU // HEAD_DIM
NA_NQ = NA_Q_ROWS * WIN_C
NA_NK = NA_KEY_ROWS * NA_SPAN
VMEM_LIMIT = 58 * 1024 * 1024


def _sel_spec(prefix, tail):
    prefix, tail = tuple(prefix), tuple(tail)
    return pl.BlockSpec((None,) * len(prefix) + tail, lambda *_: prefix + (0,) * len(tail),
                        pipeline_mode=pl.Buffered(1))


def _tok_spec(tm):
    return pl.BlockSpec((tm, D_MODEL), lambda i: (i, 0))


def _params(n_grid_dims):
    return pltpu.CompilerParams(
        dimension_semantics=("arbitrary",) * n_grid_dims, vmem_limit_bytes=VMEM_LIMIT)


def _dot(a, b):
    return jnp.dot(a, b, preferred_element_type=F32)


def _dot_nt(a, b):
    return lax.dot_general(a, b, (((1,), (1,)), ((), ())), preferred_element_type=F32)


def _rmsnorm(x, g):
    return x * lax.rsqrt(jnp.mean(x * x, axis=-1, keepdims=True) + RMS_EPS) * g


def _rms_mod(x, g, shift, scale):
    return _rmsnorm(x, g) * (1.0 + scale) + shift


def _cast_once(w_ref, wb_ref):
    @pl.when(pl.program_id(0) == 0)
    def _():
        wb_ref[...] = w_ref[...].astype(BF16)


def _next_tok_spec(tm, n_blocks):
    return pl.BlockSpec((tm, D_MODEL), lambda i: (jnp.minimum(i + 1, n_blocks - 1), 0))


def _prime_norm(x_ref, h_ref, norm):
    i = pl.program_id(0)

    @pl.when(i == 0)
    def _():
        h_ref[0] = norm(x_ref[...])

    return i % 2


def _prep_next(xn_ref, h_ref, slot, k, n_pieces, norm):
    piece = xn_ref.shape[0] // n_pieces
    rows = slice(k * piece, (k + 1) * piece)
    h_ref[1 - slot, rows, :] = norm(xn_ref[rows, :])


def _mod_kernel(cc_ref, w_ref, b_ref, o_ref):
    s = jax.nn.silu(cc_ref[...]).astype(BF16)
    o_ref[...] = _dot(s, w_ref[...].astype(BF16)) + b_ref[...]


def _modulations(cc, mod_w, mod_b):
    rows = cc.shape[0]
    return pl.pallas_call(
        _mod_kernel,
        grid=(DEPTH, N_MOD // MOD_PER_STEP),
        in_specs=[
            pl.BlockSpec((rows, D_MODEL), lambda l, j: (0, 0)),
            pl.BlockSpec((None, D_MODEL, MOD_PER_STEP * D_MODEL), lambda l, j: (l, 0, j)),
            pl.BlockSpec((None, 1, MOD_PER_STEP * D_MODEL), lambda l, j: (l, 0, j)),
        ],
        out_specs=pl.BlockSpec((None, rows, MOD_PER_STEP * D_MODEL), lambda l, j: (l, 0, j)),
        out_shape=jax.ShapeDtypeStruct((DEPTH, rows, N_MOD * D_MODEL), F32),
        compiler_params=_params(2),
        name="modulations",
    )(cc, mod_w, mod_b.reshape(DEPTH, 1, N_MOD * D_MODEL))


def _ffn_kernel(x_ref, xn_ref, mod_ref, g_ref, wg_ref, wu_ref, wd_ref, *rest, mod0, final, n_cast):
    cast_in, rest = rest[:n_cast], rest[n_cast:]
    final_ref = rest[0] if final else None
    o_ref = rest[1 if final else 0]
    cast_out = rest[(2 if final else 1):-1]
    h_ref = rest[-1]
    shift, scale, gate = (mod_ref[mod0 + k:mod0 + k + 1, :] for k in range(3))
    norm = lambda v: _rms_mod(v, g_ref[...], shift, scale).astype(BF16)
    slot = _prime_norm(x_ref, h_ref, norm)
    x = x_ref[...]
    acc = jnp.zeros(x.shape, F32)
    n_pieces = x.shape[0] // FFN_PREP_ROWS
    for f in range(D_FF // FF_CHUNK):
        sl = slice(f * FF_CHUNK, (f + 1) * FF_CHUNK)
        h = h_ref[slot]
        a = jax.nn.silu(_dot(h, wg_ref[:, sl])) * _dot(h, wu_ref[:, sl])
        acc = acc + _dot(a.astype(BF16), wd_ref[sl, :])
        for piece in range(f, n_pieces, D_FF // FF_CHUNK):
            _prep_next(xn_ref, h_ref, slot, piece, n_pieces, norm)
    out = x + 0.5 * gate * acc
    if final:
        out = _rmsnorm(out, final_ref[...])
    o_ref[...] = out
    for src, dst in zip(cast_in, cast_out):
        dst[...] = src[...].astype(BF16)


def _ffn_half(x, mods, norm_g, w_bf16, layer, stream, half, tm, final_g=None, next_w=None):
    t = x.shape[0]
    final = final_g is not None
    n_blocks = t // tm
    wg, wu, wd = w_bf16
    in_specs = [
        _tok_spec(tm),
        _next_tok_spec(tm, n_blocks),
        _sel_spec((layer, stream), (N_MOD, D_MODEL)),
        _sel_spec((layer, 2 * half), (1, D_MODEL)),
        _sel_spec((), (D_MODEL, D_FF)),
        _sel_spec((), (D_MODEL, D_FF)),
        _sel_spec((), (D_FF, D_MODEL)),
    ]
    args = [x, x, mods, norm_g, wg, wu, wd]
    out_specs = [_tok_spec(tm)]
    out_shape = [jax.ShapeDtypeStruct((t, D_MODEL), F32)]
    n_cast = 0
    if next_w is not None:
        stacked, nl, nh = next_w
        n_cast = len(stacked)
        for w in stacked:
            rows, cols = w.shape[2:]
            slab = next(s for s in range(16, rows + 1, 16) if rows % s == 0 and rows // s <= n_blocks)
            slab_of = lambda i, n_slabs=rows // slab: jnp.minimum(i, n_slabs - 1)
            in_specs.append(pl.BlockSpec((None, None, slab, cols),
                                         lambda i, slab_of=slab_of: (nl, nh, slab_of(i), 0)))
            args.append(w)
            out_specs.append(pl.BlockSpec((slab, cols), lambda i, slab_of=slab_of: (slab_of(i), 0)))
            out_shape.append(jax.ShapeDtypeStruct((rows, cols), BF16))
    if final:
        in_specs.append(_sel_spec((), (1, D_MODEL)))
        args.append(final_g.reshape(1, D_MODEL))
    outs = pl.pallas_call(
        functools.partial(_ffn_kernel, mod0=6 * half, final=final, n_cast=n_cast),
        grid=(n_blocks,),
        in_specs=in_specs,
        out_specs=out_specs,
        out_shape=out_shape,
        scratch_shapes=[pltpu.VMEM((2, tm, D_MODEL), BF16)],
        compiler_params=_params(1),
        name="ffn_half",
    )(*args)
    return outs[0], tuple(outs[1:])


def _conv_glu_kernel(x_ref, mod_ref, g_ref, w1_ref, b1_ref, u_ref, w1b_ref):
    _cast_once(w1_ref, w1b_ref)
    h = _rms_mod(x_ref[...], g_ref[...], mod_ref[3:4, :], mod_ref[4:5, :]).astype(BF16)
    a = _dot(h, w1b_ref[:, :D_MODEL]) + b1_ref[:, :D_MODEL]
    gt = _dot(h, w1b_ref[:, D_MODEL:]) + b1_ref[:, D_MODEL:]
    u_ref[...] = a * jax.nn.sigmoid(gt)


def _conv_glu(x, mods, norm_g, w1, b1, layer, stream, j, tm):
    t = x.shape[0]
    return pl.pallas_call(
        _conv_glu_kernel,
        grid=(t // tm,),
        in_specs=[
            _tok_spec(tm),
            _sel_spec((layer, stream), (N_MOD, D_MODEL)),
            _sel_spec((layer, 1), (1, D_MODEL)),
            _sel_spec((j,), (D_MODEL, 2 * D_MODEL)),
            _sel_spec((j,), (1, 2 * D_MODEL)),
        ],
        out_specs=_tok_spec(tm),
        out_shape=jax.ShapeDtypeStruct((t, D_MODEL), F32),
        scratch_shapes=[pltpu.VMEM((D_MODEL, 2 * D_MODEL), BF16)],
        compiler_params=_params(1),
        name="conv_glu",
    )(x, mods, norm_g, w1, b1)


def _conv_tail_kernel(ul_ref, um_ref, ur_ref, x_ref, mod_ref, wdw_ref, bdw_ref, lng_ref, lnb_ref,
                      w2_ref, b2_ref, o_ref, win_ref, cv_ref, w2b_ref, *, tm):
    _cast_once(w2_ref, w2b_ref)
    i = pl.program_id(0)
    last = pl.num_programs(0) - 1
    n_slabs = D_MODEL // V7X_LANES
    for c in range(n_slabs):
        cs = slice(c * V7X_LANES, (c + 1) * V7X_LANES)
        win_ref[c, 0:CONV_HALO, :] = jnp.where(i > 0, ul_ref[:, cs], 0.0)
        win_ref[c, CONV_HALO:CONV_HALO + tm, :] = um_ref[:, cs]
        win_ref[c, CONV_HALO + tm:, :] = jnp.where(i < last, ur_ref[:, cs], 0.0)
    off0 = CONV_HALO - CONV_WIDTH // 2
    group = V7X_SUBLANES * CONV_STRIDE
    for c in range(n_slabs):
        cs = slice(c * V7X_LANES, (c + 1) * V7X_LANES)
        strided = lambda start: pl.ds(start, V7X_SUBLANES, stride=CONV_STRIDE)
        for k0 in range(0, CONV_WIDTH, CONV_TAPS_PER_PASS):
            taps = [wdw_ref[k:k + 1, cs] for k in range(k0, min(k0 + CONV_TAPS_PER_PASS, CONV_WIDTH))]
            for g in range(tm // group):
                v = [win_ref[c, strided(g * group + off0 + k0 + m), :]
                     for m in range(len(taps) + CONV_STRIDE - 1)]
                for j in range(CONV_STRIDE):
                    acc = taps[0] * v[j]
                    for k in range(1, len(taps)):
                        acc = acc + taps[k] * v[j + k]
                    if k0 > 0:
                        acc = acc + cv_ref[c, strided(g * group + j), :]
                    cv_ref[c, strided(g * group + j), :] = acc
    cv = jnp.concatenate([cv_ref[c] for c in range(n_slabs)], axis=1) + bdw_ref[...]
    mu = jnp.mean(cv, axis=-1, keepdims=True)
    cen = cv - mu
    var = jnp.mean(cen * cen, axis=-1, keepdims=True)
    y = cen * lax.rsqrt(var + LN_EPS) * lng_ref[...] + lnb_ref[...]
    out = _dot(jax.nn.silu(y).astype(BF16), w2b_ref[...]) + b2_ref[...]
    o_ref[...] = x_ref[...] + mod_ref[5:6, :] * out


def _conv_tail(u, x, mods, wdw, bdw, lng, lnb, w2, b2, layer, stream, j, tm):
    t = x.shape[0]
    hb = tm // CONV_HALO
    n_halo_blocks = t // CONV_HALO
    vec = _sel_spec((j,), (1, D_MODEL))
    return pl.pallas_call(
        functools.partial(_conv_tail_kernel, tm=tm),
        grid=(t // tm,),
        in_specs=[
            pl.BlockSpec((CONV_HALO, D_MODEL), lambda i: (jnp.maximum(i * hb - 1, 0), 0)),
            _tok_spec(tm),
            pl.BlockSpec((CONV_HALO, D_MODEL),
                         lambda i: (jnp.minimum((i + 1) * hb, n_halo_blocks - 1), 0)),
            _tok_spec(tm),
            _sel_spec((layer, stream), (N_MOD, D_MODEL)),
            _sel_spec((j,), (CONV_WIDTH, D_MODEL)),
            vec, vec, vec,
            _sel_spec((j,), (D_MODEL, D_MODEL)),
            vec,
        ],
        out_specs=_tok_spec(tm),
        out_shape=jax.ShapeDtypeStruct((t, D_MODEL), F32),
        scratch_shapes=[
            pltpu.VMEM((D_MODEL // V7X_LANES, tm + 2 * CONV_HALO, V7X_LANES), F32),
            pltpu.VMEM((D_MODEL // V7X_LANES, tm, V7X_LANES), F32),
            pltpu.VMEM((D_MODEL, D_MODEL), BF16),
        ],
        compiler_params=_params(1),
        name="conv_tail",
    )(u, u, u, x, mods, wdw, bdw, lng, lnb, w2, b2)


def _qkv_kernel(x_ref, mod_ref, g_ref, w_ref, b_ref, q_ref, k_ref, v_ref, wb_ref):
    _cast_once(w_ref, wb_ref)
    h = _rms_mod(x_ref[...], g_ref[...], mod_ref[3:4, :], mod_ref[4:5, :]).astype(BF16)
    d = D_MODEL
    q = _dot(h, wb_ref[:, 0:d]) + b_ref[:, 0:d]
    q_ref[...] = (q * SCORE_SCALE).astype(BF16)
    k_ref[...] = _dot(h, wb_ref[:, d:2 * d]) + b_ref[:, d:2 * d]
    v_ref[...] = _dot(h, wb_ref[:, 2 * d:3 * d]) + b_ref[:, 2 * d:3 * d]


def _qkv_proj(x, mods, norm_g, w, b, layer, stream, j, tm):
    t = x.shape[0]
    return pl.pallas_call(
        _qkv_kernel,
        grid=(t // tm,),
        in_specs=[
            _tok_spec(tm),
            _sel_spec((layer, stream), (N_MOD, D_MODEL)),
            _sel_spec((layer, 1), (1, D_MODEL)),
            _sel_spec((j,), (D_MODEL, 3 * D_MODEL)),
            _sel_spec((j,), (1, 3 * D_MODEL)),
        ],
        out_specs=[_tok_spec(tm)] * 3,
        out_shape=[
            jax.ShapeDtypeStruct((t, D_MODEL), BF16),
            jax.ShapeDtypeStruct((t, D_MODEL), F32),
            jax.ShapeDtypeStruct((t, D_MODEL), F32),
        ],
        scratch_shapes=[pltpu.VMEM((D_MODEL, 3 * D_MODEL), BF16)],
        compiler_params=_params(1),
        name="qkv_proj",
    )(x, mods, norm_g, w, b)


def _head_lane_masks():
    lane = lax.broadcasted_iota(jnp.int32, (1, V7X_MXU), 1)
    return [(lane >= h * HEAD_DIM) & (lane < (h + 1) * HEAD_DIM) for h in range(NA_GROUP)]


def _stack_heads(q, masks):
    return jnp.concatenate([q * m.astype(q.dtype) for m in masks], axis=0)


def _softmax_pv(scores, values):
    mx = functools.reduce(jnp.maximum, [jnp.max(s, axis=-1, keepdims=True) for s in scores])
    ps = [jnp.exp2(s - mx) for s in scores]
    denom = functools.reduce(jnp.add, [jnp.sum(p, axis=-1, keepdims=True) for p in ps])
    o = functools.reduce(jnp.add, [_dot(p.astype(BF16), v) for p, v in zip(ps, values)])
    return o / denom


def _unstack_heads(o, masks, nq):
    out = o[(NA_GROUP - 1) * nq:]
    for h in range(NA_GROUP - 2, -1, -1):
        out = jnp.where(masks[h], o[h * nq:(h + 1) * nq], out)
    return out


def _build_bias_tables(rpb_ref, colmask_ref, rowmask_ref, tab_ref):
    n_rows = 2 * WIN_R - 1
    front = NA_Q_ROWS - WIN_R + NA_HALO_ROWS
    back = NA_Q_ROWS + NA_KEY_ROWS - front - n_rows
    masked = jnp.full((WIN_C, NA_SPAN), MASK_VALUE, F32)
    top = (NA_Q_ROWS - 1) * NA_SPAN
    for n in range(GRID_W // WIN_C):
        c0 = min(max(n * WIN_C - WIN_C // 2, 0), GRID_W - NA_SPAN)
        lane0 = c0 - n * WIN_C + WIN_C - 1
        keep = colmask_ref[n] > 0.5
        for hh in range(NA_GROUP):
            pieces = []
            for d in range(n_rows):
                row = jnp.broadcast_to(rpb_ref[hh, d:d + 1, :] * LOG2_E, (WIN_C, V7X_LANES))
                rolled = pltpu.roll(row, (V7X_LANES - lane0) % V7X_LANES, 1, stride=1, stride_axis=0)
                pieces.append(jnp.where(keep, rolled[:, :NA_SPAN], MASK_VALUE))
            compact = jnp.concatenate([masked] * front + pieces + [masked] * back, axis=1)
            for rho in range(NA_Q_ROWS):
                strip = compact[:, top - rho * NA_SPAN:top - rho * NA_SPAN + NA_NK]
                r0 = hh * NA_NQ + rho * WIN_C
                for cls in range(3):
                    tab_ref[cls, n, r0:r0 + WIN_C, :] = jnp.where(
                        rowmask_ref[cls, rho] > 0.5, strip, MASK_VALUE)


def _na_kernel(q_ref, kt_ref, km_ref, kb_ref, vt_ref, vm_ref, vb_ref, kc_ref, vc_ref, rpb_ref, colmask_ref,
               rowmask_ref, o_ref, tab_ref):
    b = pl.program_id(1)
    last = pl.num_programs(1) - 1

    @pl.when(b == 0)
    def _():
        _build_bias_tables(rpb_ref, colmask_ref, rowmask_ref, tab_ref)

    masks = _head_lane_masks()
    step_rows = NA_STEP_BLOCKS * NA_Q_ROWS
    q3 = q_ref[...].reshape(step_rows, GRID_W, V7X_MXU)
    kw = jnp.concatenate([kt_ref[...], km_ref[...], kb_ref[...]], axis=0)
    vw = jnp.concatenate([vt_ref[...], vm_ref[...], vb_ref[...]], axis=0)
    kw3 = kw.reshape(step_rows + 2 * NA_HALO_ROWS, GRID_W, V7X_MXU)
    vw3 = vw.reshape(step_rows + 2 * NA_HALO_ROWS, GRID_W, V7X_MXU)
    kc = kc_ref[...].astype(BF16)
    vc = vc_ref[...].astype(BF16)
    for sb in range(NA_STEP_BLOCKS):
        cls = 1
        if sb == 0:
            cls = jnp.where(b == 0, 0, cls)
        if sb == NA_STEP_BLOCKS - 1:
            cls = jnp.where(b == last, 2, cls)
        r_lo = sb * NA_Q_ROWS
        for n in range(GRID_W // WIN_C):
            c0 = min(max(n * WIN_C - WIN_C // 2, 0), GRID_W - NA_SPAN)
            qn = q3[r_lo:r_lo + NA_Q_ROWS, n * WIN_C:(n + 1) * WIN_C, :].reshape(NA_NQ, V7X_MXU)
            kn = kw3[r_lo:r_lo + NA_KEY_ROWS, c0:c0 + NA_SPAN, :].reshape(NA_NK, V7X_MXU).astype(BF16)
            vn = vw3[r_lo:r_lo + NA_KEY_ROWS, c0:c0 + NA_SPAN, :].reshape(NA_NK, V7X_MXU).astype(BF16)
            qs = _stack_heads(qn, masks)
            s_lat = _dot_nt(qs, kn) + tab_ref[cls, n]
            s_ctx = _dot_nt(qs, kc)
            o = _unstack_heads(_softmax_pv([s_lat, s_ctx], [vn, vc]), masks, NA_NQ)
            o3 = o.astype(BF16).reshape(NA_Q_ROWS, WIN_C, V7X_MXU)
            for r in range(NA_Q_ROWS):
                row0 = (r_lo + r) * GRID_W + n * WIN_C
                o_ref[row0:row0 + WIN_C, :] = o3[r]


def _na_attention(q, k, v, kc, vc, rpb, colmask, rowmask, j):
    t = q.shape[0]
    tq = NA_STEP_BLOCKS * NA_Q_ROWS * GRID_W
    th = NA_HALO_ROWS * GRID_W
    nb = t // tq
    n_halo_blocks = t // th
    hpb = tq // th
    ncb = GRID_W // WIN_C
    main = lambda: pl.BlockSpec((tq, V7X_MXU), lambda g, b: (b, g))
    top = lambda: pl.BlockSpec((th, V7X_MXU), lambda g, b: (jnp.maximum(b * hpb - 1, 0), g))
    bot = lambda: pl.BlockSpec(
        (th, V7X_MXU), lambda g, b: (jnp.minimum((b + 1) * hpb, n_halo_blocks - 1), g))
    ctx = lambda: pl.BlockSpec((kc.shape[0], V7X_MXU), lambda g, b: (0, g))
    return pl.pallas_call(
        _na_kernel,
        grid=(N_HEADS // NA_GROUP, nb),
        in_specs=[main(), top(), main(), bot(), top(), main(), bot(), ctx(), ctx(),
                  pl.BlockSpec((None, NA_GROUP) + rpb.shape[2:], lambda g, b: (j, g, 0, 0)),
                  pl.BlockSpec(colmask.shape, lambda g, b: (0, 0, 0)),
                  pl.BlockSpec(rowmask.shape, lambda g, b: (0, 0, 0, 0))],
        out_specs=main(),
        out_shape=jax.ShapeDtypeStruct((t, D_MODEL), BF16),
        scratch_shapes=[pltpu.VMEM((3, ncb, NA_GROUP * NA_NQ, NA_NK), F32)],
        compiler_params=_params(2),
        name="neighbourhood_attention",
    )(q, k, k, k, v, v, v, kc, vc, rpb, colmask, rowmask)


def _ctx_attn_kernel(q_ref, k_ref, v_ref, o_ref):
    masks = _head_lane_masks()
    nq = q_ref.shape[0]
    qs = _stack_heads(q_ref[...], masks)
    k = k_ref[...].astype(BF16)
    v = v_ref[...].astype(BF16)
    o_ref[...] = _unstack_heads(_softmax_pv([_dot_nt(qs, k)], [v]), masks, nq).astype(BF16)


def _ctx_attention(q, k, v):
    t = q.shape[0]
    blk = pl.BlockSpec((t, V7X_MXU), lambda g: (0, g))
    return pl.pallas_call(
        _ctx_attn_kernel,
        grid=(N_HEADS // NA_GROUP,),
        in_specs=[blk, blk, blk],
        out_specs=blk,
        out_shape=jax.ShapeDtypeStruct((t, D_MODEL), BF16),
        compiler_params=_params(1),
        name="context_attention",
    )(q, k, v)


def _out_proj_kernel(o_ref, x_ref, mod_ref, w_ref, b_ref, y_ref, wb_ref):
    _cast_once(w_ref, wb_ref)
    y = _dot(o_ref[...], wb_ref[...]) + b_ref[...]
    y_ref[...] = x_ref[...] + mod_ref[5:6, :] * y


def _out_proj(o, x, mods, w, b, layer, stream, j, tm):
    t = x.shape[0]
    return pl.pallas_call(
        _out_proj_kernel,
        grid=(t // tm,),
        in_specs=[_tok_spec(tm), _tok_spec(tm),
                  _sel_spec((layer, stream), (N_MOD, D_MODEL)),
                  _sel_spec((j,), (D_MODEL, D_MODEL)),
                  _sel_spec((j,), (1, D_MODEL))],
        out_specs=_tok_spec(tm),
        out_shape=jax.ShapeDtypeStruct((t, D_MODEL), F32),
        scratch_shapes=[pltpu.VMEM((D_MODEL, D_MODEL), BF16)],
        compiler_params=_params(1),
        name="attn_out_proj",
    )(o, x, mods, w, b)


def _na_index_tables(rows):
    ncb = GRID_W // WIN_C
    n = np.arange(ncb)
    c0 = np.clip(n * WIN_C - WIN_C // 2, 0, GRID_W - NA_SPAN)
    q_col = n[:, None] * WIN_C + np.arange(WIN_C)[None, :]
    k_col = c0[:, None] + np.arange(NA_SPAN)[None, :]
    q_start = np.clip(q_col - WIN_C // 2, 0, GRID_W - WIN_C)
    col_valid = ((k_col[:, None, :] >= q_start[:, :, None])
                 & (k_col[:, None, :] < q_start[:, :, None] + WIN_C))
    wr = min(WIN_R, rows)
    rho = np.arange(NA_Q_ROWS)
    kappa = np.arange(NA_KEY_ROWS)
    row_valid = []
    for r_start in (0, NA_Q_ROWS, rows - NA_Q_ROWS):
        r = r_start + rho
        j = r_start - NA_HALO_ROWS + kappa
        r0 = np.clip(r - wr // 2, 0, rows - wr)
        row_valid.append((j[None, :] >= r0[:, None]) & (j[None, :] < r0[:, None] + wr))
    row_valid = np.repeat(np.stack(row_valid), NA_SPAN, axis=-1)
    rowmask = row_valid.astype(np.float32).reshape(3, NA_Q_ROWS, 1, NA_NK)
    return col_valid, rowmask


def kernel(x, c, ctx, c_ctx, mod_w, mod_b, norm_g, ffn_w_gate, ffn_w_up, ffn_w_down, conv_w_pw1, conv_b_pw1, conv_w_dw, conv_b_dw, conv_ln_g, conv_ln_b, conv_w_pw2, conv_b_pw2, na_w_qkv, na_b_qkv, na_rpb, na_w_o, na_b_o, final_g):
    b, t, d = x.shape
    assert b == 1 and d == D_MODEL and t % (NA_STEP_BLOCKS * NA_Q_ROWS * GRID_W) == 0
    n_ctx = ctx.shape[1]
    tm = {0: 512, 1: n_ctx}
    tm_mm = {0: 1024, 1: n_ctx}
    col_valid, rowmask = _na_index_tables(t // GRID_W)
    rowmask = jnp.asarray(rowmask)
    colmask = jnp.asarray(col_valid, F32)
    rpb = jnp.pad(na_rpb, ((0, 0),) * 3 + ((0, V7X_LANES - na_rpb.shape[-1]),))
    xs = {0: x.reshape(t, d), 1: ctx.reshape(n_ctx, d)}

    cc = jnp.zeros((V7X_SUBLANES, d), F32).at[0].set(c[0]).at[1].set(c_ctx)
    mods = _modulations(cc, mod_w, mod_b).reshape(DEPTH, V7X_SUBLANES, N_MOD, d)
    ng = norm_g.reshape(DEPTH, 3, 1, d)
    row = lambda v: v.reshape(v.shape[0], 1, v.shape[1])
    conv_vecs = tuple(row(v) for v in (conv_b_dw, conv_ln_g, conv_ln_b))
    ffn_w32 = (ffn_w_gate, ffn_w_up, ffn_w_down)
    w_cur = tuple(w[0, 0].astype(BF16) for w in ffn_w32)

    for i in range(DEPTH):
        mixer, j = i % 2, i // 2
        last = i == DEPTH - 1
        for half in (0, 1):
            if half == 1:
                if mixer == 0:
                    for s in ((0,) if last else (0, 1)):
                        u = _conv_glu(xs[s], mods, ng, conv_w_pw1, row(conv_b_pw1), i, s, j, tm_mm[s])
                        xs[s] = _conv_tail(u, xs[s], mods, conv_w_dw, *conv_vecs, conv_w_pw2,
                                           row(conv_b_pw2), i, s, j, tm[s])
                else:
                    qkv = {s: _qkv_proj(xs[s], mods, ng, na_w_qkv, row(na_b_qkv), i, s, j, tm_mm[s])
                           for s in (0, 1)}
                    o = {0: _na_attention(*qkv[0], qkv[1][1], qkv[1][2], rpb, colmask, rowmask, j)}
                    if not last:
                        o[1] = _ctx_attention(*qkv[1])
                    for s in o:
                        xs[s] = _out_proj(o[s], xs[s], mods, na_w_o, row(na_b_o), i, s, j, tm_mm[s])
            final = last and half == 1
            nxt = None if final else (ffn_w32, i + half, 1 - half)
            xs[0], w_next = _ffn_half(xs[0], mods, ng, w_cur, i, 0, half, tm_mm[0],
                                      final_g=final_g if final else None, next_w=nxt)
            if not final:
                xs[1], _ = _ffn_half(xs[1], mods, ng, w_cur, i, 1, half, tm_mm[1])
                w_cur = w_next
    return xs[0].reshape(b, t, d)
```

```python
import functools

import numpy as np
import jax
import jax.numpy as jnp
from jax import lax
from jax.experimental import pallas as pl
from jax.experimental.pallas import tpu as pltpu

D_MODEL = 1024
DEPTH = 4
GRID_W = 64
N_HEADS = 16
HEAD_DIM = D_MODEL // N_HEADS
D_FF = 2816
CONV_WIDTH = 31
WIN_R = 8
WIN_C = 16
N_MOD = 9
RMS_EPS = 1e-6
LN_EPS = 1e-5
MASK_VALUE = -1e30
LOG2_E = 1.4426950408889634
SCORE_SCALE = HEAD_DIM ** -0.5 * LOG2_E

F32 = jnp.float32
BF16 = jnp.bfloat16

V7X_LANES = 128
V7X_SUBLANES = 8
V7X_MXU = 256
FF_CHUNK = V7X_MXU
FFN_PREP_ROWS = 64
MOD_PER_STEP = 3
CONV_HALO = 16
CONV_STRIDE = 4
CONV_TAPS_PER_PASS = 16
NA_Q_ROWS = 8
NA_STEP_BLOCKS = 4
NA_HALO_ROWS = WIN_R // 2
NA_KEY_ROWS = NA_Q_ROWS + 2 * NA_HALO_ROWS
NA_SPAN = 2 * WIN_C
NA_GROUP = V7X_MXU // HEAD_DIM
NA_NQ = NA_Q_ROWS * WIN_C
NA_NK = NA_KEY_ROWS * NA_SPAN
VMEM_LIMIT = 58 * 1024 * 1024


def _sel_spec(prefix, tail):
    prefix, tail = tuple(prefix), tuple(tail)
    return pl.BlockSpec((None,) * len(prefix) + tail, lambda *_: prefix + (0,) * len(tail),
                        pipeline_mode=pl.Buffered(1))


def _tok_spec(tm):
    return pl.BlockSpec((tm, D_MODEL), lambda i: (i, 0))


def _params(n_grid_dims):
    return pltpu.CompilerParams(
        dimension_semantics=("arbitrary",) * n_grid_dims, vmem_limit_bytes=VMEM_LIMIT)


def _dot(a, b):
    return jnp.dot(a, b, preferred_element_type=F32)


def _dot_nt(a, b):
    return lax.dot_general(a, b, (((1,), (1,)), ((), ())), preferred_element_type=F32)


def _rmsnorm(x, g):
    return x * lax.rsqrt(jnp.mean(x * x, axis=-1, keepdims=True) + RMS_EPS) * g


def _rms_mod(x, g, shift, scale):
    return _rmsnorm(x, g) * (1.0 + scale) + shift


def _cast_once(w_ref, wb_ref):
    @pl.when(pl.program_id(0) == 0)
    def _():
        wb_ref[...] = w_ref[...].astype(BF16)


def _next_tok_spec(tm, n_blocks):
    return pl.BlockSpec((tm, D_MODEL), lambda i: (jnp.minimum(i + 1, n_blocks - 1), 0))


def _prime_norm(x_ref, h_ref, norm):
    i = pl.program_id(0)

    @pl.when(i == 0)
    def _():
        h_ref[0] = norm(x_ref[...])

    return i % 2


def _prep_next(xn_ref, h_ref, slot, k, n_pieces, norm):
    piece = xn_ref.shape[0] // n_pieces
    rows = slice(k * piece, (k + 1) * piece)
    h_ref[1 - slot, rows, :] = norm(xn_ref[rows, :])


def _mod_kernel(cc_ref, w_ref, b_ref, o_ref):
    s = jax.nn.silu(cc_ref[...]).astype(BF16)
    o_ref[...] = _dot(s, w_ref[...].astype(BF16)) + b_ref[...]


def _modulations(cc, mod_w, mod_b):
    rows = cc.shape[0]
    return pl.pallas_call(
        _mod_kernel,
        grid=(DEPTH, N_MOD // MOD_PER_STEP),
        in_specs=[
            pl.BlockSpec((rows, D_MODEL), lambda l, j: (0, 0)),
            pl.BlockSpec((None, D_MODEL, MOD_PER_STEP * D_MODEL), lambda l, j: (l, 0, j)),
            pl.BlockSpec((None, 1, MOD_PER_STEP * D_MODEL), lambda l, j: (l, 0, j)),
        ],
        out_specs=pl.BlockSpec((None, rows, MOD_PER_STEP * D_MODEL), lambda l, j: (l, 0, j)),
        out_shape=jax.ShapeDtypeStruct((DEPTH, rows, N_MOD * D_MODEL), F32),
        compiler_params=_params(2),
        name="modulations",
    )(cc, mod_w, mod_b.reshape(DEPTH, 1, N_MOD * D_MODEL))


def _swiglu(h, wg_ref, wu_ref, wd_ref, after_chunk=None):
    acc = None
    for f in range(D_FF // FF_CHUNK):
        sl = slice(f * FF_CHUNK, (f + 1) * FF_CHUNK)
        hf = h()
        a = jax.nn.silu(_dot(hf, wg_ref[:, sl])) * _dot(hf, wu_ref[:, sl])
        part = _dot(a.astype(BF16), wd_ref[sl, :])
        acc = part if acc is None else acc + part
        if after_chunk is not None:
            after_chunk(f)
    return acc


def _ffn_kernel(x_ref, xn_ref, mod_ref, g_ref, wg_ref, wu_ref, wd_ref, *rest, mod0, final, n_cast, ctx):
    if ctx:
        (xc_ref, modc_ref), rest = rest[:2], rest[2:]
    cast_in, rest = rest[:n_cast], rest[n_cast:]
    if final:
        final_ref, rest = rest[0], rest[1:]
    o_ref, rest = rest[0], rest[1:]
    if ctx:
        oc_ref, rest = rest[0], rest[1:]
    cast_out, h_ref = rest[:-1], rest[-1]
    mod_rows = lambda ref: (ref[mod0 + k:mod0 + k + 1, :] for k in range(3))
    shift, scale, gate = mod_rows(mod_ref)
    norm = lambda v: _rms_mod(v, g_ref[...], shift, scale).astype(BF16)
    slot = _prime_norm(x_ref, h_ref, norm)
    n_pieces = x_ref.shape[0] // FFN_PREP_ROWS

    def after_chunk(f):
        for piece in range(f, n_pieces, D_FF // FF_CHUNK):
            _prep_next(xn_ref, h_ref, slot, piece, n_pieces, norm)

    acc = _swiglu(lambda: h_ref[slot], wg_ref, wu_ref, wd_ref, after_chunk)
    out = x_ref[...] + 0.5 * gate * acc
    if final:
        out = _rmsnorm(out, final_ref[...])
    o_ref[...] = out
    for src, dst in zip(cast_in, cast_out):
        dst[...] = src[...].astype(BF16)

    if ctx:
        @pl.when(pl.program_id(0) == pl.num_programs(0) - 1)
        def _():
            shift_c, scale_c, gate_c = mod_rows(modc_ref)
            xc = xc_ref[...]
            hc = _rms_mod(xc, g_ref[...], shift_c, scale_c).astype(BF16)
            oc_ref[...] = xc + 0.5 * gate_c * _swiglu(lambda: hc, wg_ref, wu_ref, wd_ref)


def _ffn_half(x, mods, norm_g, w_bf16, layer, half, tm, xc=None, final_g=None, next_w=None):
    t = x.shape[0]
    final = final_g is not None
    ctx = xc is not None
    n_blocks = t // tm
    wg, wu, wd = w_bf16
    in_specs = [
        _tok_spec(tm),
        _next_tok_spec(tm, n_blocks),
        _sel_spec((layer, 0), (N_MOD, D_MODEL)),
        _sel_spec((layer, 2 * half), (1, D_MODEL)),
        _sel_spec((), (D_MODEL, D_FF)),
        _sel_spec((), (D_MODEL, D_FF)),
        _sel_spec((), (D_FF, D_MODEL)),
    ]
    args = [x, x, mods, norm_g, wg, wu, wd]
    out_specs = [_tok_spec(tm)]
    out_shape = [jax.ShapeDtypeStruct((t, D_MODEL), F32)]
    if ctx:
        in_specs += [_sel_spec((), xc.shape), _sel_spec((layer, 1), (N_MOD, D_MODEL))]
        args += [xc, mods]
        out_specs.append(pl.BlockSpec(xc.shape, lambda i: (0, 0)))
        out_shape.append(jax.ShapeDtypeStruct(xc.shape, F32))
    n_cast = 0
    if next_w is not None:
        stacked, nl, nh = next_w
        n_cast = len(stacked)
        for w in stacked:
            rows, cols = w.shape[2:]
            slab = next(s for s in range(16, rows + 1, 16) if rows % s == 0 and rows // s <= n_blocks)
            slab_of = lambda i, n_slabs=rows // slab: jnp.minimum(i, n_slabs - 1)
            in_specs.append(pl.BlockSpec((None, None, slab, cols),
                                         lambda i, slab_of=slab_of: (nl, nh, slab_of(i), 0)))
            args.append(w)
            out_specs.append(pl.BlockSpec((slab, cols), lambda i, slab_of=slab_of: (slab_of(i), 0)))
            out_shape.append(jax.ShapeDtypeStruct((rows, cols), BF16))
    if final:
        in_specs.append(_sel_spec((), (1, D_MODEL)))
        args.append(final_g.reshape(1, D_MODEL))
    outs = pl.pallas_call(
        functools.partial(_ffn_kernel, mod0=6 * half, final=final, n_cast=n_cast, ctx=ctx),
        grid=(n_blocks,),
        in_specs=in_specs,
        out_specs=out_specs,
        out_shape=out_shape,
        scratch_shapes=[pltpu.VMEM((2, tm, D_MODEL), BF16)],
        compiler_params=_params(1),
        name="ffn_half",
    )(*args)
    n_main = 2 if ctx else 1
    return outs[:n_main], tuple(outs[n_main:])


def _conv_glu_kernel(x_ref, mod_ref, g_ref, w1_ref, b1_ref, u_ref, w1b_ref):
    _cast_once(w1_ref, w1b_ref)
    h = _rms_mod(x_ref[...], g_ref[...], mod_ref[3:4, :], mod_ref[4:5, :]).astype(BF16)
    a = _dot(h, w1b_ref[:, :D_MODEL]) + b1_ref[:, :D_MODEL]
    gt = _dot(h, w1b_ref[:, D_MODEL:]) + b1_ref[:, D_MODEL:]
    u_ref[...] = a * jax.nn.sigmoid(gt)


def _conv_glu(x, mods, norm_g, w1, b1, layer, stream, j, tm):
    t = x.shape[0]
    return pl.pallas_call(
        _conv_glu_kernel,
        grid=(t // tm,),
        in_specs=[
            _tok_spec(tm),
            _sel_spec((layer, stream), (N_MOD, D_MODEL)),
            _sel_spec((layer, 1), (1, D_MODEL)),
            _sel_spec((j,), (D_MODEL, 2 * D_MODEL)),
            _sel_spec((j,), (1, 2 * D_MODEL)),
        ],
        out_specs=_tok_spec(tm),
        out_shape=jax.ShapeDtypeStruct((t, D_MODEL), F32),
        scratch_shapes=[pltpu.VMEM((D_MODEL, 2 * D_MODEL), BF16)],
        compiler_params=_params(1),
        name="conv_glu",
    )(x, mods, norm_g, w1, b1)


def _conv_tail_kernel(ul_ref, um_ref, ur_ref, x_ref, mod_ref, wdw_ref, bdw_ref, lng_ref, lnb_ref,
                      w2_ref, b2_ref, o_ref, win_ref, cv_ref, w2b_ref, *, tm):
    _cast_once(w2_ref, w2b_ref)
    i = pl.program_id(0)
    last = pl.num_programs(0) - 1
    n_slabs = D_MODEL // V7X_LANES
    for c in range(n_slabs):
        cs = slice(c * V7X_LANES, (c + 1) * V7X_LANES)
        win_ref[c, 0:CONV_HALO, :] = jnp.where(i > 0, ul_ref[:, cs], 0.0)
        win_ref[c, CONV_HALO:CONV_HALO + tm, :] = um_ref[:, cs]
        win_ref[c, CONV_HALO + tm:, :] = jnp.where(i < last, ur_ref[:, cs], 0.0)
    off0 = CONV_HALO - CONV_WIDTH // 2
    group = V7X_SUBLANES * CONV_STRIDE
    for c in range(n_slabs):
        cs = slice(c * V7X_LANES, (c + 1) * V7X_LANES)
        strided = lambda start: pl.ds(start, V7X_SUBLANES, stride=CONV_STRIDE)
        for k0 in range(0, CONV_WIDTH, CONV_TAPS_PER_PASS):
            taps = [wdw_ref[k:k + 1, cs] for k in range(k0, min(k0 + CONV_TAPS_PER_PASS, CONV_WIDTH))]
            for g in range(tm // group):
                v = [win_ref[c, strided(g * group + off0 + k0 + m), :]
                     for m in range(len(taps) + CONV_STRIDE - 1)]
                for j in range(CONV_STRIDE):
                    acc = taps[0] * v[j]
                    for k in range(1, len(taps)):
                        acc = acc + taps[k] * v[j + k]
                    if k0 > 0:
                        acc = acc + cv_ref[c, strided(g * group + j), :]
                    cv_ref[c, strided(g * group + j), :] = acc
    cv = jnp.concatenate([cv_ref[c] for c in range(n_slabs)], axis=1) + bdw_ref[...]
    mu = jnp.mean(cv, axis=-1, keepdims=True)
    cen = cv - mu
    var = jnp.mean(cen * cen, axis=-1, keepdims=True)
    y = cen * lax.rsqrt(var + LN_EPS) * lng_ref[...] + lnb_ref[...]
    out = _dot(jax.nn.silu(y).astype(BF16), w2b_ref[...]) + b2_ref[...]
    o_ref[...] = x_ref[...] + mod_ref[5:6, :] * out


def _conv_tail(u, x, mods, wdw, bdw, lng, lnb, w2, b2, layer, stream, j, tm):
    t = x.shape[0]
    hb = tm // CONV_HALO
    n_halo_blocks = t // CONV_HALO
    vec = _sel_spec((j,), (1, D_MODEL))
    return pl.pallas_call(
        functools.partial(_conv_tail_kernel, tm=tm),
        grid=(t // tm,),
        in_specs=[
            pl.BlockSpec((CONV_HALO, D_MODEL), lambda i: (jnp.maximum(i * hb - 1, 0), 0)),
            _tok_spec(tm),
            pl.BlockSpec((CONV_HALO, D_MODEL),
                         lambda i: (jnp.minimum((i + 1) * hb, n_halo_blocks - 1), 0)),
            _tok_spec(tm),
            _sel_spec((layer, stream), (N_MOD, D_MODEL)),
            _sel_spec((j,), (CONV_WIDTH, D_MODEL)),
            vec, vec, vec,
            _sel_spec((j,), (D_MODEL, D_MODEL)),
            vec,
        ],
        out_specs=_tok_spec(tm),
        out_shape=jax.ShapeDtypeStruct((t, D_MODEL), F32),
        scratch_shapes=[
            pltpu.VMEM((D_MODEL // V7X_LANES, tm + 2 * CONV_HALO, V7X_LANES), F32),
            pltpu.VMEM((D_MODEL // V7X_LANES, tm, V7X_LANES), F32),
            pltpu.VMEM((D_MODEL, D_MODEL), BF16),
        ],
        compiler_params=_params(1),
        name="conv_tail",
    )(u, u, u, x, mods, wdw, bdw, lng, lnb, w2, b2)


def _qkv_kernel(x_ref, mod_ref, g_ref, w_ref, b_ref, q_ref, k_ref, v_ref, wb_ref):
    _cast_once(w_ref, wb_ref)
    h = _rms_mod(x_ref[...], g_ref[...], mod_ref[3:4, :], mod_ref[4:5, :]).astype(BF16)
    d = D_MODEL
    q = _dot(h, wb_ref[:, 0:d]) + b_ref[:, 0:d]
    q_ref[...] = (q * SCORE_SCALE).astype(BF16)
    k_ref[...] = _dot(h, wb_ref[:, d:2 * d]) + b_ref[:, d:2 * d]
    v_ref[...] = _dot(h, wb_ref[:, 2 * d:3 * d]) + b_ref[:, 2 * d:3 * d]


def _qkv_proj(x, mods, norm_g, w, b, layer, stream, j, tm):
    t = x.shape[0]
    return pl.pallas_call(
        _qkv_kernel,
        grid=(t // tm,),
        in_specs=[
            _tok_spec(tm),
            _sel_spec((layer, stream), (N_MOD, D_MODEL)),
            _sel_spec((layer, 1), (1, D_MODEL)),
            _sel_spec((j,), (D_MODEL, 3 * D_MODEL)),
            _sel_spec((j,), (1, 3 * D_MODEL)),
        ],
        out_specs=[_tok_spec(tm)] * 3,
        out_shape=[
            jax.ShapeDtypeStruct((t, D_MODEL), BF16),
            jax.ShapeDtypeStruct((t, D_MODEL), F32),
            jax.ShapeDtypeStruct((t, D_MODEL), F32),
        ],
        scratch_shapes=[pltpu.VMEM((D_MODEL, 3 * D_MODEL), BF16)],
        compiler_params=_params(1),
        name="qkv_proj",
    )(x, mods, norm_g, w, b)


def _head_lane_masks():
    lane = lax.broadcasted_iota(jnp.int32, (1, V7X_MXU), 1)
    return [(lane >= h * HEAD_DIM) & (lane < (h + 1) * HEAD_DIM) for h in range(NA_GROUP)]


def _stack_heads(q, masks):
    return jnp.concatenate([q * m.astype(q.dtype) for m in masks], axis=0)


def _softmax_pv(scores, values):
    mx = functools.reduce(jnp.maximum, [jnp.max(s, axis=-1, keepdims=True) for s in scores])
    ps = [jnp.exp2(s - mx) for s in scores]
    denom = functools.reduce(jnp.add, [jnp.sum(p, axis=-1, keepdims=True) for p in ps])
    o = functools.reduce(jnp.add, [_dot(p.astype(BF16), v) for p, v in zip(ps, values)])
    return o / denom


def _unstack_heads(o, masks, nq):
    out = o[(NA_GROUP - 1) * nq:]
    for h in range(NA_GROUP - 2, -1, -1):
        out = jnp.where(masks[h], o[h * nq:(h + 1) * nq], out)
    return out


def _build_bias_tables(rpb_ref, colmask_ref, rowmask_ref, tab_ref):
    n_rows = 2 * WIN_R - 1
    front = NA_Q_ROWS - WIN_R + NA_HALO_ROWS
    back = NA_Q_ROWS + NA_KEY_ROWS - front - n_rows
    masked = jnp.full((WIN_C, NA_SPAN), MASK_VALUE, F32)
    top = (NA_Q_ROWS - 1) * NA_SPAN
    for n in range(GRID_W // WIN_C):
        c0 = min(max(n * WIN_C - WIN_C // 2, 0), GRID_W - NA_SPAN)
        lane0 = c0 - n * WIN_C + WIN_C - 1
        keep = colmask_ref[n] > 0.5
        for hh in range(NA_GROUP):
            pieces = []
            for d in range(n_rows):
                row = jnp.broadcast_to(rpb_ref[hh, d:d + 1, :] * LOG2_E, (WIN_C, V7X_LANES))
                rolled = pltpu.roll(row, (V7X_LANES - lane0) % V7X_LANES, 1, stride=1, stride_axis=0)
                pieces.append(jnp.where(keep, rolled[:, :NA_SPAN], MASK_VALUE))
            compact = jnp.concatenate([masked] * front + pieces + [masked] * back, axis=1)
            for rho in range(NA_Q_ROWS):
                strip = compact[:, top - rho * NA_SPAN:top - rho * NA_SPAN + NA_NK]
                r0 = hh * NA_NQ + rho * WIN_C
                for cls in range(3):
                    tab_ref[cls, n, r0:r0 + WIN_C, :] = jnp.where(
                        rowmask_ref[cls, rho] > 0.5, strip, MASK_VALUE)


def _na_kernel(q_ref, kt_ref, km_ref, kb_ref, vt_ref, vm_ref, vb_ref, kc_ref, vc_ref, rpb_ref, colmask_ref,
               rowmask_ref, o_ref, tab_ref):
    b = pl.program_id(1)
    last = pl.num_programs(1) - 1

    @pl.when(b == 0)
    def _():
        _build_bias_tables(rpb_ref, colmask_ref, rowmask_ref, tab_ref)

    masks = _head_lane_masks()
    step_rows = NA_STEP_BLOCKS * NA_Q_ROWS
    q3 = q_ref[...].reshape(step_rows, GRID_W, V7X_MXU)
    kw = jnp.concatenate([kt_ref[...], km_ref[...], kb_ref[...]], axis=0)
    vw = jnp.concatenate([vt_ref[...], vm_ref[...], vb_ref[...]], axis=0)
    kw3 = kw.reshape(step_rows + 2 * NA_HALO_ROWS, GRID_W, V7X_MXU)
    vw3 = vw.reshape(step_rows + 2 * NA_HALO_ROWS, GRID_W, V7X_MXU)
    kc = kc_ref[...].astype(BF16)
    vc = vc_ref[...].astype(BF16)
    for sb in range(NA_STEP_BLOCKS):
        cls = 1
        if sb == 0:
            cls = jnp.where(b == 0, 0, cls)
        if sb == NA_STEP_BLOCKS - 1:
            cls = jnp.where(b == last, 2, cls)
        r_lo = sb * NA_Q_ROWS
        for n in range(GRID_W // WIN_C):
            c0 = min(max(n * WIN_C - WIN_C // 2, 0), GRID_W - NA_SPAN)
            qn = q3[r_lo:r_lo + NA_Q_ROWS, n * WIN_C:(n + 1) * WIN_C, :].reshape(NA_NQ, V7X_MXU)
            kn = kw3[r_lo:r_lo + NA_KEY_ROWS, c0:c0 + NA_SPAN, :].reshape(NA_NK, V7X_MXU).astype(BF16)
            vn = vw3[r_lo:r_lo + NA_KEY_ROWS, c0:c0 + NA_SPAN, :].reshape(NA_NK, V7X_MXU).astype(BF16)
            qs = _stack_heads(qn, masks)
            s_lat = _dot_nt(qs, kn) + tab_ref[cls, n]
            s_ctx = _dot_nt(qs, kc)
            o = _unstack_heads(_softmax_pv([s_lat, s_ctx], [vn, vc]), masks, NA_NQ)
            o3 = o.astype(BF16).reshape(NA_Q_ROWS, WIN_C, V7X_MXU)
            for r in range(NA_Q_ROWS):
                row0 = (r_lo + r) * GRID_W + n * WIN_C
                o_ref[row0:row0 + WIN_C, :] = o3[r]


def _na_attention(q, k, v, kc, vc, rpb, colmask, rowmask, j):
    t = q.shape[0]
    tq = NA_STEP_BLOCKS * NA_Q_ROWS * GRID_W
    th = NA_HALO_ROWS * GRID_W
    nb = t // tq
    n_halo_blocks = t // th
    hpb = tq // th
    ncb = GRID_W // WIN_C
    main = lambda: pl.BlockSpec((tq, V7X_MXU), lambda g, b: (b, g))
    top = lambda: pl.BlockSpec((th, V7X_MXU), lambda g, b: (jnp.maximum(b * hpb - 1, 0), g))
    bot = lambda: pl.BlockSpec(
        (th, V7X_MXU), lambda g, b: (jnp.minimum((b + 1) * hpb, n_halo_blocks - 1), g))
    ctx = lambda: pl.BlockSpec((kc.shape[0], V7X_MXU), lambda g, b: (0, g))
    return pl.pallas_call(
        _na_kernel,
        grid=(N_HEADS // NA_GROUP, nb),
        in_specs=[main(), top(), main(), bot(), top(), main(), bot(), ctx(), ctx(),
                  pl.BlockSpec((None, NA_GROUP) + rpb.shape[2:], lambda g, b: (j, g, 0, 0)),
                  pl.BlockSpec(colmask.shape, lambda g, b: (0, 0, 0)),
                  pl.BlockSpec(rowmask.shape, lambda g, b: (0, 0, 0, 0))],
        out_specs=main(),
        out_shape=jax.ShapeDtypeStruct((t, D_MODEL), BF16),
        scratch_shapes=[pltpu.VMEM((3, ncb, NA_GROUP * NA_NQ, NA_NK), F32)],
        compiler_params=_params(2),
        name="neighbourhood_attention",
    )(q, k, k, k, v, v, v, kc, vc, rpb, colmask, rowmask)


def _ctx_attn_kernel(q_ref, k_ref, v_ref, o_ref):
    masks = _head_lane_masks()
    nq = q_ref.shape[0]
    qs = _stack_heads(q_ref[...], masks)
    k = k_ref[...].astype(BF16)
    v = v_ref[...].astype(BF16)
    o_ref[...] = _unstack_heads(_softmax_pv([_dot_nt(qs, k)], [v]), masks, nq).astype(BF16)


def _ctx_attention(q, k, v):
    t = q.shape[0]
    blk = pl.BlockSpec((t, V7X_MXU), lambda g: (0, g))
    return pl.pallas_call(
        _ctx_attn_kernel,
        grid=(N_HEADS // NA_GROUP,),
        in_specs=[blk, blk, blk],
        out_specs=blk,
        out_shape=jax.ShapeDtypeStruct((t, D_MODEL), BF16),
        compiler_params=_params(1),
        name="context_attention",
    )(q, k, v)


def _out_proj_kernel(o_ref, x_ref, mod_ref, w_ref, b_ref, y_ref, wb_ref):
    _cast_once(w_ref, wb_ref)
    y = _dot(o_ref[...], wb_ref[...]) + b_ref[...]
    y_ref[...] = x_ref[...] + mod_ref[5:6, :] * y


def _out_proj(o, x, mods, w, b, layer, stream, j, tm):
    t = x.shape[0]
    return pl.pallas_call(
        _out_proj_kernel,
        grid=(t // tm,),
        in_specs=[_tok_spec(tm), _tok_spec(tm),
                  _sel_spec((layer, stream), (N_MOD, D_MODEL)),
                  _sel_spec((j,), (D_MODEL, D_MODEL)),
                  _sel_spec((j,), (1, D_MODEL))],
        out_specs=_tok_spec(tm),
        out_shape=jax.ShapeDtypeStruct((t, D_MODEL), F32),
        scratch_shapes=[pltpu.VMEM((D_MODEL, D_MODEL), BF16)],
        compiler_params=_params(1),
        name="attn_out_proj",
    )(o, x, mods, w, b)


def _na_index_tables(rows):
    ncb = GRID_W // WIN_C
    n = np.arange(ncb)
    c0 = np.clip(n * WIN_C - WIN_C // 2, 0, GRID_W - NA_SPAN)
    q_col = n[:, None] * WIN_C + np.arange(WIN_C)[None, :]
    k_col = c0[:, None] + np.arange(NA_SPAN)[None, :]
    q_start = np.clip(q_col - WIN_C // 2, 0, GRID_W - WIN_C)
    col_valid = ((k_col[:, None, :] >= q_start[:, :, None])
                 & (k_col[:, None, :] < q_start[:, :, None] + WIN_C))
    wr = min(WIN_R, rows)
    rho = np.arange(NA_Q_ROWS)
    kappa = np.arange(NA_KEY_ROWS)
    row_valid = []
    for r_start in (0, NA_Q_ROWS, rows - NA_Q_ROWS):
        r = r_start + rho
        j = r_start - NA_HALO_ROWS + kappa
        r0 = np.clip(r - wr // 2, 0, rows - wr)
        row_valid.append((j[None, :] >= r0[:, None]) & (j[None, :] < r0[:, None] + wr))
    row_valid = np.repeat(np.stack(row_valid), NA_SPAN, axis=-1)
    rowmask = row_valid.astype(np.float32).reshape(3, NA_Q_ROWS, 1, NA_NK)
    return col_valid, rowmask


def kernel(x, c, ctx, c_ctx, mod_w, mod_b, norm_g, ffn_w_gate, ffn_w_up, ffn_w_down, conv_w_pw1, conv_b_pw1, conv_w_dw, conv_b_dw, conv_ln_g, conv_ln_b, conv_w_pw2, conv_b_pw2, na_w_qkv, na_b_qkv, na_rpb, na_w_o, na_b_o, final_g):
    b, t, d = x.shape
    assert b == 1 and d == D_MODEL and t % (NA_STEP_BLOCKS * NA_Q_ROWS * GRID_W) == 0
    n_ctx = ctx.shape[1]
    tm = {0: 512, 1: n_ctx}
    tm_mm = {0: 1024, 1: n_ctx}
    col_valid, rowmask = _na_index_tables(t // GRID_W)
    rowmask = jnp.asarray(rowmask)
    colmask = jnp.asarray(col_valid, F32)
    rpb = jnp.pad(na_rpb, ((0, 0),) * 3 + ((0, V7X_LANES - na_rpb.shape[-1]),))
    xs = {0: x.reshape(t, d), 1: ctx.reshape(n_ctx, d)}

    cc = jnp.zeros((V7X_SUBLANES, d), F32).at[0].set(c[0]).at[1].set(c_ctx)
    mods = _modulations(cc, mod_w, mod_b).reshape(DEPTH, V7X_SUBLANES, N_MOD, d)
    ng = norm_g.reshape(DEPTH, 3, 1, d)
    row = lambda v: v.reshape(v.shape[0], 1, v.shape[1])
    conv_vecs = tuple(row(v) for v in (conv_b_dw, conv_ln_g, conv_ln_b))
    ffn_w32 = (ffn_w_gate, ffn_w_up, ffn_w_down)
    w_cur = tuple(w[0, 0].astype(BF16) for w in ffn_w32)

    for i in range(DEPTH):
        mixer, j = i % 2, i // 2
        last = i == DEPTH - 1
        for half in (0, 1):
            if half == 1:
                if mixer == 0:
                    for s in ((0,) if last else (0, 1)):
                        u = _conv_glu(xs[s], mods, ng, conv_w_pw1, row(conv_b_pw1), i, s, j, tm_mm[s])
                        xs[s] = _conv_tail(u, xs[s], mods, conv_w_dw, *conv_vecs, conv_w_pw2,
                                           row(conv_b_pw2), i, s, j, tm[s])
                else:
                    qkv = {s: _qkv_proj(xs[s], mods, ng, na_w_qkv, row(na_b_qkv), i, s, j, tm_mm[s])
                           for s in (0, 1)}
                    o = {0: _na_attention(*qkv[0], qkv[1][1], qkv[1][2], rpb, colmask, rowmask, j)}
                    if not last:
                        o[1] = _ctx_attention(*qkv[1])
                    for s in o:
                        xs[s] = _out_proj(o[s], xs[s], mods, na_w_o, row(na_b_o), i, s, j, tm_mm[s])
            final = last and half == 1
            nxt = None if final else (ffn_w32, i + half, 1 - half)
            outs, w_next = _ffn_half(xs[0], mods, ng, w_cur, i, half, tm[0], xc=None if final else xs[1],
                                     final_g=final_g if final else None, next_w=nxt)
            xs[0] = outs[0]
            if not final:
                xs[1] = outs[1]
                w_cur = w_next
    return xs[0].reshape(b, t, d)
```

```python
import functools

import numpy as np
import jax
import jax.numpy as jnp
from jax import lax
from jax.experimental import pallas as pl
from jax.experimental.pallas import tpu as pltpu

D_MODEL = 1024
DEPTH = 4
GRID_W = 64
N_HEADS = 16
HEAD_DIM = D_MODEL // N_HEADS
D_FF = 2816
CONV_WIDTH = 31
WIN_R = 8
WIN_C = 16
N_MOD = 9
RMS_EPS = 1e-6
LN_EPS = 1e-5
MASK_VALUE = -1e30
LOG2_E = 1.4426950408889634
SCORE_SCALE = HEAD_DIM ** -0.5 * LOG2_E

F32 = jnp.float32
BF16 = jnp.bfloat16

V7X_LANES = 128
V7X_SUBLANES = 8
V7X_MXU = 256
FF_CHUNK = V7X_MXU
FFN_PREP_ROWS = 64
MOD_PER_STEP = 3
CONV_HALO = 16
CONV_STRIDE = 4
CONV_TAPS_PER_PASS = 16
NA_Q_ROWS = 8
NA_STEP_BLOCKS = 4
NA_HALO_ROWS = WIN_R // 2
NA_KEY_ROWS = NA_Q_ROWS + 2 * NA_HALO_ROWS
NA_SPAN = 2 * WIN_C
NA_GROUP = V7X_MXU // HEAD_DIM
NA_NQ = NA_Q_ROWS * WIN_C
NA_NK = NA_KEY_ROWS * NA_SPAN
VMEM_LIMIT = 58 * 1024 * 1024


def _sel_spec(prefix, tail):
    prefix, tail = tuple(prefix), tuple(tail)
    return pl.BlockSpec((None,) * len(prefix) + tail, lambda *_: prefix + (0,) * len(tail),
                        pipeline_mode=pl.Buffered(1))


def _tok_spec(tm):
    return pl.BlockSpec((tm, D_MODEL), lambda i: (i, 0))


def _params(n_grid_dims):
    return pltpu.CompilerParams(
        dimension_semantics=("arbitrary",) * n_grid_dims, vmem_limit_bytes=VMEM_LIMIT)


def _dot(a, b):
    return jnp.dot(a, b, preferred_element_type=F32)


def _dot_nt(a, b):
    return lax.dot_general(a, b, (((1,), (1,)), ((), ())), preferred_element_type=F32)


def _rmsnorm(x, g):
    return x * lax.rsqrt(jnp.mean(x * x, axis=-1, keepdims=True) + RMS_EPS) * g


def _rms_mod(x, g, shift, scale):
    return _rmsnorm(x, g) * (1.0 + scale) + shift


def _cast_once(w_ref, wb_ref):
    @pl.when(pl.program_id(0) == 0)
    def _():
        wb_ref[...] = w_ref[...].astype(BF16)


def _next_tok_spec(tm, n_blocks):
    return pl.BlockSpec((tm, D_MODEL), lambda i: (jnp.minimum(i + 1, n_blocks - 1), 0))


def _prime_norm(x_ref, h_ref, norm):
    i = pl.program_id(0)

    @pl.when(i == 0)
    def _():
        h_ref[0] = norm(x_ref[...])

    return i % 2


def _prep_next(xn_ref, h_ref, slot, k, n_pieces, norm):
    piece = xn_ref.shape[0] // n_pieces
    rows = slice(k * piece, (k + 1) * piece)
    h_ref[1 - slot, rows, :] = norm(xn_ref[rows, :])


def _mod_kernel(cc_ref, w_ref, b_ref, o_ref):
    s = jax.nn.silu(cc_ref[...]).astype(BF16)
    o_ref[...] = _dot(s, w_ref[...].astype(BF16)) + b_ref[...]


def _modulations(cc, mod_w, mod_b):
    rows = cc.shape[0]
    return pl.pallas_call(
        _mod_kernel,
        grid=(DEPTH, N_MOD // MOD_PER_STEP),
        in_specs=[
            pl.BlockSpec((rows, D_MODEL), lambda l, j: (0, 0)),
            pl.BlockSpec((None, D_MODEL, MOD_PER_STEP * D_MODEL), lambda l, j: (l, 0, j)),
            pl.BlockSpec((None, 1, MOD_PER_STEP * D_MODEL), lambda l, j: (l, 0, j)),
        ],
        out_specs=pl.BlockSpec((None, rows, MOD_PER_STEP * D_MODEL), lambda l, j: (l, 0, j)),
        out_shape=jax.ShapeDtypeStruct((DEPTH, rows, N_MOD * D_MODEL), F32),
        compiler_params=_params(2),
        name="modulations",
    )(cc, mod_w, mod_b.reshape(DEPTH, 1, N_MOD * D_MODEL))


def _swiglu(h, wg_ref, wu_ref, wd_ref, after_chunk=None):
    acc = None
    for f in range(D_FF // FF_CHUNK):
        sl = slice(f * FF_CHUNK, (f + 1) * FF_CHUNK)
        hf = h()
        a = jax.nn.silu(_dot(hf, wg_ref[:, sl])) * _dot(hf, wu_ref[:, sl])
        part = _dot(a.astype(BF16), wd_ref[sl, :])
        acc = part if acc is None else acc + part
        if after_chunk is not None:
            after_chunk(f)
    return acc


def _ffn_kernel(x_ref, xn_ref, mod_ref, g_ref, wg_ref, wu_ref, wd_ref, *rest, mod0, final, n_cast, ctx):
    if ctx:
        (xc_ref, modc_ref), rest = rest[:2], rest[2:]
    cast_in, rest = rest[:n_cast], rest[n_cast:]
    if final:
        final_ref, rest = rest[0], rest[1:]
    o_ref, rest = rest[0], rest[1:]
    if ctx:
        oc_ref, rest = rest[0], rest[1:]
    cast_out, h_ref = rest[:-1], rest[-1]
    mod_rows = lambda ref: (ref[mod0 + k:mod0 + k + 1, :] for k in range(3))
    shift, scale, gate = mod_rows(mod_ref)
    norm = lambda v: _rms_mod(v, g_ref[...], shift, scale).astype(BF16)
    slot = _prime_norm(x_ref, h_ref, norm)
    n_pieces = x_ref.shape[0] // FFN_PREP_ROWS

    def after_chunk(f):
        for piece in range(f, n_pieces, D_FF // FF_CHUNK):
            _prep_next(xn_ref, h_ref, slot, piece, n_pieces, norm)

    acc = _swiglu(lambda: h_ref[slot], wg_ref, wu_ref, wd_ref, after_chunk)
    out = x_ref[...] + 0.5 * gate * acc
    if final:
        out = _rmsnorm(out, final_ref[...])
    o_ref[...] = out
    for src, dst in zip(cast_in, cast_out):
        dst[...] = src[...].astype(BF16)

    if ctx:
        @pl.when(pl.program_id(0) == pl.num_programs(0) - 1)
        def _():
            shift_c, scale_c, gate_c = mod_rows(modc_ref)
            xc = xc_ref[...]
            hc = _rms_mod(xc, g_ref[...], shift_c, scale_c).astype(BF16)
            oc_ref[...] = xc + 0.5 * gate_c * _swiglu(lambda: hc, wg_ref, wu_ref, wd_ref)


def _ffn_half(x, mods, norm_g, w_bf16, layer, half, tm, xc=None, final_g=None, next_w=None):
    t = x.shape[0]
    final = final_g is not None
    ctx = xc is not None
    n_blocks = t // tm
    wg, wu, wd = w_bf16
    in_specs = [
        _tok_spec(tm),
        _next_tok_spec(tm, n_blocks),
        _sel_spec((layer, 0), (N_MOD, D_MODEL)),
        _sel_spec((layer, 2 * half), (1, D_MODEL)),
        _sel_spec((), (D_MODEL, D_FF)),
        _sel_spec((), (D_MODEL, D_FF)),
        _sel_spec((), (D_FF, D_MODEL)),
    ]
    args = [x, x, mods, norm_g, wg, wu, wd]
    out_specs = [_tok_spec(tm)]
    out_shape = [jax.ShapeDtypeStruct((t, D_MODEL), F32)]
    if ctx:
        in_specs += [_sel_spec((), xc.shape), _sel_spec((layer, 1), (N_MOD, D_MODEL))]
        args += [xc, mods]
        out_specs.append(pl.BlockSpec(xc.shape, lambda i: (0, 0)))
        out_shape.append(jax.ShapeDtypeStruct(xc.shape, F32))
    n_cast = 0
    if next_w is not None:
        stacked, nl, nh = next_w
        n_cast = len(stacked)
        for w in stacked:
            rows, cols = w.shape[2:]
            slab = next(s for s in range(16, rows + 1, 16) if rows % s == 0 and rows // s <= n_blocks)
            slab_of = lambda i, n_slabs=rows // slab: jnp.minimum(i, n_slabs - 1)
            in_specs.append(pl.BlockSpec((None, None, slab, cols),
                                         lambda i, slab_of=slab_of: (nl, nh, slab_of(i), 0)))
            args.append(w)
            out_specs.append(pl.BlockSpec((slab, cols), lambda i, slab_of=slab_of: (slab_of(i), 0)))
            out_shape.append(jax.ShapeDtypeStruct((rows, cols), BF16))
    if final:
        in_specs.append(_sel_spec((), (1, D_MODEL)))
        args.append(final_g.reshape(1, D_MODEL))
    outs = pl.pallas_call(
        functools.partial(_ffn_kernel, mod0=6 * half, final=final, n_cast=n_cast, ctx=ctx),
        grid=(n_blocks,),
        in_specs=in_specs,
        out_specs=out_specs,
        out_shape=out_shape,
        scratch_shapes=[pltpu.VMEM((2, tm, D_MODEL), BF16)],
        compiler_params=_params(1),
        name="ffn_half",
    )(*args)
    n_main = 2 if ctx else 1
    return outs[:n_main], tuple(outs[n_main:])


def _ctx_spec(xc):
    return pl.BlockSpec(xc.shape, lambda i: (0, 0))


def _on_last_step(fn):
    pl.when(pl.program_id(0) == pl.num_programs(0) - 1)(fn)


def _conv_glu_block(x, mod_ref, g_ref, w1b_ref, b1_ref, u_ref):
    h = _rms_mod(x, g_ref[...], mod_ref[3:4, :], mod_ref[4:5, :]).astype(BF16)
    a = _dot(h, w1b_ref[:, :D_MODEL]) + b1_ref[:, :D_MODEL]
    gt = _dot(h, w1b_ref[:, D_MODEL:]) + b1_ref[:, D_MODEL:]
    u_ref[...] = a * jax.nn.sigmoid(gt)


def _conv_glu_kernel(x_ref, mod_ref, g_ref, w1_ref, b1_ref, *rest, ctx):
    w1b_ref = rest[-1]
    _cast_once(w1_ref, w1b_ref)
    u_ref = rest[2] if ctx else rest[0]
    _conv_glu_block(x_ref[...], mod_ref, g_ref, w1b_ref, b1_ref, u_ref)
    if ctx:
        xc_ref, modc_ref, _, uc_ref = rest[:4]
        _on_last_step(lambda: _conv_glu_block(xc_ref[...], modc_ref, g_ref, w1b_ref, b1_ref, uc_ref))


def _conv_glu(x, mods, norm_g, w1, b1, layer, j, tm, xc=None):
    t = x.shape[0]
    ctx = xc is not None
    in_specs = [
        _tok_spec(tm),
        _sel_spec((layer, 0), (N_MOD, D_MODEL)),
        _sel_spec((layer, 1), (1, D_MODEL)),
        _sel_spec((j,), (D_MODEL, 2 * D_MODEL)),
        _sel_spec((j,), (1, 2 * D_MODEL)),
    ]
    args = [x, mods, norm_g, w1, b1]
    out_specs = [_tok_spec(tm)]
    out_shape = [jax.ShapeDtypeStruct((t, D_MODEL), F32)]
    if ctx:
        in_specs += [_sel_spec((), xc.shape), _sel_spec((layer, 1), (N_MOD, D_MODEL))]
        args += [xc, mods]
        out_specs.append(_ctx_spec(xc))
        out_shape.append(jax.ShapeDtypeStruct(xc.shape, F32))
    return pl.pallas_call(
        functools.partial(_conv_glu_kernel, ctx=ctx),
        grid=(t // tm,),
        in_specs=in_specs,
        out_specs=out_specs,
        out_shape=out_shape,
        scratch_shapes=[pltpu.VMEM((D_MODEL, 2 * D_MODEL), BF16)],
        compiler_params=_params(1),
        name="conv_glu",
    )(*args)


def _conv_tail_kernel(ul_ref, um_ref, ur_ref, x_ref, mod_ref, wdw_ref, bdw_ref, lng_ref, lnb_ref,
                      w2_ref, b2_ref, o_ref, win_ref, cv_ref, w2b_ref, *, tm):
    _cast_once(w2_ref, w2b_ref)
    i = pl.program_id(0)
    last = pl.num_programs(0) - 1
    n_slabs = D_MODEL // V7X_LANES
    for c in range(n_slabs):
        cs = slice(c * V7X_LANES, (c + 1) * V7X_LANES)
        win_ref[c, 0:CONV_HALO, :] = jnp.where(i > 0, ul_ref[:, cs], 0.0)
        win_ref[c, CONV_HALO:CONV_HALO + tm, :] = um_ref[:, cs]
        win_ref[c, CONV_HALO + tm:, :] = jnp.where(i < last, ur_ref[:, cs], 0.0)
    off0 = CONV_HALO - CONV_WIDTH // 2
    group = V7X_SUBLANES * CONV_STRIDE
    for c in range(n_slabs):
        cs = slice(c * V7X_LANES, (c + 1) * V7X_LANES)
        strided = lambda start: pl.ds(start, V7X_SUBLANES, stride=CONV_STRIDE)
        for k0 in range(0, CONV_WIDTH, CONV_TAPS_PER_PASS):
            taps = [wdw_ref[k:k + 1, cs] for k in range(k0, min(k0 + CONV_TAPS_PER_PASS, CONV_WIDTH))]
            for g in range(tm // group):
                v = [win_ref[c, strided(g * group + off0 + k0 + m), :]
                     for m in range(len(taps) + CONV_STRIDE - 1)]
                for j in range(CONV_STRIDE):
                    acc = taps[0] * v[j]
                    for k in range(1, len(taps)):
                        acc = acc + taps[k] * v[j + k]
                    if k0 > 0:
                        acc = acc + cv_ref[c, strided(g * group + j), :]
                    cv_ref[c, strided(g * group + j), :] = acc
    cv = jnp.concatenate([cv_ref[c] for c in range(n_slabs)], axis=1) + bdw_ref[...]
    mu = jnp.mean(cv, axis=-1, keepdims=True)
    cen = cv - mu
    var = jnp.mean(cen * cen, axis=-1, keepdims=True)
    y = cen * lax.rsqrt(var + LN_EPS) * lng_ref[...] + lnb_ref[...]
    out = _dot(jax.nn.silu(y).astype(BF16), w2b_ref[...]) + b2_ref[...]
    o_ref[...] = x_ref[...] + mod_ref[5:6, :] * out


def _conv_tail(u, x, mods, wdw, bdw, lng, lnb, w2, b2, layer, stream, j, tm):
    t = x.shape[0]
    hb = tm // CONV_HALO
    n_halo_blocks = t // CONV_HALO
    vec = _sel_spec((j,), (1, D_MODEL))
    return pl.pallas_call(
        functools.partial(_conv_tail_kernel, tm=tm),
        grid=(t // tm,),
        in_specs=[
            pl.BlockSpec((CONV_HALO, D_MODEL), lambda i: (jnp.maximum(i * hb - 1, 0), 0)),
            _tok_spec(tm),
            pl.BlockSpec((CONV_HALO, D_MODEL),
                         lambda i: (jnp.minimum((i + 1) * hb, n_halo_blocks - 1), 0)),
            _tok_spec(tm),
            _sel_spec((layer, stream), (N_MOD, D_MODEL)),
            _sel_spec((j,), (CONV_WIDTH, D_MODEL)),
            vec, vec, vec,
            _sel_spec((j,), (D_MODEL, D_MODEL)),
            vec,
        ],
        out_specs=_tok_spec(tm),
        out_shape=jax.ShapeDtypeStruct((t, D_MODEL), F32),
        scratch_shapes=[
            pltpu.VMEM((D_MODEL // V7X_LANES, tm + 2 * CONV_HALO, V7X_LANES), F32),
            pltpu.VMEM((D_MODEL // V7X_LANES, tm, V7X_LANES), F32),
            pltpu.VMEM((D_MODEL, D_MODEL), BF16),
        ],
        compiler_params=_params(1),
        name="conv_tail",
    )(u, u, u, x, mods, wdw, bdw, lng, lnb, w2, b2)


def _qkv_block(x, mod_ref, g_ref, wb_ref, b_ref, q_ref, k_ref, v_ref):
    h = _rms_mod(x, g_ref[...], mod_ref[3:4, :], mod_ref[4:5, :]).astype(BF16)
    d = D_MODEL
    q = _dot(h, wb_ref[:, 0:d]) + b_ref[:, 0:d]
    q_ref[...] = (q * SCORE_SCALE).astype(BF16)
    k_ref[...] = _dot(h, wb_ref[:, d:2 * d]) + b_ref[:, d:2 * d]
    v_ref[...] = _dot(h, wb_ref[:, 2 * d:3 * d]) + b_ref[:, 2 * d:3 * d]


def _qkv_kernel(x_ref, mod_ref, g_ref, w_ref, b_ref, xc_ref, modc_ref, q_ref, k_ref, v_ref,
                qc_ref, kc_ref, vc_ref, wb_ref):
    _cast_once(w_ref, wb_ref)
    _qkv_block(x_ref[...], mod_ref, g_ref, wb_ref, b_ref, q_ref, k_ref, v_ref)
    _on_last_step(lambda: _qkv_block(xc_ref[...], modc_ref, g_ref, wb_ref, b_ref, qc_ref, kc_ref, vc_ref))


def _qkv_proj(x, xc, mods, norm_g, w, b, layer, j, tm):
    t, tc = x.shape[0], xc.shape[0]
    shapes = lambda n: [jax.ShapeDtypeStruct((n, D_MODEL), BF16), jax.ShapeDtypeStruct((n, D_MODEL), F32),
                        jax.ShapeDtypeStruct((n, D_MODEL), F32)]
    outs = pl.pallas_call(
        _qkv_kernel,
        grid=(t // tm,),
        in_specs=[
            _tok_spec(tm),
            _sel_spec((layer, 0), (N_MOD, D_MODEL)),
            _sel_spec((layer, 1), (1, D_MODEL)),
            _sel_spec((j,), (D_MODEL, 3 * D_MODEL)),
            _sel_spec((j,), (1, 3 * D_MODEL)),
            _sel_spec((), xc.shape),
            _sel_spec((layer, 1), (N_MOD, D_MODEL)),
        ],
        out_specs=[_tok_spec(tm)] * 3 + [_ctx_spec(xc)] * 3,
        out_shape=shapes(t) + shapes(tc),
        scratch_shapes=[pltpu.VMEM((D_MODEL, 3 * D_MODEL), BF16)],
        compiler_params=_params(1),
        name="qkv_proj",
    )(x, mods, norm_g, w, b, xc, mods)
    return tuple(outs[:3]), tuple(outs[3:])


def _head_lane_masks():
    lane = lax.broadcasted_iota(jnp.int32, (1, V7X_MXU), 1)
    return [(lane >= h * HEAD_DIM) & (lane < (h + 1) * HEAD_DIM) for h in range(NA_GROUP)]


def _stack_heads(q, masks):
    return jnp.concatenate([q * m.astype(q.dtype) for m in masks], axis=0)


def _softmax_pv(scores, values):
    mx = functools.reduce(jnp.maximum, [jnp.max(s, axis=-1, keepdims=True) for s in scores])
    ps = [jnp.exp2(s - mx) for s in scores]
    denom = functools.reduce(jnp.add, [jnp.sum(p, axis=-1, keepdims=True) for p in ps])
    o = functools.reduce(jnp.add, [_dot(p.astype(BF16), v) for p, v in zip(ps, values)])
    return o / denom


def _unstack_heads(o, masks, nq):
    out = o[(NA_GROUP - 1) * nq:]
    for h in range(NA_GROUP - 2, -1, -1):
        out = jnp.where(masks[h], o[h * nq:(h + 1) * nq], out)
    return out


def _build_bias_tables(rpb_ref, colmask_ref, rowmask_ref, tab_ref):
    n_rows = 2 * WIN_R - 1
    front = NA_Q_ROWS - WIN_R + NA_HALO_ROWS
    back = NA_Q_ROWS + NA_KEY_ROWS - front - n_rows
    masked = jnp.full((WIN_C, NA_SPAN), MASK_VALUE, F32)
    top = (NA_Q_ROWS - 1) * NA_SPAN
    for n in range(GRID_W // WIN_C):
        c0 = min(max(n * WIN_C - WIN_C // 2, 0), GRID_W - NA_SPAN)
        lane0 = c0 - n * WIN_C + WIN_C - 1
        keep = colmask_ref[n] > 0.5
        for hh in range(NA_GROUP):
            pieces = []
            for d in range(n_rows):
                row = jnp.broadcast_to(rpb_ref[hh, d:d + 1, :] * LOG2_E, (WIN_C, V7X_LANES))
                rolled = pltpu.roll(row, (V7X_LANES - lane0) % V7X_LANES, 1, stride=1, stride_axis=0)
                pieces.append(jnp.where(keep, rolled[:, :NA_SPAN], MASK_VALUE))
            compact = jnp.concatenate([masked] * front + pieces + [masked] * back, axis=1)
            for rho in range(NA_Q_ROWS):
                strip = compact[:, top - rho * NA_SPAN:top - rho * NA_SPAN + NA_NK]
                r0 = hh * NA_NQ + rho * WIN_C
                for cls in range(3):
                    tab_ref[cls, n, r0:r0 + WIN_C, :] = jnp.where(
                        rowmask_ref[cls, rho] > 0.5, strip, MASK_VALUE)


def _na_kernel(q_ref, kt_ref, km_ref, kb_ref, vt_ref, vm_ref, vb_ref, kc_ref, vc_ref, rpb_ref, colmask_ref,
               rowmask_ref, o_ref, tab_ref):
    b = pl.program_id(1)
    last = pl.num_programs(1) - 1

    @pl.when(b == 0)
    def _():
        _build_bias_tables(rpb_ref, colmask_ref, rowmask_ref, tab_ref)

    masks = _head_lane_masks()
    step_rows = NA_STEP_BLOCKS * NA_Q_ROWS
    q3 = q_ref[...].reshape(step_rows, GRID_W, V7X_MXU)
    kw = jnp.concatenate([kt_ref[...], km_ref[...], kb_ref[...]], axis=0)
    vw = jnp.concatenate([vt_ref[...], vm_ref[...], vb_ref[...]], axis=0)
    kw3 = kw.reshape(step_rows + 2 * NA_HALO_ROWS, GRID_W, V7X_MXU)
    vw3 = vw.reshape(step_rows + 2 * NA_HALO_ROWS, GRID_W, V7X_MXU)
    kc = kc_ref[...].astype(BF16)
    vc = vc_ref[...].astype(BF16)
    for sb in range(NA_STEP_BLOCKS):
        cls = 1
        if sb == 0:
            cls = jnp.where(b == 0, 0, cls)
        if sb == NA_STEP_BLOCKS - 1:
            cls = jnp.where(b == last, 2, cls)
        r_lo = sb * NA_Q_ROWS
        for n in range(GRID_W // WIN_C):
            c0 = min(max(n * WIN_C - WIN_C // 2, 0), GRID_W - NA_SPAN)
            qn = q3[r_lo:r_lo + NA_Q_ROWS, n * WIN_C:(n + 1) * WIN_C, :].reshape(NA_NQ, V7X_MXU)
            kn = kw3[r_lo:r_lo + NA_KEY_ROWS, c0:c0 + NA_SPAN, :].reshape(NA_NK, V7X_MXU).astype(BF16)
            vn = vw3[r_lo:r_lo + NA_KEY_ROWS, c0:c0 + NA_SPAN, :].reshape(NA_NK, V7X_MXU).astype(BF16)
            qs = _stack_heads(qn, masks)
            s_lat = _dot_nt(qs, kn) + tab_ref[cls, n]
            s_ctx = _dot_nt(qs, kc)
            o = _unstack_heads(_softmax_pv([s_lat, s_ctx], [vn, vc]), masks, NA_NQ)
            o3 = o.astype(BF16).reshape(NA_Q_ROWS, WIN_C, V7X_MXU)
            for r in range(NA_Q_ROWS):
                row0 = (r_lo + r) * GRID_W + n * WIN_C
                o_ref[row0:row0 + WIN_C, :] = o3[r]


def _na_attention(q, k, v, kc, vc, rpb, colmask, rowmask, j):
    t = q.shape[0]
    tq = NA_STEP_BLOCKS * NA_Q_ROWS * GRID_W
    th = NA_HALO_ROWS * GRID_W
    nb = t // tq
    n_halo_blocks = t // th
    hpb = tq // th
    ncb = GRID_W // WIN_C
    main = lambda: pl.BlockSpec((tq, V7X_MXU), lambda g, b: (b, g))
    top = lambda: pl.BlockSpec((th, V7X_MXU), lambda g, b: (jnp.maximum(b * hpb - 1, 0), g))
    bot = lambda: pl.BlockSpec(
        (th, V7X_MXU), lambda g, b: (jnp.minimum((b + 1) * hpb, n_halo_blocks - 1), g))
    ctx = lambda: pl.BlockSpec((kc.shape[0], V7X_MXU), lambda g, b: (0, g))
    return pl.pallas_call(
        _na_kernel,
        grid=(N_HEADS // NA_GROUP, nb),
        in_specs=[main(), top(), main(), bot(), top(), main(), bot(), ctx(), ctx(),
                  pl.BlockSpec((None, NA_GROUP) + rpb.shape[2:], lambda g, b: (j, g, 0, 0)),
                  pl.BlockSpec(colmask.shape, lambda g, b: (0, 0, 0)),
                  pl.BlockSpec(rowmask.shape, lambda g, b: (0, 0, 0, 0))],
        out_specs=main(),
        out_shape=jax.ShapeDtypeStruct((t, D_MODEL), BF16),
        scratch_shapes=[pltpu.VMEM((3, ncb, NA_GROUP * NA_NQ, NA_NK), F32)],
        compiler_params=_params(2),
        name="neighbourhood_attention",
    )(q, k, k, k, v, v, v, kc, vc, rpb, colmask, rowmask)


def _ctx_attn_kernel(q_ref, k_ref, v_ref, o_ref):
    masks = _head_lane_masks()
    nq = q_ref.shape[0]
    qs = _stack_heads(q_ref[...], masks)
    k = k_ref[...].astype(BF16)
    v = v_ref[...].astype(BF16)
    o_ref[...] = _unstack_heads(_softmax_pv([_dot_nt(qs, k)], [v]), masks, nq).astype(BF16)


def _ctx_attention(q, k, v):
    t = q.shape[0]
    blk = pl.BlockSpec((t, V7X_MXU), lambda g: (0, g))
    return pl.pallas_call(
        _ctx_attn_kernel,
        grid=(N_HEADS // NA_GROUP,),
        in_specs=[blk, blk, blk],
        out_specs=blk,
        out_shape=jax.ShapeDtypeStruct((t, D_MODEL), BF16),
        compiler_params=_params(1),
        name="context_attention",
    )(q, k, v)


def _out_proj_block(o, x, mod_ref, wb_ref, b_ref, y_ref):
    y_ref[...] = x + mod_ref[5:6, :] * (_dot(o, wb_ref[...]) + b_ref[...])


def _out_proj_kernel(o_ref, x_ref, mod_ref, w_ref, b_ref, *rest, ctx):
    wb_ref = rest[-1]
    _cast_once(w_ref, wb_ref)
    y_ref = rest[3] if ctx else rest[0]
    _out_proj_block(o_ref[...], x_ref[...], mod_ref, wb_ref, b_ref, y_ref)
    if ctx:
        oc_ref, xc_ref, modc_ref, _, yc_ref = rest[:5]
        _on_last_step(lambda: _out_proj_block(oc_ref[...], xc_ref[...], modc_ref, wb_ref, b_ref, yc_ref))


def _out_proj(o, x, mods, w, b, layer, j, tm, ctx_streams=None):
    t = x.shape[0]
    ctx = ctx_streams is not None
    in_specs = [_tok_spec(tm), _tok_spec(tm),
                _sel_spec((layer, 0), (N_MOD, D_MODEL)),
                _sel_spec((j,), (D_MODEL, D_MODEL)),
                _sel_spec((j,), (1, D_MODEL))]
    args = [o, x, mods, w, b]
    out_specs = [_tok_spec(tm)]
    out_shape = [jax.ShapeDtypeStruct((t, D_MODEL), F32)]
    if ctx:
        oc, xc = ctx_streams
        in_specs += [_sel_spec((), oc.shape), _sel_spec((), xc.shape), _sel_spec((layer, 1), (N_MOD, D_MODEL))]
        args += [oc, xc, mods]
        out_specs.append(_ctx_spec(xc))
        out_shape.append(jax.ShapeDtypeStruct(xc.shape, F32))
    return pl.pallas_call(
        functools.partial(_out_proj_kernel, ctx=ctx),
        grid=(t // tm,),
        in_specs=in_specs,
        out_specs=out_specs,
        out_shape=out_shape,
        scratch_shapes=[pltpu.VMEM((D_MODEL, D_MODEL), BF16)],
        compiler_params=_params(1),
        name="attn_out_proj",
    )(*args)


def _na_index_tables(rows):
    ncb = GRID_W // WIN_C
    n = np.arange(ncb)
    c0 = np.clip(n * WIN_C - WIN_C // 2, 0, GRID_W - NA_SPAN)
    q_col = n[:, None] * WIN_C + np.arange(WIN_C)[None, :]
    k_col = c0[:, None] + np.arange(NA_SPAN)[None, :]
    q_start = np.clip(q_col - WIN_C // 2, 0, GRID_W - WIN_C)
    col_valid = ((k_col[:, None, :] >= q_start[:, :, None])
                 & (k_col[:, None, :] < q_start[:, :, None] + WIN_C))
    wr = min(WIN_R, rows)
    rho = np.arange(NA_Q_ROWS)
    kappa = np.arange(NA_KEY_ROWS)
    row_valid = []
    for r_start in (0, NA_Q_ROWS, rows - NA_Q_ROWS):
        r = r_start + rho
        j = r_start - NA_HALO_ROWS + kappa
        r0 = np.clip(r - wr // 2, 0, rows - wr)
        row_valid.append((j[None, :] >= r0[:, None]) & (j[None, :] < r0[:, None] + wr))
    row_valid = np.repeat(np.stack(row_valid), NA_SPAN, axis=-1)
    rowmask = row_valid.astype(np.float32).reshape(3, NA_Q_ROWS, 1, NA_NK)
    return col_valid, rowmask


def kernel(x, c, ctx, c_ctx, mod_w, mod_b, norm_g, ffn_w_gate, ffn_w_up, ffn_w_down, conv_w_pw1, conv_b_pw1, conv_w_dw, conv_b_dw, conv_ln_g, conv_ln_b, conv_w_pw2, conv_b_pw2, na_w_qkv, na_b_qkv, na_rpb, na_w_o, na_b_o, final_g):
    b, t, d = x.shape
    assert b == 1 and d == D_MODEL and t % (NA_STEP_BLOCKS * NA_Q_ROWS * GRID_W) == 0
    n_ctx = ctx.shape[1]
    tm = {0: 512, 1: n_ctx}
    tm_mm = {0: 1024, 1: n_ctx}
    col_valid, rowmask = _na_index_tables(t // GRID_W)
    rowmask = jnp.asarray(rowmask)
    colmask = jnp.asarray(col_valid, F32)
    rpb = jnp.pad(na_rpb, ((0, 0),) * 3 + ((0, V7X_LANES - na_rpb.shape[-1]),))
    xs = {0: x.reshape(t, d), 1: ctx.reshape(n_ctx, d)}

    cc = jnp.zeros((V7X_SUBLANES, d), F32).at[0].set(c[0]).at[1].set(c_ctx)
    mods = _modulations(cc, mod_w, mod_b).reshape(DEPTH, V7X_SUBLANES, N_MOD, d)
    ng = norm_g.reshape(DEPTH, 3, 1, d)
    row = lambda v: v.reshape(v.shape[0], 1, v.shape[1])
    conv_vecs = tuple(row(v) for v in (conv_b_dw, conv_ln_g, conv_ln_b))
    ffn_w32 = (ffn_w_gate, ffn_w_up, ffn_w_down)
    w_cur = tuple(w[0, 0].astype(BF16) for w in ffn_w32)

    for i in range(DEPTH):
        mixer, j = i % 2, i // 2
        last = i == DEPTH - 1
        for half in (0, 1):
            if half == 1:
                if mixer == 0:
                    us = _conv_glu(xs[0], mods, ng, conv_w_pw1, row(conv_b_pw1), i, j, tm_mm[0],
                                   xc=None if last else xs[1])
                    for s, u in enumerate(us):
                        xs[s] = _conv_tail(u, xs[s], mods, conv_w_dw, *conv_vecs, conv_w_pw2,
                                           row(conv_b_pw2), i, s, j, tm[s])
                else:
                    qkv, qkv_c = _qkv_proj(xs[0], xs[1], mods, ng, na_w_qkv, row(na_b_qkv), i, j, tm_mm[0])
                    o = _na_attention(*qkv, qkv_c[1], qkv_c[2], rpb, colmask, rowmask, j)
                    ctx_streams = None if last else (_ctx_attention(*qkv_c), xs[1])
                    ys = _out_proj(o, xs[0], mods, na_w_o, row(na_b_o), i, j, tm_mm[0], ctx_streams)
                    for s, y in enumerate(ys):
                        xs[s] = y
            final = last and half == 1
            nxt = None if final else (ffn_w32, i + half, 1 - half)
            outs, w_next = _ffn_half(xs[0], mods, ng, w_cur, i, half, tm[0], xc=None if final else xs[1],
                                     final_g=final_g if final else None, next_w=nxt)
            xs[0] = outs[0]
            if not final:
                xs[1] = outs[1]
                w_cur = w_next
    return xs[0].reshape(b, t, d)
```

```python
import functools

import numpy as np
import jax
import jax.numpy as jnp
from jax import lax
from jax.experimental import pallas as pl
from jax.experimental.pallas import tpu as pltpu

D_MODEL = 1024
DEPTH = 4
GRID_W = 64
N_HEADS = 16
HEAD_DIM = D_MODEL // N_HEADS
D_FF = 2816
CONV_WIDTH = 31
WIN_R = 8
WIN_C = 16
N_MOD = 9
RMS_EPS = 1e-6
LN_EPS = 1e-5
MASK_VALUE = -1e30
LOG2_E = 1.4426950408889634
SCORE_SCALE = HEAD_DIM ** -0.5 * LOG2_E

F32 = jnp.float32
BF16 = jnp.bfloat16

V7X_LANES = 128
V7X_SUBLANES = 8
V7X_MXU = 256
FF_CHUNK = V7X_MXU
FFN_PREP_ROWS = 64
MOD_PER_STEP = 3
CONV_HALO = 16
CONV_STRIDE = 4
CONV_TAPS_PER_PASS = 16
NA_Q_ROWS = 8
NA_STEP_BLOCKS = 4
NA_HALO_ROWS = WIN_R // 2
NA_KEY_ROWS = NA_Q_ROWS + 2 * NA_HALO_ROWS
NA_SPAN = 2 * WIN_C
NA_GROUP = V7X_MXU // HEAD_DIM
NA_NQ = NA_Q_ROWS * WIN_C
NA_NK = NA_KEY_ROWS * NA_SPAN
VMEM_LIMIT = 58 * 1024 * 1024


def _sel_spec(prefix, tail):
    prefix, tail = tuple(prefix), tuple(tail)
    return pl.BlockSpec((None,) * len(prefix) + tail, lambda *_: prefix + (0,) * len(tail),
                        pipeline_mode=pl.Buffered(1))


def _tok_spec(tm):
    return pl.BlockSpec((tm, D_MODEL), lambda i: (i, 0))


def _params(n_grid_dims):
    return pltpu.CompilerParams(
        dimension_semantics=("arbitrary",) * n_grid_dims, vmem_limit_bytes=VMEM_LIMIT)


def _dot(a, b):
    return jnp.dot(a, b, preferred_element_type=F32)


def _dot_nt(a, b):
    return lax.dot_general(a, b, (((1,), (1,)), ((), ())), preferred_element_type=F32)


def _rmsnorm(x, g):
    return x * lax.rsqrt(jnp.mean(x * x, axis=-1, keepdims=True) + RMS_EPS) * g


def _rms_mod(x, g, shift, scale):
    return _rmsnorm(x, g) * (1.0 + scale) + shift


def _cast_once(w_ref, wb_ref):
    @pl.when(pl.program_id(0) == 0)
    def _():
        wb_ref[...] = w_ref[...].astype(BF16)


def _next_tok_spec(tm, n_blocks):
    return pl.BlockSpec((tm, D_MODEL), lambda i: (jnp.minimum(i + 1, n_blocks - 1), 0))


def _prime_norm(x_ref, h_ref, norm):
    i = pl.program_id(0)

    @pl.when(i == 0)
    def _():
        h_ref[0] = norm(x_ref[...])

    return i % 2


def _prep_next(xn_ref, h_ref, slot, k, n_pieces, norm):
    piece = xn_ref.shape[0] // n_pieces
    rows = slice(k * piece, (k + 1) * piece)
    h_ref[1 - slot, rows, :] = norm(xn_ref[rows, :])


def _mod_kernel(cc_ref, w_ref, b_ref, o_ref):
    s = jax.nn.silu(cc_ref[...]).astype(BF16)
    o_ref[...] = _dot(s, w_ref[...].astype(BF16)) + b_ref[...]


def _modulations(cc, mod_w, mod_b):
    rows = cc.shape[0]
    return pl.pallas_call(
        _mod_kernel,
        grid=(DEPTH, N_MOD // MOD_PER_STEP),
        in_specs=[
            pl.BlockSpec((rows, D_MODEL), lambda l, j: (0, 0)),
            pl.BlockSpec((None, D_MODEL, MOD_PER_STEP * D_MODEL), lambda l, j: (l, 0, j)),
            pl.BlockSpec((None, 1, MOD_PER_STEP * D_MODEL), lambda l, j: (l, 0, j)),
        ],
        out_specs=pl.BlockSpec((None, rows, MOD_PER_STEP * D_MODEL), lambda l, j: (l, 0, j)),
        out_shape=jax.ShapeDtypeStruct((DEPTH, rows, N_MOD * D_MODEL), F32),
        compiler_params=_params(2),
        name="modulations",
    )(cc, mod_w, mod_b.reshape(DEPTH, 1, N_MOD * D_MODEL))


def _swiglu(h, wg_ref, wu_ref, wd_ref, after_chunk=None):
    acc = None
    for f in range(D_FF // FF_CHUNK):
        sl = slice(f * FF_CHUNK, (f + 1) * FF_CHUNK)
        hf = h()
        a = jax.nn.silu(_dot(hf, wg_ref[:, sl])) * _dot(hf, wu_ref[:, sl])
        part = _dot(a.astype(BF16), wd_ref[sl, :])
        acc = part if acc is None else acc + part
        if after_chunk is not None:
            after_chunk(f)
    return acc


def _ffn_kernel(x_ref, xn_ref, mod_ref, g_ref, wg_ref, wu_ref, wd_ref, *rest, mod0, final, n_cast, ctx, mix):
    if mix:
        (a_ref, an_ref, wp_ref, bp_ref), rest = rest[:4], rest[4:]
    if ctx:
        (xc_ref, modc_ref), rest = rest[:2], rest[2:]
        if mix:
            ac_ref, rest = rest[0], rest[1:]
    cast_in, rest = rest[:n_cast], rest[n_cast:]
    if final:
        final_ref, rest = rest[0], rest[1:]
    o_ref, rest = rest[0], rest[1:]
    if ctx:
        oc_ref, rest = rest[0], rest[1:]
    if mix:
        xres_ref, rest = rest[-1], rest[:-1]
    cast_out, h_ref = rest[:-1], rest[-1]
    mod_rows = lambda ref: (ref[mod0 + k:mod0 + k + 1, :] for k in range(3))
    shift, scale, gate = mod_rows(mod_ref)
    norm = lambda v: _rms_mod(v, g_ref[...], shift, scale).astype(BF16)
    mixed = lambda x, a, m_ref: x + m_ref[5:6, :] * (_dot(a, wp_ref[...]) + bp_ref[...])
    i = pl.program_id(0)
    slot = i % 2
    n_pieces = x_ref.shape[0] // FFN_PREP_ROWS

    @pl.when(i == 0)
    def _():
        if mix:
            xres_ref[0] = mixed(x_ref[...], a_ref[...], mod_ref)
            h_ref[0] = norm(xres_ref[0])
        else:
            h_ref[0] = norm(x_ref[...])

    if mix:
        xres_ref[1 - slot] = mixed(xn_ref[...], an_ref[...], mod_ref)
    nxt_ref = xres_ref.at[1 - slot] if mix else xn_ref

    def after_chunk(f):
        for piece in range(f, n_pieces, D_FF // FF_CHUNK):
            _prep_next(nxt_ref, h_ref, slot, piece, n_pieces, norm)

    acc = _swiglu(lambda: h_ref[slot], wg_ref, wu_ref, wd_ref, after_chunk)
    out = (xres_ref[slot] if mix else x_ref[...]) + 0.5 * gate * acc
    if final:
        out = _rmsnorm(out, final_ref[...])
    o_ref[...] = out
    for src, dst in zip(cast_in, cast_out):
        dst[...] = src[...].astype(BF16)

    if ctx:
        @pl.when(i == pl.num_programs(0) - 1)
        def _():
            shift_c, scale_c, gate_c = mod_rows(modc_ref)
            xc = mixed(xc_ref[...], ac_ref[...], modc_ref) if mix else xc_ref[...]
            hc = _rms_mod(xc, g_ref[...], shift_c, scale_c).astype(BF16)
            oc_ref[...] = xc + 0.5 * gate_c * _swiglu(lambda: hc, wg_ref, wu_ref, wd_ref)


def _ffn_half(x, mods, norm_g, w_bf16, layer, half, tm, xc=None, final_g=None, next_w=None, mix=None):
    t = x.shape[0]
    final = final_g is not None
    ctx = xc is not None
    n_blocks = t // tm
    wg, wu, wd = w_bf16
    in_specs = [
        _tok_spec(tm),
        _next_tok_spec(tm, n_blocks),
        _sel_spec((layer, 0), (N_MOD, D_MODEL)),
        _sel_spec((layer, 2 * half), (1, D_MODEL)),
        _sel_spec((), (D_MODEL, D_FF)),
        _sel_spec((), (D_MODEL, D_FF)),
        _sel_spec((), (D_FF, D_MODEL)),
    ]
    args = [x, x, mods, norm_g, wg, wu, wd]
    out_specs = [_tok_spec(tm)]
    out_shape = [jax.ShapeDtypeStruct((t, D_MODEL), F32)]
    scratch = [pltpu.VMEM((2, tm, D_MODEL), BF16)]
    if mix is not None:
        a, ac, wp, bp, j = mix
        in_specs += [_tok_spec(tm), _next_tok_spec(tm, n_blocks), _sel_spec((j,), (D_MODEL, D_MODEL)),
                     _sel_spec((j,), (1, D_MODEL))]
        args += [a, a, wp, bp]
        scratch.append(pltpu.VMEM((2, tm, D_MODEL), F32))
    if ctx:
        in_specs += [_sel_spec((), xc.shape), _sel_spec((layer, 1), (N_MOD, D_MODEL))]
        args += [xc, mods]
        if mix is not None:
            in_specs.append(_sel_spec((), ac.shape))
            args.append(ac)
        out_specs.append(pl.BlockSpec(xc.shape, lambda i: (0, 0)))
        out_shape.append(jax.ShapeDtypeStruct(xc.shape, F32))
    n_cast = 0
    if next_w is not None:
        stacked, nl, nh = next_w
        n_cast = len(stacked)
        for w in stacked:
            rows, cols = w.shape[2:]
            slab = next(s for s in range(16, rows + 1, 16) if rows % s == 0 and rows // s <= n_blocks)
            slab_of = lambda i, n_slabs=rows // slab: jnp.minimum(i, n_slabs - 1)
            in_specs.append(pl.BlockSpec((None, None, slab, cols),
                                         lambda i, slab_of=slab_of: (nl, nh, slab_of(i), 0)))
            args.append(w)
            out_specs.append(pl.BlockSpec((slab, cols), lambda i, slab_of=slab_of: (slab_of(i), 0)))
            out_shape.append(jax.ShapeDtypeStruct((rows, cols), BF16))
    if final:
        in_specs.append(_sel_spec((), (1, D_MODEL)))
        args.append(final_g.reshape(1, D_MODEL))
    outs = pl.pallas_call(
        functools.partial(_ffn_kernel, mod0=6 * half, final=final, n_cast=n_cast, ctx=ctx,
                          mix=mix is not None),
        grid=(n_blocks,),
        in_specs=in_specs,
        out_specs=out_specs,
        out_shape=out_shape,
        scratch_shapes=scratch,
        compiler_params=_params(1),
        name="ffn_half",
    )(*args)
    n_main = 2 if ctx else 1
    return outs[:n_main], tuple(outs[n_main:])


def _ctx_spec(xc):
    return pl.BlockSpec(xc.shape, lambda i: (0, 0))


def _on_last_step(fn):
    pl.when(pl.program_id(0) == pl.num_programs(0) - 1)(fn)


def _conv_glu_block(x, mod_ref, g_ref, w1b_ref, b1_ref, u_ref):
    h = _rms_mod(x, g_ref[...], mod_ref[3:4, :], mod_ref[4:5, :]).astype(BF16)
    a = _dot(h, w1b_ref[:, :D_MODEL]) + b1_ref[:, :D_MODEL]
    gt = _dot(h, w1b_ref[:, D_MODEL:]) + b1_ref[:, D_MODEL:]
    u_ref[...] = a * jax.nn.sigmoid(gt)


def _conv_glu_kernel(x_ref, mod_ref, g_ref, w1_ref, b1_ref, *rest, ctx):
    w1b_ref = rest[-1]
    _cast_once(w1_ref, w1b_ref)
    u_ref = rest[2] if ctx else rest[0]
    _conv_glu_block(x_ref[...], mod_ref, g_ref, w1b_ref, b1_ref, u_ref)
    if ctx:
        xc_ref, modc_ref, _, uc_ref = rest[:4]
        _on_last_step(lambda: _conv_glu_block(xc_ref[...], modc_ref, g_ref, w1b_ref, b1_ref, uc_ref))


def _conv_glu(x, mods, norm_g, w1, b1, layer, j, tm, xc=None):
    t = x.shape[0]
    ctx = xc is not None
    in_specs = [
        _tok_spec(tm),
        _sel_spec((layer, 0), (N_MOD, D_MODEL)),
        _sel_spec((layer, 1), (1, D_MODEL)),
        _sel_spec((j,), (D_MODEL, 2 * D_MODEL)),
        _sel_spec((j,), (1, 2 * D_MODEL)),
    ]
    args = [x, mods, norm_g, w1, b1]
    out_specs = [_tok_spec(tm)]
    out_shape = [jax.ShapeDtypeStruct((t, D_MODEL), F32)]
    if ctx:
        in_specs += [_sel_spec((), xc.shape), _sel_spec((layer, 1), (N_MOD, D_MODEL))]
        args += [xc, mods]
        out_specs.append(_ctx_spec(xc))
        out_shape.append(jax.ShapeDtypeStruct(xc.shape, F32))
    return pl.pallas_call(
        functools.partial(_conv_glu_kernel, ctx=ctx),
        grid=(t // tm,),
        in_specs=in_specs,
        out_specs=out_specs,
        out_shape=out_shape,
        scratch_shapes=[pltpu.VMEM((D_MODEL, 2 * D_MODEL), BF16)],
        compiler_params=_params(1),
        name="conv_glu",
    )(*args)


def _conv_tail_kernel(ul_ref, um_ref, ur_ref, x_ref, mod_ref, wdw_ref, bdw_ref, lng_ref, lnb_ref,
                      w2_ref, b2_ref, o_ref, win_ref, cv_ref, w2b_ref, *, tm):
    _cast_once(w2_ref, w2b_ref)
    i = pl.program_id(0)
    last = pl.num_programs(0) - 1
    n_slabs = D_MODEL // V7X_LANES
    for c in range(n_slabs):
        cs = slice(c * V7X_LANES, (c + 1) * V7X_LANES)
        win_ref[c, 0:CONV_HALO, :] = jnp.where(i > 0, ul_ref[:, cs], 0.0)
        win_ref[c, CONV_HALO:CONV_HALO + tm, :] = um_ref[:, cs]
        win_ref[c, CONV_HALO + tm:, :] = jnp.where(i < last, ur_ref[:, cs], 0.0)
    off0 = CONV_HALO - CONV_WIDTH // 2
    group = V7X_SUBLANES * CONV_STRIDE
    for c in range(n_slabs):
        cs = slice(c * V7X_LANES, (c + 1) * V7X_LANES)
        strided = lambda start: pl.ds(start, V7X_SUBLANES, stride=CONV_STRIDE)
        for k0 in range(0, CONV_WIDTH, CONV_TAPS_PER_PASS):
            taps = [wdw_ref[k:k + 1, cs] for k in range(k0, min(k0 + CONV_TAPS_PER_PASS, CONV_WIDTH))]
            for g in range(tm // group):
                v = [win_ref[c, strided(g * group + off0 + k0 + m), :]
                     for m in range(len(taps) + CONV_STRIDE - 1)]
                for j in range(CONV_STRIDE):
                    acc = taps[0] * v[j]
                    for k in range(1, len(taps)):
                        acc = acc + taps[k] * v[j + k]
                    if k0 > 0:
                        acc = acc + cv_ref[c, strided(g * group + j), :]
                    cv_ref[c, strided(g * group + j), :] = acc
    cv = jnp.concatenate([cv_ref[c] for c in range(n_slabs)], axis=1) + bdw_ref[...]
    mu = jnp.mean(cv, axis=-1, keepdims=True)
    cen = cv - mu
    var = jnp.mean(cen * cen, axis=-1, keepdims=True)
    y = cen * lax.rsqrt(var + LN_EPS) * lng_ref[...] + lnb_ref[...]
    out = _dot(jax.nn.silu(y).astype(BF16), w2b_ref[...]) + b2_ref[...]
    o_ref[...] = x_ref[...] + mod_ref[5:6, :] * out


def _conv_tail(u, x, mods, wdw, bdw, lng, lnb, w2, b2, layer, stream, j, tm):
    t = x.shape[0]
    hb = tm // CONV_HALO
    n_halo_blocks = t // CONV_HALO
    vec = _sel_spec((j,), (1, D_MODEL))
    return pl.pallas_call(
        functools.partial(_conv_tail_kernel, tm=tm),
        grid=(t // tm,),
        in_specs=[
            pl.BlockSpec((CONV_HALO, D_MODEL), lambda i: (jnp.maximum(i * hb - 1, 0), 0)),
            _tok_spec(tm),
            pl.BlockSpec((CONV_HALO, D_MODEL),
                         lambda i: (jnp.minimum((i + 1) * hb, n_halo_blocks - 1), 0)),
            _tok_spec(tm),
            _sel_spec((layer, stream), (N_MOD, D_MODEL)),
            _sel_spec((j,), (CONV_WIDTH, D_MODEL)),
            vec, vec, vec,
            _sel_spec((j,), (D_MODEL, D_MODEL)),
            vec,
        ],
        out_specs=_tok_spec(tm),
        out_shape=jax.ShapeDtypeStruct((t, D_MODEL), F32),
        scratch_shapes=[
            pltpu.VMEM((D_MODEL // V7X_LANES, tm + 2 * CONV_HALO, V7X_LANES), F32),
            pltpu.VMEM((D_MODEL // V7X_LANES, tm, V7X_LANES), F32),
            pltpu.VMEM((D_MODEL, D_MODEL), BF16),
        ],
        compiler_params=_params(1),
        name="conv_tail",
    )(u, u, u, x, mods, wdw, bdw, lng, lnb, w2, b2)


def _qkv_block(x, mod_ref, g_ref, wb_ref, b_ref, q_ref, k_ref, v_ref):
    h = _rms_mod(x, g_ref[...], mod_ref[3:4, :], mod_ref[4:5, :]).astype(BF16)
    d = D_MODEL
    q = _dot(h, wb_ref[:, 0:d]) + b_ref[:, 0:d]
    q_ref[...] = (q * SCORE_SCALE).astype(BF16)
    k_ref[...] = _dot(h, wb_ref[:, d:2 * d]) + b_ref[:, d:2 * d]
    v_ref[...] = _dot(h, wb_ref[:, 2 * d:3 * d]) + b_ref[:, 2 * d:3 * d]


def _qkv_kernel(x_ref, mod_ref, g_ref, w_ref, b_ref, xc_ref, modc_ref, q_ref, k_ref, v_ref,
                qc_ref, kc_ref, vc_ref, wb_ref):
    _cast_once(w_ref, wb_ref)
    _qkv_block(x_ref[...], mod_ref, g_ref, wb_ref, b_ref, q_ref, k_ref, v_ref)
    _on_last_step(lambda: _qkv_block(xc_ref[...], modc_ref, g_ref, wb_ref, b_ref, qc_ref, kc_ref, vc_ref))


def _qkv_proj(x, xc, mods, norm_g, w, b, layer, j, tm):
    t, tc = x.shape[0], xc.shape[0]
    shapes = lambda n: [jax.ShapeDtypeStruct((n, D_MODEL), BF16), jax.ShapeDtypeStruct((n, D_MODEL), F32),
                        jax.ShapeDtypeStruct((n, D_MODEL), F32)]
    outs = pl.pallas_call(
        _qkv_kernel,
        grid=(t // tm,),
        in_specs=[
            _tok_spec(tm),
            _sel_spec((layer, 0), (N_MOD, D_MODEL)),
            _sel_spec((layer, 1), (1, D_MODEL)),
            _sel_spec((j,), (D_MODEL, 3 * D_MODEL)),
            _sel_spec((j,), (1, 3 * D_MODEL)),
            _sel_spec((), xc.shape),
            _sel_spec((layer, 1), (N_MOD, D_MODEL)),
        ],
        out_specs=[_tok_spec(tm)] * 3 + [_ctx_spec(xc)] * 3,
        out_shape=shapes(t) + shapes(tc),
        scratch_shapes=[pltpu.VMEM((D_MODEL, 3 * D_MODEL), BF16)],
        compiler_params=_params(1),
        name="qkv_proj",
    )(x, mods, norm_g, w, b, xc, mods)
    return tuple(outs[:3]), tuple(outs[3:])


def _head_lane_masks():
    lane = lax.broadcasted_iota(jnp.int32, (1, V7X_MXU), 1)
    return [(lane >= h * HEAD_DIM) & (lane < (h + 1) * HEAD_DIM) for h in range(NA_GROUP)]


def _stack_heads(q, masks):
    return jnp.concatenate([q * m.astype(q.dtype) for m in masks], axis=0)


def _softmax_pv(scores, values):
    mx = functools.reduce(jnp.maximum, [jnp.max(s, axis=-1, keepdims=True) for s in scores])
    ps = [jnp.exp2(s - mx) for s in scores]
    denom = functools.reduce(jnp.add, [jnp.sum(p, axis=-1, keepdims=True) for p in ps])
    o = functools.reduce(jnp.add, [_dot(p.astype(BF16), v) for p, v in zip(ps, values)])
    return o / denom


def _unstack_heads(o, masks, nq):
    out = o[(NA_GROUP - 1) * nq:]
    for h in range(NA_GROUP - 2, -1, -1):
        out = jnp.where(masks[h], o[h * nq:(h + 1) * nq], out)
    return out


def _build_bias_tables(rpb_ref, colmask_ref, rowmask_ref, tab_ref):
    n_rows = 2 * WIN_R - 1
    front = NA_Q_ROWS - WIN_R + NA_HALO_ROWS
    back = NA_Q_ROWS + NA_KEY_ROWS - front - n_rows
    masked = jnp.full((WIN_C, NA_SPAN), MASK_VALUE, F32)
    top = (NA_Q_ROWS - 1) * NA_SPAN
    for n in range(GRID_W // WIN_C):
        c0 = min(max(n * WIN_C - WIN_C // 2, 0), GRID_W - NA_SPAN)
        lane0 = c0 - n * WIN_C + WIN_C - 1
        keep = colmask_ref[n] > 0.5
        for hh in range(NA_GROUP):
            pieces = []
            for d in range(n_rows):
                row = jnp.broadcast_to(rpb_ref[hh, d:d + 1, :] * LOG2_E, (WIN_C, V7X_LANES))
                rolled = pltpu.roll(row, (V7X_LANES - lane0) % V7X_LANES, 1, stride=1, stride_axis=0)
                pieces.append(jnp.where(keep, rolled[:, :NA_SPAN], MASK_VALUE))
            compact = jnp.concatenate([masked] * front + pieces + [masked] * back, axis=1)
            for rho in range(NA_Q_ROWS):
                strip = compact[:, top - rho * NA_SPAN:top - rho * NA_SPAN + NA_NK]
                r0 = hh * NA_NQ + rho * WIN_C
                for cls in range(3):
                    tab_ref[cls, n, r0:r0 + WIN_C, :] = jnp.where(
                        rowmask_ref[cls, rho] > 0.5, strip, MASK_VALUE)


def _na_kernel(q_ref, kt_ref, km_ref, kb_ref, vt_ref, vm_ref, vb_ref, kc_ref, vc_ref, rpb_ref, colmask_ref,
               rowmask_ref, o_ref, tab_ref):
    b = pl.program_id(1)
    last = pl.num_programs(1) - 1

    @pl.when(b == 0)
    def _():
        _build_bias_tables(rpb_ref, colmask_ref, rowmask_ref, tab_ref)

    masks = _head_lane_masks()
    step_rows = NA_STEP_BLOCKS * NA_Q_ROWS
    q3 = q_ref[...].reshape(step_rows, GRID_W, V7X_MXU)
    kw = jnp.concatenate([kt_ref[...], km_ref[...], kb_ref[...]], axis=0)
    vw = jnp.concatenate([vt_ref[...], vm_ref[...], vb_ref[...]], axis=0)
    kw3 = kw.reshape(step_rows + 2 * NA_HALO_ROWS, GRID_W, V7X_MXU)
    vw3 = vw.reshape(step_rows + 2 * NA_HALO_ROWS, GRID_W, V7X_MXU)
    kc = kc_ref[...].astype(BF16)
    vc = vc_ref[...].astype(BF16)
    for sb in range(NA_STEP_BLOCKS):
        cls = 1
        if sb == 0:
            cls = jnp.where(b == 0, 0, cls)
        if sb == NA_STEP_BLOCKS - 1:
            cls = jnp.where(b == last, 2, cls)
        r_lo = sb * NA_Q_ROWS
        for n in range(GRID_W // WIN_C):
            c0 = min(max(n * WIN_C - WIN_C // 2, 0), GRID_W - NA_SPAN)
            qn = q3[r_lo:r_lo + NA_Q_ROWS, n * WIN_C:(n + 1) * WIN_C, :].reshape(NA_NQ, V7X_MXU)
            kn = kw3[r_lo:r_lo + NA_KEY_ROWS, c0:c0 + NA_SPAN, :].reshape(NA_NK, V7X_MXU).astype(BF16)
            vn = vw3[r_lo:r_lo + NA_KEY_ROWS, c0:c0 + NA_SPAN, :].reshape(NA_NK, V7X_MXU).astype(BF16)
            qs = _stack_heads(qn, masks)
            s_lat = _dot_nt(qs, kn) + tab_ref[cls, n]
            s_ctx = _dot_nt(qs, kc)
            o = _unstack_heads(_softmax_pv([s_lat, s_ctx], [vn, vc]), masks, NA_NQ)
            o3 = o.astype(BF16).reshape(NA_Q_ROWS, WIN_C, V7X_MXU)
            for r in range(NA_Q_ROWS):
                row0 = (r_lo + r) * GRID_W + n * WIN_C
                o_ref[row0:row0 + WIN_C, :] = o3[r]


def _na_attention(q, k, v, kc, vc, rpb, colmask, rowmask, j):
    t = q.shape[0]
    tq = NA_STEP_BLOCKS * NA_Q_ROWS * GRID_W
    th = NA_HALO_ROWS * GRID_W
    nb = t // tq
    n_halo_blocks = t // th
    hpb = tq // th
    ncb = GRID_W // WIN_C
    main = lambda: pl.BlockSpec((tq, V7X_MXU), lambda g, b: (b, g))
    top = lambda: pl.BlockSpec((th, V7X_MXU), lambda g, b: (jnp.maximum(b * hpb - 1, 0), g))
    bot = lambda: pl.BlockSpec(
        (th, V7X_MXU), lambda g, b: (jnp.minimum((b + 1) * hpb, n_halo_blocks - 1), g))
    ctx = lambda: pl.BlockSpec((kc.shape[0], V7X_MXU), lambda g, b: (0, g))
    return pl.pallas_call(
        _na_kernel,
        grid=(N_HEADS // NA_GROUP, nb),
        in_specs=[main(), top(), main(), bot(), top(), main(), bot(), ctx(), ctx(),
                  pl.BlockSpec((None, NA_GROUP) + rpb.shape[2:], lambda g, b: (j, g, 0, 0)),
                  pl.BlockSpec(colmask.shape, lambda g, b: (0, 0, 0)),
                  pl.BlockSpec(rowmask.shape, lambda g, b: (0, 0, 0, 0))],
        out_specs=main(),
        out_shape=jax.ShapeDtypeStruct((t, D_MODEL), BF16),
        scratch_shapes=[pltpu.VMEM((3, ncb, NA_GROUP * NA_NQ, NA_NK), F32)],
        compiler_params=_params(2),
        name="neighbourhood_attention",
    )(q, k, k, k, v, v, v, kc, vc, rpb, colmask, rowmask)


def _ctx_attn_kernel(q_ref, k_ref, v_ref, o_ref):
    masks = _head_lane_masks()
    nq = q_ref.shape[0]
    qs = _stack_heads(q_ref[...], masks)
    k = k_ref[...].astype(BF16)
    v = v_ref[...].astype(BF16)
    o_ref[...] = _unstack_heads(_softmax_pv([_dot_nt(qs, k)], [v]), masks, nq).astype(BF16)


def _ctx_attention(q, k, v):
    t = q.shape[0]
    blk = pl.BlockSpec((t, V7X_MXU), lambda g: (0, g))
    return pl.pallas_call(
        _ctx_attn_kernel,
        grid=(N_HEADS // NA_GROUP,),
        in_specs=[blk, blk, blk],
        out_specs=blk,
        out_shape=jax.ShapeDtypeStruct((t, D_MODEL), BF16),
        compiler_params=_params(1),
        name="context_attention",
    )(q, k, v)


def _na_index_tables(rows):
    ncb = GRID_W // WIN_C
    n = np.arange(ncb)
    c0 = np.clip(n * WIN_C - WIN_C // 2, 0, GRID_W - NA_SPAN)
    q_col = n[:, None] * WIN_C + np.arange(WIN_C)[None, :]
    k_col = c0[:, None] + np.arange(NA_SPAN)[None, :]
    q_start = np.clip(q_col - WIN_C // 2, 0, GRID_W - WIN_C)
    col_valid = ((k_col[:, None, :] >= q_start[:, :, None])
                 & (k_col[:, None, :] < q_start[:, :, None] + WIN_C))
    wr = min(WIN_R, rows)
    rho = np.arange(NA_Q_ROWS)
    kappa = np.arange(NA_KEY_ROWS)
    row_valid = []
    for r_start in (0, NA_Q_ROWS, rows - NA_Q_ROWS):
        r = r_start + rho
        j = r_start - NA_HALO_ROWS + kappa
        r0 = np.clip(r - wr // 2, 0, rows - wr)
        row_valid.append((j[None, :] >= r0[:, None]) & (j[None, :] < r0[:, None] + wr))
    row_valid = np.repeat(np.stack(row_valid), NA_SPAN, axis=-1)
    rowmask = row_valid.astype(np.float32).reshape(3, NA_Q_ROWS, 1, NA_NK)
    return col_valid, rowmask


def kernel(x, c, ctx, c_ctx, mod_w, mod_b, norm_g, ffn_w_gate, ffn_w_up, ffn_w_down, conv_w_pw1, conv_b_pw1, conv_w_dw, conv_b_dw, conv_ln_g, conv_ln_b, conv_w_pw2, conv_b_pw2, na_w_qkv, na_b_qkv, na_rpb, na_w_o, na_b_o, final_g):
    b, t, d = x.shape
    assert b == 1 and d == D_MODEL and t % (NA_STEP_BLOCKS * NA_Q_ROWS * GRID_W) == 0
    n_ctx = ctx.shape[1]
    tm = {0: 512, 1: n_ctx}
    tm_mm = {0: 1024, 1: n_ctx}
    col_valid, rowmask = _na_index_tables(t // GRID_W)
    rowmask = jnp.asarray(rowmask)
    colmask = jnp.asarray(col_valid, F32)
    rpb = jnp.pad(na_rpb, ((0, 0),) * 3 + ((0, V7X_LANES - na_rpb.shape[-1]),))
    xs = {0: x.reshape(t, d), 1: ctx.reshape(n_ctx, d)}

    cc = jnp.zeros((V7X_SUBLANES, d), F32).at[0].set(c[0]).at[1].set(c_ctx)
    mods = _modulations(cc, mod_w, mod_b).reshape(DEPTH, V7X_SUBLANES, N_MOD, d)
    ng = norm_g.reshape(DEPTH, 3, 1, d)
    row = lambda v: v.reshape(v.shape[0], 1, v.shape[1])
    conv_vecs = tuple(row(v) for v in (conv_b_dw, conv_ln_g, conv_ln_b))
    ffn_w32 = (ffn_w_gate, ffn_w_up, ffn_w_down)
    w_cur = tuple(w[0, 0].astype(BF16) for w in ffn_w32)
    wob = na_w_o.astype(BF16)

    for i in range(DEPTH):
        mixer, j = i % 2, i // 2
        last = i == DEPTH - 1
        for half in (0, 1):
            mix = None
            if half == 1:
                if mixer == 0:
                    us = _conv_glu(xs[0], mods, ng, conv_w_pw1, row(conv_b_pw1), i, j, tm_mm[0],
                                   xc=None if last else xs[1])
                    for s, u in enumerate(us):
                        xs[s] = _conv_tail(u, xs[s], mods, conv_w_dw, *conv_vecs, conv_w_pw2,
                                           row(conv_b_pw2), i, s, j, tm[s])
                else:
                    qkv, qkv_c = _qkv_proj(xs[0], xs[1], mods, ng, na_w_qkv, row(na_b_qkv), i, j, tm_mm[0])
                    o = _na_attention(*qkv, qkv_c[1], qkv_c[2], rpb, colmask, rowmask, j)
                    mix = (o, None if last else _ctx_attention(*qkv_c), wob, row(na_b_o), j)
            final = last and half == 1
            nxt = None if final else (ffn_w32, i + half, 1 - half)
            outs, w_next = _ffn_half(xs[0], mods, ng, w_cur, i, half, tm[0], xc=None if final else xs[1],
                                     final_g=final_g if final else None, next_w=nxt, mix=mix)
            xs[0] = outs[0]
            if not final:
                xs[1] = outs[1]
                w_cur = w_next
    return xs[0].reshape(b, t, d)
```

```python
import functools

import numpy as np
import jax
import jax.numpy as jnp
from jax import lax
from jax.experimental import pallas as pl
from jax.experimental.pallas import tpu as pltpu

D_MODEL = 1024
DEPTH = 4
GRID_W = 64
N_HEADS = 16
HEAD_DIM = D_MODEL // N_HEADS
D_FF = 2816
CONV_WIDTH = 31
WIN_R = 8
WIN_C = 16
N_MOD = 9
RMS_EPS = 1e-6
LN_EPS = 1e-5
MASK_VALUE = -1e30
LOG2_E = 1.4426950408889634
SCORE_SCALE = HEAD_DIM ** -0.5 * LOG2_E

F32 = jnp.float32
BF16 = jnp.bfloat16

V7X_LANES = 128
V7X_SUBLANES = 8
V7X_MXU = 256
FF_CHUNK = V7X_MXU
FFN_PREP_ROWS = 64
MOD_PER_STEP = 3
CONV_HALO = 16
CONV_STRIDE = 4
CONV_TAPS_PER_PASS = 16
NA_Q_ROWS = 8
NA_STEP_BLOCKS = 4
NA_HALO_ROWS = WIN_R // 2
NA_KEY_ROWS = NA_Q_ROWS + 2 * NA_HALO_ROWS
NA_SPAN = 2 * WIN_C
NA_GROUP = V7X_MXU // HEAD_DIM
NA_NQ = NA_Q_ROWS * WIN_C
NA_NK = NA_KEY_ROWS * NA_SPAN
VMEM_LIMIT = 58 * 1024 * 1024


def _sel_spec(prefix, tail):
    prefix, tail = tuple(prefix), tuple(tail)
    return pl.BlockSpec((None,) * len(prefix) + tail, lambda *_: prefix + (0,) * len(tail),
                        pipeline_mode=pl.Buffered(1))


def _tok_spec(tm):
    return pl.BlockSpec((tm, D_MODEL), lambda i: (i, 0))


def _params(n_grid_dims):
    return pltpu.CompilerParams(
        dimension_semantics=("arbitrary",) * n_grid_dims, vmem_limit_bytes=VMEM_LIMIT)


def _dot(a, b):
    return jnp.dot(a, b, preferred_element_type=F32)


def _dot_nt(a, b):
    return lax.dot_general(a, b, (((1,), (1,)), ((), ())), preferred_element_type=F32)


def _rmsnorm(x, g):
    return x * lax.rsqrt(jnp.mean(x * x, axis=-1, keepdims=True) + RMS_EPS) * g


def _rms_mod(x, g, shift, scale):
    return _rmsnorm(x, g) * (1.0 + scale) + shift


def _cast_once(w_ref, wb_ref):
    @pl.when(pl.program_id(0) == 0)
    def _():
        wb_ref[...] = w_ref[...].astype(BF16)


def _next_tok_spec(tm, n_blocks):
    return pl.BlockSpec((tm, D_MODEL), lambda i: (jnp.minimum(i + 1, n_blocks - 1), 0))


def _prime_norm(x_ref, h_ref, norm):
    i = pl.program_id(0)

    @pl.when(i == 0)
    def _():
        h_ref[0] = norm(x_ref[...])

    return i % 2


def _prep_next(xn_ref, h_ref, slot, k, n_pieces, norm):
    piece = xn_ref.shape[0] // n_pieces
    rows = slice(k * piece, (k + 1) * piece)
    h_ref[1 - slot, rows, :] = norm(xn_ref[rows, :])


def _mod_kernel(cc_ref, w_ref, b_ref, o_ref):
    s = jax.nn.silu(cc_ref[...]).astype(BF16)
    o_ref[...] = _dot(s, w_ref[...].astype(BF16)) + b_ref[...]


def _modulations(cc, mod_w, mod_b):
    rows = cc.shape[0]
    return pl.pallas_call(
        _mod_kernel,
        grid=(DEPTH, N_MOD // MOD_PER_STEP),
        in_specs=[
            pl.BlockSpec((rows, D_MODEL), lambda l, j: (0, 0)),
            pl.BlockSpec((None, D_MODEL, MOD_PER_STEP * D_MODEL), lambda l, j: (l, 0, j)),
            pl.BlockSpec((None, 1, MOD_PER_STEP * D_MODEL), lambda l, j: (l, 0, j)),
        ],
        out_specs=pl.BlockSpec((None, rows, MOD_PER_STEP * D_MODEL), lambda l, j: (l, 0, j)),
        out_shape=jax.ShapeDtypeStruct((DEPTH, rows, N_MOD * D_MODEL), F32),
        compiler_params=_params(2),
        name="modulations",
    )(cc, mod_w, mod_b.reshape(DEPTH, 1, N_MOD * D_MODEL))


def _swiglu(h, wg_ref, wu_ref, wd_ref, after_chunk=None):
    acc = None
    for f in range(D_FF // FF_CHUNK):
        sl = slice(f * FF_CHUNK, (f + 1) * FF_CHUNK)
        hf = h()
        a = jax.nn.silu(_dot(hf, wg_ref[:, sl])) * _dot(hf, wu_ref[:, sl])
        part = _dot(a.astype(BF16), wd_ref[sl, :])
        acc = part if acc is None else acc + part
        if after_chunk is not None:
            after_chunk(f)
    return acc


def _ffn_kernel(x_ref, xn_ref, mod_ref, g_ref, wg_ref, wu_ref, wd_ref, *rest, mod0, final, n_cast, ctx):
    if ctx:
        (xc_ref, modc_ref), rest = rest[:2], rest[2:]
    cast_in, rest = rest[:n_cast], rest[n_cast:]
    if final:
        final_ref, rest = rest[0], rest[1:]
    o_ref, rest = rest[0], rest[1:]
    if ctx:
        oc_ref, rest = rest[0], rest[1:]
    cast_out, h_ref = rest[:-1], rest[-1]
    mod_rows = lambda ref: (ref[mod0 + k:mod0 + k + 1, :] for k in range(3))
    shift, scale, gate = mod_rows(mod_ref)
    norm = lambda v: _rms_mod(v, g_ref[...], shift, scale).astype(BF16)
    slot = _prime_norm(x_ref, h_ref, norm)
    n_pieces = x_ref.shape[0] // FFN_PREP_ROWS

    def after_chunk(f):
        for piece in range(f, n_pieces, D_FF // FF_CHUNK):
            _prep_next(xn_ref, h_ref, slot, piece, n_pieces, norm)

    acc = _swiglu(lambda: h_ref[slot], wg_ref, wu_ref, wd_ref, after_chunk)
    out = x_ref[...] + 0.5 * gate * acc
    if final:
        out = _rmsnorm(out, final_ref[...])
    o_ref[...] = out
    for src, dst in zip(cast_in, cast_out):
        dst[...] = src[...].astype(BF16)

    if ctx:
        @pl.when(pl.program_id(0) == pl.num_programs(0) - 1)
        def _():
            shift_c, scale_c, gate_c = mod_rows(modc_ref)
            xc = xc_ref[...]
            hc = _rms_mod(xc, g_ref[...], shift_c, scale_c).astype(BF16)
            oc_ref[...] = xc + 0.5 * gate_c * _swiglu(lambda: hc, wg_ref, wu_ref, wd_ref)


def _ffn_half(x, mods, norm_g, w_bf16, layer, half, tm, xc=None, final_g=None, next_w=None):
    t = x.shape[0]
    final = final_g is not None
    ctx = xc is not None
    n_blocks = t // tm
    wg, wu, wd = w_bf16
    in_specs = [
        _tok_spec(tm),
        _next_tok_spec(tm, n_blocks),
        _sel_spec((layer, 0), (N_MOD, D_MODEL)),
        _sel_spec((layer, 2 * half), (1, D_MODEL)),
        _sel_spec((), (D_MODEL, D_FF)),
        _sel_spec((), (D_MODEL, D_FF)),
        _sel_spec((), (D_FF, D_MODEL)),
    ]
    args = [x, x, mods, norm_g, wg, wu, wd]
    out_specs = [_tok_spec(tm)]
    out_shape = [jax.ShapeDtypeStruct((t, D_MODEL), F32)]
    if ctx:
        in_specs += [_sel_spec((), xc.shape), _sel_spec((layer, 1), (N_MOD, D_MODEL))]
        args += [xc, mods]
        out_specs.append(pl.BlockSpec(xc.shape, lambda i: (0, 0)))
        out_shape.append(jax.ShapeDtypeStruct(xc.shape, F32))
    n_cast = 0
    if next_w is not None:
        stacked, nl, nh = next_w
        n_cast = len(stacked)
        for w in stacked:
            rows, cols = w.shape[2:]
            slab = next(s for s in range(16, rows + 1, 16) if rows % s == 0 and rows // s <= n_blocks)
            slab_of = lambda i, n_slabs=rows // slab: jnp.minimum(i, n_slabs - 1)
            in_specs.append(pl.BlockSpec((None, None, slab, cols),
                                         lambda i, slab_of=slab_of: (nl, nh, slab_of(i), 0)))
            args.append(w)
            out_specs.append(pl.BlockSpec((slab, cols), lambda i, slab_of=slab_of: (slab_of(i), 0)))
            out_shape.append(jax.ShapeDtypeStruct((rows, cols), BF16))
    if final:
        in_specs.append(_sel_spec((), (1, D_MODEL)))
        args.append(final_g.reshape(1, D_MODEL))
    outs = pl.pallas_call(
        functools.partial(_ffn_kernel, mod0=6 * half, final=final, n_cast=n_cast, ctx=ctx),
        grid=(n_blocks,),
        in_specs=in_specs,
        out_specs=out_specs,
        out_shape=out_shape,
        scratch_shapes=[pltpu.VMEM((2, tm, D_MODEL), BF16)],
        compiler_params=_params(1),
        name="ffn_half",
    )(*args)
    n_main = 2 if ctx else 1
    return outs[:n_main], tuple(outs[n_main:])


def _ctx_spec(xc):
    return pl.BlockSpec(xc.shape, lambda i: (0, 0))


def _on_last_step(fn):
    pl.when(pl.program_id(0) == pl.num_programs(0) - 1)(fn)


def _conv_glu_block(x, mod_ref, g_ref, w1b_ref, b1_ref, u_ref):
    h = _rms_mod(x, g_ref[...], mod_ref[3:4, :], mod_ref[4:5, :]).astype(BF16)
    a = _dot(h, w1b_ref[:, :D_MODEL]) + b1_ref[:, :D_MODEL]
    gt = _dot(h, w1b_ref[:, D_MODEL:]) + b1_ref[:, D_MODEL:]
    u_ref[...] = a * jax.nn.sigmoid(gt)


def _conv_glu_kernel(x_ref, mod_ref, g_ref, w1_ref, b1_ref, *rest, ctx):
    w1b_ref = rest[-1]
    _cast_once(w1_ref, w1b_ref)
    u_ref = rest[2] if ctx else rest[0]
    _conv_glu_block(x_ref[...], mod_ref, g_ref, w1b_ref, b1_ref, u_ref)
    if ctx:
        xc_ref, modc_ref, _, uc_ref = rest[:4]
        _on_last_step(lambda: _conv_glu_block(xc_ref[...], modc_ref, g_ref, w1b_ref, b1_ref, uc_ref))


def _conv_glu(x, mods, norm_g, w1, b1, layer, j, tm, xc=None):
    t = x.shape[0]
    ctx = xc is not None
    in_specs = [
        _tok_spec(tm),
        _sel_spec((layer, 0), (N_MOD, D_MODEL)),
        _sel_spec((layer, 1), (1, D_MODEL)),
        _sel_spec((j,), (D_MODEL, 2 * D_MODEL)),
        _sel_spec((j,), (1, 2 * D_MODEL)),
    ]
    args = [x, mods, norm_g, w1, b1]
    out_specs = [_tok_spec(tm)]
    out_shape = [jax.ShapeDtypeStruct((t, D_MODEL), F32)]
    if ctx:
        in_specs += [_sel_spec((), xc.shape), _sel_spec((layer, 1), (N_MOD, D_MODEL))]
        args += [xc, mods]
        out_specs.append(_ctx_spec(xc))
        out_shape.append(jax.ShapeDtypeStruct(xc.shape, F32))
    return pl.pallas_call(
        functools.partial(_conv_glu_kernel, ctx=ctx),
        grid=(t // tm,),
        in_specs=in_specs,
        out_specs=out_specs,
        out_shape=out_shape,
        scratch_shapes=[pltpu.VMEM((D_MODEL, 2 * D_MODEL), BF16)],
        compiler_params=_params(1),
        name="conv_glu",
    )(*args)


def _conv_tail_kernel(ul_ref, um_ref, ur_ref, x_ref, mod_ref, wdw_ref, bdw_ref, lng_ref, lnb_ref,
                      w2_ref, b2_ref, o_ref, win_ref, cv_ref, w2b_ref, *, tm):
    _cast_once(w2_ref, w2b_ref)
    i = pl.program_id(0)
    last = pl.num_programs(0) - 1
    n_slabs = D_MODEL // V7X_LANES
    for c in range(n_slabs):
        cs = slice(c * V7X_LANES, (c + 1) * V7X_LANES)
        win_ref[c, 0:CONV_HALO, :] = jnp.where(i > 0, ul_ref[:, cs], 0.0)
        win_ref[c, CONV_HALO:CONV_HALO + tm, :] = um_ref[:, cs]
        win_ref[c, CONV_HALO + tm:, :] = jnp.where(i < last, ur_ref[:, cs], 0.0)
    off0 = CONV_HALO - CONV_WIDTH // 2
    group = V7X_SUBLANES * CONV_STRIDE
    for c in range(n_slabs):
        cs = slice(c * V7X_LANES, (c + 1) * V7X_LANES)
        strided = lambda start: pl.ds(start, V7X_SUBLANES, stride=CONV_STRIDE)
        for k0 in range(0, CONV_WIDTH, CONV_TAPS_PER_PASS):
            taps = [wdw_ref[k:k + 1, cs] for k in range(k0, min(k0 + CONV_TAPS_PER_PASS, CONV_WIDTH))]
            for g in range(tm // group):
                v = [win_ref[c, strided(g * group + off0 + k0 + m), :]
                     for m in range(len(taps) + CONV_STRIDE - 1)]
                for j in range(CONV_STRIDE):
                    acc = taps[0] * v[j]
                    for k in range(1, len(taps)):
                        acc = acc + taps[k] * v[j + k]
                    if k0 > 0:
                        acc = acc + cv_ref[c, strided(g * group + j), :]
                    cv_ref[c, strided(g * group + j), :] = acc
    cv = jnp.concatenate([cv_ref[c] for c in range(n_slabs)], axis=1) + bdw_ref[...]
    mu = jnp.mean(cv, axis=-1, keepdims=True)
    cen = cv - mu
    var = jnp.mean(cen * cen, axis=-1, keepdims=True)
    y = cen * lax.rsqrt(var + LN_EPS) * lng_ref[...] + lnb_ref[...]
    out = _dot(jax.nn.silu(y).astype(BF16), w2b_ref[...]) + b2_ref[...]
    o_ref[...] = x_ref[...] + mod_ref[5:6, :] * out


def _conv_tail(u, x, mods, wdw, bdw, lng, lnb, w2, b2, layer, stream, j, tm):
    t = x.shape[0]
    hb = tm // CONV_HALO
    n_halo_blocks = t // CONV_HALO
    vec = _sel_spec((j,), (1, D_MODEL))
    return pl.pallas_call(
        functools.partial(_conv_tail_kernel, tm=tm),
        grid=(t // tm,),
        in_specs=[
            pl.BlockSpec((CONV_HALO, D_MODEL), lambda i: (jnp.maximum(i * hb - 1, 0), 0)),
            _tok_spec(tm),
            pl.BlockSpec((CONV_HALO, D_MODEL),
                         lambda i: (jnp.minimum((i + 1) * hb, n_halo_blocks - 1), 0)),
            _tok_spec(tm),
            _sel_spec((layer, stream), (N_MOD, D_MODEL)),
            _sel_spec((j,), (CONV_WIDTH, D_MODEL)),
            vec, vec, vec,
            _sel_spec((j,), (D_MODEL, D_MODEL)),
            vec,
        ],
        out_specs=_tok_spec(tm),
        out_shape=jax.ShapeDtypeStruct((t, D_MODEL), F32),
        scratch_shapes=[
            pltpu.VMEM((D_MODEL // V7X_LANES, tm + 2 * CONV_HALO, V7X_LANES), F32),
            pltpu.VMEM((D_MODEL // V7X_LANES, tm, V7X_LANES), F32),
            pltpu.VMEM((D_MODEL, D_MODEL), BF16),
        ],
        compiler_params=_params(1),
        name="conv_tail",
    )(u, u, u, x, mods, wdw, bdw, lng, lnb, w2, b2)


def _qkv_block(x, mod_ref, g_ref, wb_ref, b_ref, q_ref, k_ref, v_ref):
    h = _rms_mod(x, g_ref[...], mod_ref[3:4, :], mod_ref[4:5, :]).astype(BF16)
    d = D_MODEL
    q = _dot(h, wb_ref[:, 0:d]) + b_ref[:, 0:d]
    q_ref[...] = (q * SCORE_SCALE).astype(BF16)
    k_ref[...] = _dot(h, wb_ref[:, d:2 * d]) + b_ref[:, d:2 * d]
    v_ref[...] = _dot(h, wb_ref[:, 2 * d:3 * d]) + b_ref[:, 2 * d:3 * d]


def _qkv_kernel(x_ref, mod_ref, g_ref, w_ref, b_ref, xc_ref, modc_ref, q_ref, k_ref, v_ref,
                qc_ref, kc_ref, vc_ref, wb_ref):
    _cast_once(w_ref, wb_ref)
    _qkv_block(x_ref[...], mod_ref, g_ref, wb_ref, b_ref, q_ref, k_ref, v_ref)
    _on_last_step(lambda: _qkv_block(xc_ref[...], modc_ref, g_ref, wb_ref, b_ref, qc_ref, kc_ref, vc_ref))


def _qkv_proj(x, xc, mods, norm_g, w, b, layer, j, tm):
    t, tc = x.shape[0], xc.shape[0]
    shapes = lambda n: [jax.ShapeDtypeStruct((n, D_MODEL), BF16), jax.ShapeDtypeStruct((n, D_MODEL), F32),
                        jax.ShapeDtypeStruct((n, D_MODEL), F32)]
    outs = pl.pallas_call(
        _qkv_kernel,
        grid=(t // tm,),
        in_specs=[
            _tok_spec(tm),
            _sel_spec((layer, 0), (N_MOD, D_MODEL)),
            _sel_spec((layer, 1), (1, D_MODEL)),
            _sel_spec((j,), (D_MODEL, 3 * D_MODEL)),
            _sel_spec((j,), (1, 3 * D_MODEL)),
            _sel_spec((), xc.shape),
            _sel_spec((layer, 1), (N_MOD, D_MODEL)),
        ],
        out_specs=[_tok_spec(tm)] * 3 + [_ctx_spec(xc)] * 3,
        out_shape=shapes(t) + shapes(tc),
        scratch_shapes=[pltpu.VMEM((D_MODEL, 3 * D_MODEL), BF16)],
        compiler_params=_params(1),
        name="qkv_proj",
    )(x, mods, norm_g, w, b, xc, mods)
    return tuple(outs[:3]), tuple(outs[3:])


def _head_lane_masks():
    lane = lax.broadcasted_iota(jnp.int32, (1, V7X_MXU), 1)
    return [(lane >= h * HEAD_DIM) & (lane < (h + 1) * HEAD_DIM) for h in range(NA_GROUP)]


def _stack_heads(q, masks):
    return jnp.concatenate([q * m.astype(q.dtype) for m in masks], axis=0)


def _softmax_pv(scores, values):
    mx = functools.reduce(jnp.maximum, [jnp.max(s, axis=-1, keepdims=True) for s in scores])
    ps = [jnp.exp2(s - mx) for s in scores]
    denom = functools.reduce(jnp.add, [jnp.sum(p, axis=-1, keepdims=True) for p in ps])
    o = functools.reduce(jnp.add, [_dot(p.astype(BF16), v) for p, v in zip(ps, values)])
    return o / denom


def _unstack_heads(o, masks, nq):
    out = o[(NA_GROUP - 1) * nq:]
    for h in range(NA_GROUP - 2, -1, -1):
        out = jnp.where(masks[h], o[h * nq:(h + 1) * nq], out)
    return out


def _build_bias_tables(rpb_ref, colmask_ref, rowmask_ref, tab_ref):
    n_rows = 2 * WIN_R - 1
    front = NA_Q_ROWS - WIN_R + NA_HALO_ROWS
    back = NA_Q_ROWS + NA_KEY_ROWS - front - n_rows
    masked = jnp.full((WIN_C, NA_SPAN), MASK_VALUE, F32)
    top = (NA_Q_ROWS - 1) * NA_SPAN
    for n in range(GRID_W // WIN_C):
        c0 = min(max(n * WIN_C - WIN_C // 2, 0), GRID_W - NA_SPAN)
        lane0 = c0 - n * WIN_C + WIN_C - 1
        keep = colmask_ref[n] > 0.5
        for hh in range(NA_GROUP):
            pieces = []
            for d in range(n_rows):
                row = jnp.broadcast_to(rpb_ref[hh, d:d + 1, :] * LOG2_E, (WIN_C, V7X_LANES))
                rolled = pltpu.roll(row, (V7X_LANES - lane0) % V7X_LANES, 1, stride=1, stride_axis=0)
                pieces.append(jnp.where(keep, rolled[:, :NA_SPAN], MASK_VALUE))
            compact = jnp.concatenate([masked] * front + pieces + [masked] * back, axis=1)
            for rho in range(NA_Q_ROWS):
                strip = compact[:, top - rho * NA_SPAN:top - rho * NA_SPAN + NA_NK]
                r0 = hh * NA_NQ + rho * WIN_C
                for cls in range(3):
                    tab_ref[cls, n, r0:r0 + WIN_C, :] = jnp.where(
                        rowmask_ref[cls, rho] > 0.5, strip, MASK_VALUE)


def _na_kernel(q_ref, kt_ref, km_ref, kb_ref, vt_ref, vm_ref, vb_ref, kc_ref, vc_ref, rpb_ref, colmask_ref,
               rowmask_ref, o_ref, tab_ref):
    b = pl.program_id(1)
    last = pl.num_programs(1) - 1

    @pl.when(b == 0)
    def _():
        _build_bias_tables(rpb_ref, colmask_ref, rowmask_ref, tab_ref)

    masks = _head_lane_masks()
    step_rows = NA_STEP_BLOCKS * NA_Q_ROWS
    q3 = q_ref[...].reshape(step_rows, GRID_W, V7X_MXU)
    kw = jnp.concatenate([kt_ref[...], km_ref[...], kb_ref[...]], axis=0)
    vw = jnp.concatenate([vt_ref[...], vm_ref[...], vb_ref[...]], axis=0)
    kw3 = kw.reshape(step_rows + 2 * NA_HALO_ROWS, GRID_W, V7X_MXU)
    vw3 = vw.reshape(step_rows + 2 * NA_HALO_ROWS, GRID_W, V7X_MXU)
    col0 = [min(max(n * WIN_C - WIN_C // 2, 0), GRID_W - NA_SPAN) for n in range(GRID_W // WIN_C)]
    k_span = [kw3[:, c:c + NA_SPAN, :].astype(BF16) for c in col0]
    v_span = [vw3[:, c:c + NA_SPAN, :].astype(BF16) for c in col0]
    kc = kc_ref[...].astype(BF16)
    vc = vc_ref[...].astype(BF16)
    for sb in range(NA_STEP_BLOCKS):
        cls = 1
        if sb == 0:
            cls = jnp.where(b == 0, 0, cls)
        if sb == NA_STEP_BLOCKS - 1:
            cls = jnp.where(b == last, 2, cls)
        r_lo = sb * NA_Q_ROWS
        for n in range(GRID_W // WIN_C):
            qn = q3[r_lo:r_lo + NA_Q_ROWS, n * WIN_C:(n + 1) * WIN_C, :].reshape(NA_NQ, V7X_MXU)
            kn = k_span[n][r_lo:r_lo + NA_KEY_ROWS].reshape(NA_NK, V7X_MXU)
            vn = v_span[n][r_lo:r_lo + NA_KEY_ROWS].reshape(NA_NK, V7X_MXU)
            qs = _stack_heads(qn, masks)
            s_lat = _dot_nt(qs, kn) + tab_ref[cls, n]
            s_ctx = _dot_nt(qs, kc)
            o = _unstack_heads(_softmax_pv([s_lat, s_ctx], [vn, vc]), masks, NA_NQ)
            o3 = o.astype(BF16).reshape(NA_Q_ROWS, WIN_C, V7X_MXU)
            for r in range(NA_Q_ROWS):
                row0 = (r_lo + r) * GRID_W + n * WIN_C
                o_ref[row0:row0 + WIN_C, :] = o3[r]


def _na_attention(q, k, v, kc, vc, rpb, colmask, rowmask, j):
    t = q.shape[0]
    tq = NA_STEP_BLOCKS * NA_Q_ROWS * GRID_W
    th = NA_HALO_ROWS * GRID_W
    nb = t // tq
    n_halo_blocks = t // th
    hpb = tq // th
    ncb = GRID_W // WIN_C
    main = lambda: pl.BlockSpec((tq, V7X_MXU), lambda g, b: (b, g))
    top = lambda: pl.BlockSpec((th, V7X_MXU), lambda g, b: (jnp.maximum(b * hpb - 1, 0), g))
    bot = lambda: pl.BlockSpec(
        (th, V7X_MXU), lambda g, b: (jnp.minimum((b + 1) * hpb, n_halo_blocks - 1), g))
    ctx = lambda: pl.BlockSpec((kc.shape[0], V7X_MXU), lambda g, b: (0, g))
    return pl.pallas_call(
        _na_kernel,
        grid=(N_HEADS // NA_GROUP, nb),
        in_specs=[main(), top(), main(), bot(), top(), main(), bot(), ctx(), ctx(),
                  pl.BlockSpec((None, NA_GROUP) + rpb.shape[2:], lambda g, b: (j, g, 0, 0)),
                  pl.BlockSpec(colmask.shape, lambda g, b: (0, 0, 0)),
                  pl.BlockSpec(rowmask.shape, lambda g, b: (0, 0, 0, 0))],
        out_specs=main(),
        out_shape=jax.ShapeDtypeStruct((t, D_MODEL), BF16),
        scratch_shapes=[pltpu.VMEM((3, ncb, NA_GROUP * NA_NQ, NA_NK), F32)],
        compiler_params=_params(2),
        name="neighbourhood_attention",
    )(q, k, k, k, v, v, v, kc, vc, rpb, colmask, rowmask)


def _ctx_attn_kernel(q_ref, k_ref, v_ref, o_ref):
    masks = _head_lane_masks()
    nq = q_ref.shape[0]
    qs = _stack_heads(q_ref[...], masks)
    k = k_ref[...].astype(BF16)
    v = v_ref[...].astype(BF16)
    o_ref[...] = _unstack_heads(_softmax_pv([_dot_nt(qs, k)], [v]), masks, nq).astype(BF16)


def _ctx_attention(q, k, v):
    t = q.shape[0]
    blk = pl.BlockSpec((t, V7X_MXU), lambda g: (0, g))
    return pl.pallas_call(
        _ctx_attn_kernel,
        grid=(N_HEADS // NA_GROUP,),
        in_specs=[blk, blk, blk],
        out_specs=blk,
        out_shape=jax.ShapeDtypeStruct((t, D_MODEL), BF16),
        compiler_params=_params(1),
        name="context_attention",
    )(q, k, v)


def _out_proj_block(o, x, mod_ref, wb_ref, b_ref, y_ref):
    y_ref[...] = x + mod_ref[5:6, :] * (_dot(o, wb_ref[...]) + b_ref[...])


def _out_proj_kernel(o_ref, x_ref, mod_ref, w_ref, b_ref, *rest, ctx):
    wb_ref = rest[-1]
    _cast_once(w_ref, wb_ref)
    y_ref = rest[3] if ctx else rest[0]
    _out_proj_block(o_ref[...], x_ref[...], mod_ref, wb_ref, b_ref, y_ref)
    if ctx:
        oc_ref, xc_ref, modc_ref, _, yc_ref = rest[:5]
        _on_last_step(lambda: _out_proj_block(oc_ref[...], xc_ref[...], modc_ref, wb_ref, b_ref, yc_ref))


def _out_proj(o, x, mods, w, b, layer, j, tm, ctx_streams=None):
    t = x.shape[0]
    ctx = ctx_streams is not None
    in_specs = [_tok_spec(tm), _tok_spec(tm),
                _sel_spec((layer, 0), (N_MOD, D_MODEL)),
                _sel_spec((j,), (D_MODEL, D_MODEL)),
                _sel_spec((j,), (1, D_MODEL))]
    args = [o, x, mods, w, b]
    out_specs = [_tok_spec(tm)]
    out_shape = [jax.ShapeDtypeStruct((t, D_MODEL), F32)]
    if ctx:
        oc, xc = ctx_streams
        in_specs += [_sel_spec((), oc.shape), _sel_spec((), xc.shape), _sel_spec((layer, 1), (N_MOD, D_MODEL))]
        args += [oc, xc, mods]
        out_specs.append(_ctx_spec(xc))
        out_shape.append(jax.ShapeDtypeStruct(xc.shape, F32))
    return pl.pallas_call(
        functools.partial(_out_proj_kernel, ctx=ctx),
        grid=(t // tm,),
        in_specs=in_specs,
        out_specs=out_specs,
        out_shape=out_shape,
        scratch_shapes=[pltpu.VMEM((D_MODEL, D_MODEL), BF16)],
        compiler_params=_params(1),
        name="attn_out_proj",
    )(*args)


def _na_index_tables(rows):
    ncb = GRID_W // WIN_C
    n = np.arange(ncb)
    c0 = np.clip(n * WIN_C - WIN_C // 2, 0, GRID_W - NA_SPAN)
    q_col = n[:, None] * WIN_C + np.arange(WIN_C)[None, :]
    k_col = c0[:, None] + np.arange(NA_SPAN)[None, :]
    q_start = np.clip(q_col - WIN_C // 2, 0, GRID_W - WIN_C)
    col_valid = ((k_col[:, None, :] >= q_start[:, :, None])
                 & (k_col[:, None, :] < q_start[:, :, None] + WIN_C))
    wr = min(WIN_R, rows)
    rho = np.arange(NA_Q_ROWS)
    kappa = np.arange(NA_KEY_ROWS)
    row_valid = []
    for r_start in (0, NA_Q_ROWS, rows - NA_Q_ROWS):
        r = r_start + rho
        j = r_start - NA_HALO_ROWS + kappa
        r0 = np.clip(r - wr // 2, 0, rows - wr)
        row_valid.append((j[None, :] >= r0[:, None]) & (j[None, :] < r0[:, None] + wr))
    row_valid = np.repeat(np.stack(row_valid), NA_SPAN, axis=-1)
    rowmask = row_valid.astype(np.float32).reshape(3, NA_Q_ROWS, 1, NA_NK)
    return col_valid, rowmask


def kernel(x, c, ctx, c_ctx, mod_w, mod_b, norm_g, ffn_w_gate, ffn_w_up, ffn_w_down, conv_w_pw1, conv_b_pw1, conv_w_dw, conv_b_dw, conv_ln_g, conv_ln_b, conv_w_pw2, conv_b_pw2, na_w_qkv, na_b_qkv, na_rpb, na_w_o, na_b_o, final_g):
    b, t, d = x.shape
    assert b == 1 and d == D_MODEL and t % (NA_STEP_BLOCKS * NA_Q_ROWS * GRID_W) == 0
    n_ctx = ctx.shape[1]
    tm = {0: 512, 1: n_ctx}
    tm_mm = {0: 1024, 1: n_ctx}
    col_valid, rowmask = _na_index_tables(t // GRID_W)
    rowmask = jnp.asarray(rowmask)
    colmask = jnp.asarray(col_valid, F32)
    rpb = jnp.pad(na_rpb, ((0, 0),) * 3 + ((0, V7X_LANES - na_rpb.shape[-1]),))
    xs = {0: x.reshape(t, d), 1: ctx.reshape(n_ctx, d)}

    cc = jnp.zeros((V7X_SUBLANES, d), F32).at[0].set(c[0]).at[1].set(c_ctx)
    mods = _modulations(cc, mod_w, mod_b).reshape(DEPTH, V7X_SUBLANES, N_MOD, d)
    ng = norm_g.reshape(DEPTH, 3, 1, d)
    row = lambda v: v.reshape(v.shape[0], 1, v.shape[1])
    conv_vecs = tuple(row(v) for v in (conv_b_dw, conv_ln_g, conv_ln_b))
    ffn_w32 = (ffn_w_gate, ffn_w_up, ffn_w_down)
    w_cur = tuple(w[0, 0].astype(BF16) for w in ffn_w32)

    for i in range(DEPTH):
        mixer, j = i % 2, i // 2
        last = i == DEPTH - 1
        for half in (0, 1):
            if half == 1:
                if mixer == 0:
                    us = _conv_glu(xs[0], mods, ng, conv_w_pw1, row(conv_b_pw1), i, j, tm_mm[0],
                                   xc=None if last else xs[1])
                    for s, u in enumerate(us):
                        xs[s] = _conv_tail(u, xs[s], mods, conv_w_dw, *conv_vecs, conv_w_pw2,
                                           row(conv_b_pw2), i, s, j, tm[s])
                else:
                    qkv, qkv_c = _qkv_proj(xs[0], xs[1], mods, ng, na_w_qkv, row(na_b_qkv), i, j, tm_mm[0])
                    o = _na_attention(*qkv, qkv_c[1], qkv_c[2], rpb, colmask, rowmask, j)
                    ctx_streams = None if last else (_ctx_attention(*qkv_c), xs[1])
                    ys = _out_proj(o, xs[0], mods, na_w_o, row(na_b_o), i, j, tm_mm[0], ctx_streams)
                    for s, y in enumerate(ys):
                        xs[s] = y
            final = last and half == 1
            nxt = None if final else (ffn_w32, i + half, 1 - half)
            outs, w_next = _ffn_half(xs[0], mods, ng, w_cur, i, half, tm[0], xc=None if final else xs[1],
                                     final_g=final_g if final else None, next_w=nxt)
            xs[0] = outs[0]
            if not final:
                xs[1] = outs[1]
                w_cur = w_next
    return xs[0].reshape(b, t, d)
```

```python
import functools

import numpy as np
import jax
import jax.numpy as jnp
from jax import lax
from jax.experimental import pallas as pl
from jax.experimental.pallas import tpu as pltpu

D_MODEL = 1024
DEPTH = 4
GRID_W = 64
N_HEADS = 16
HEAD_DIM = D_MODEL // N_HEADS
D_FF = 2816
CONV_WIDTH = 31
WIN_R = 8
WIN_C = 16
N_MOD = 9
RMS_EPS = 1e-6
LN_EPS = 1e-5
MASK_VALUE = -1e30
LOG2_E = 1.4426950408889634
SCORE_SCALE = HEAD_DIM ** -0.5 * LOG2_E

F32 = jnp.float32
BF16 = jnp.bfloat16

V7X_LANES = 128
V7X_SUBLANES = 8
V7X_MXU = 256
FF_CHUNK = V7X_MXU
FFN_PREP_ROWS = 64
MOD_PER_STEP = 3
CONV_HALO = 16
CONV_STRIDE = 4
CONV_TAPS_PER_PASS = 16
NA_Q_ROWS = 8
NA_STEP_BLOCKS = 4
NA_HALO_ROWS = WIN_R // 2
NA_KEY_ROWS = NA_Q_ROWS + 2 * NA_HALO_ROWS
NA_SPAN = 2 * WIN_C
NA_GROUP = V7X_MXU // HEAD_DIM
NA_NQ = NA_Q_ROWS * WIN_C
NA_NK = NA_KEY_ROWS * NA_SPAN
VMEM_LIMIT = 58 * 1024 * 1024


def _sel_spec(prefix, tail):
    prefix, tail = tuple(prefix), tuple(tail)
    return pl.BlockSpec((None,) * len(prefix) + tail, lambda *_: prefix + (0,) * len(tail),
                        pipeline_mode=pl.Buffered(1))


def _tok_spec(tm):
    return pl.BlockSpec((tm, D_MODEL), lambda i: (i, 0))


def _params(n_grid_dims):
    return pltpu.CompilerParams(
        dimension_semantics=("arbitrary",) * n_grid_dims, vmem_limit_bytes=VMEM_LIMIT)


def _dot(a, b):
    return jnp.dot(a, b, preferred_element_type=F32)


def _dot_nt(a, b):
    return lax.dot_general(a, b, (((1,), (1,)), ((), ())), preferred_element_type=F32)


def _rmsnorm(x, g):
    return x * lax.rsqrt(jnp.mean(x * x, axis=-1, keepdims=True) + RMS_EPS) * g


def _rms_mod(x, g, shift, scale):
    return _rmsnorm(x, g) * (1.0 + scale) + shift


def _cast_once(w_ref, wb_ref):
    @pl.when(pl.program_id(0) == 0)
    def _():
        wb_ref[...] = w_ref[...].astype(BF16)


def _next_tok_spec(tm, n_blocks):
    return pl.BlockSpec((tm, D_MODEL), lambda i: (jnp.minimum(i + 1, n_blocks - 1), 0))


def _prime_norm(x_ref, h_ref, norm):
    i = pl.program_id(0)

    @pl.when(i == 0)
    def _():
        h_ref[0] = norm(x_ref[...])

    return i % 2


def _prep_next(xn_ref, h_ref, slot, k, n_pieces, norm):
    piece = xn_ref.shape[0] // n_pieces
    rows = slice(k * piece, (k + 1) * piece)
    h_ref[1 - slot, rows, :] = norm(xn_ref[rows, :])


def _mod_kernel(cc_ref, w_ref, b_ref, o_ref):
    s = jax.nn.silu(cc_ref[...]).astype(BF16)
    o_ref[...] = _dot(s, w_ref[...].astype(BF16)) + b_ref[...]


def _modulations(cc, mod_w, mod_b):
    rows = cc.shape[0]
    return pl.pallas_call(
        _mod_kernel,
        grid=(DEPTH, N_MOD // MOD_PER_STEP),
        in_specs=[
            pl.BlockSpec((rows, D_MODEL), lambda l, j: (0, 0)),
            pl.BlockSpec((None, D_MODEL, MOD_PER_STEP * D_MODEL), lambda l, j: (l, 0, j)),
            pl.BlockSpec((None, 1, MOD_PER_STEP * D_MODEL), lambda l, j: (l, 0, j)),
        ],
        out_specs=pl.BlockSpec((None, rows, MOD_PER_STEP * D_MODEL), lambda l, j: (l, 0, j)),
        out_shape=jax.ShapeDtypeStruct((DEPTH, rows, N_MOD * D_MODEL), F32),
        compiler_params=_params(2),
        name="modulations",
    )(cc, mod_w, mod_b.reshape(DEPTH, 1, N_MOD * D_MODEL))


def _swiglu(h, wg_ref, wu_ref, wd_ref, after_chunk=None):
    acc = None
    for f in range(D_FF // FF_CHUNK):
        sl = slice(f * FF_CHUNK, (f + 1) * FF_CHUNK)
        hf = h()
        a = jax.nn.silu(_dot(hf, wg_ref[:, sl])) * _dot(hf, wu_ref[:, sl])
        part = _dot(a.astype(BF16), wd_ref[sl, :])
        acc = part if acc is None else acc + part
        if after_chunk is not None:
            after_chunk(f)
    return acc


def _ffn_kernel(x_ref, xn_ref, mod_ref, g_ref, wg_ref, wu_ref, wd_ref, *rest, mod0, final, n_cast, ctx):
    if ctx:
        (xc_ref, modc_ref), rest = rest[:2], rest[2:]
    cast_in, rest = rest[:n_cast], rest[n_cast:]
    if final:
        final_ref, rest = rest[0], rest[1:]
    o_ref, rest = rest[0], rest[1:]
    if ctx:
        oc_ref, rest = rest[0], rest[1:]
    cast_out, h_ref = rest[:-1], rest[-1]
    mod_rows = lambda ref: (ref[mod0 + k:mod0 + k + 1, :] for k in range(3))
    shift, scale, gate = mod_rows(mod_ref)
    norm = lambda v: _rms_mod(v, g_ref[...], shift, scale).astype(BF16)
    slot = _prime_norm(x_ref, h_ref, norm)
    n_pieces = x_ref.shape[0] // FFN_PREP_ROWS

    def after_chunk(f):
        for piece in range(f, n_pieces, D_FF // FF_CHUNK):
            _prep_next(xn_ref, h_ref, slot, piece, n_pieces, norm)

    acc = _swiglu(lambda: h_ref[slot], wg_ref, wu_ref, wd_ref, after_chunk)
    out = x_ref[...] + 0.5 * gate * acc
    if final:
        out = _rmsnorm(out, final_ref[...])
    o_ref[...] = out
    for src, dst in zip(cast_in, cast_out):
        dst[...] = src[...].astype(BF16)

    if ctx:
        @pl.when(pl.program_id(0) == pl.num_programs(0) - 1)
        def _():
            shift_c, scale_c, gate_c = mod_rows(modc_ref)
            xc = xc_ref[...]
            hc = _rms_mod(xc, g_ref[...], shift_c, scale_c).astype(BF16)
            oc_ref[...] = xc + 0.5 * gate_c * _swiglu(lambda: hc, wg_ref, wu_ref, wd_ref)


def _ffn_half(x, mods, norm_g, w_bf16, layer, half, tm, xc=None, final_g=None, next_w=None):
    t = x.shape[0]
    final = final_g is not None
    ctx = xc is not None
    n_blocks = t // tm
    wg, wu, wd = w_bf16
    in_specs = [
        _tok_spec(tm),
        _next_tok_spec(tm, n_blocks),
        _sel_spec((layer, 0), (N_MOD, D_MODEL)),
        _sel_spec((layer, 2 * half), (1, D_MODEL)),
        _sel_spec((), (D_MODEL, D_FF)),
        _sel_spec((), (D_MODEL, D_FF)),
        _sel_spec((), (D_FF, D_MODEL)),
    ]
    args = [x, x, mods, norm_g, wg, wu, wd]
    out_specs = [_tok_spec(tm)]
    out_shape = [jax.ShapeDtypeStruct((t, D_MODEL), F32)]
    if ctx:
        in_specs += [_sel_spec((), xc.shape), _sel_spec((layer, 1), (N_MOD, D_MODEL))]
        args += [xc, mods]
        out_specs.append(pl.BlockSpec(xc.shape, lambda i: (0, 0)))
        out_shape.append(jax.ShapeDtypeStruct(xc.shape, F32))
    n_cast = 0
    if next_w is not None:
        stacked, nl, nh = next_w
        n_cast = len(stacked)
        for w in stacked:
            rows, cols = w.shape[2:]
            slab = next(s for s in range(16, rows + 1, 16) if rows % s == 0 and rows // s <= n_blocks)
            slab_of = lambda i, n_slabs=rows // slab: jnp.minimum(i, n_slabs - 1)
            in_specs.append(pl.BlockSpec((None, None, slab, cols),
                                         lambda i, slab_of=slab_of: (nl, nh, slab_of(i), 0)))
            args.append(w)
            out_specs.append(pl.BlockSpec((slab, cols), lambda i, slab_of=slab_of: (slab_of(i), 0)))
            out_shape.append(jax.ShapeDtypeStruct((rows, cols), BF16))
    if final:
        in_specs.append(_sel_spec((), (1, D_MODEL)))
        args.append(final_g.reshape(1, D_MODEL))
    outs = pl.pallas_call(
        functools.partial(_ffn_kernel, mod0=6 * half, final=final, n_cast=n_cast, ctx=ctx),
        grid=(n_blocks,),
        in_specs=in_specs,
        out_specs=out_specs,
        out_shape=out_shape,
        scratch_shapes=[pltpu.VMEM((2, tm, D_MODEL), BF16)],
        compiler_params=_params(1),
        name="ffn_half",
    )(*args)
    n_main = 2 if ctx else 1
    return outs[:n_main], tuple(outs[n_main:])


def _ctx_spec(xc):
    return pl.BlockSpec(xc.shape, lambda i: (0, 0))


def _on_last_step(fn):
    pl.when(pl.program_id(0) == pl.num_programs(0) - 1)(fn)


def _conv_glu_block(x, mod_ref, g_ref, w1b_ref, b1_ref, u_ref):
    h = _rms_mod(x, g_ref[...], mod_ref[3:4, :], mod_ref[4:5, :]).astype(BF16)
    a = _dot(h, w1b_ref[:, :D_MODEL]) + b1_ref[:, :D_MODEL]
    gt = _dot(h, w1b_ref[:, D_MODEL:]) + b1_ref[:, D_MODEL:]
    u_ref[...] = a * jax.nn.sigmoid(gt)


def _conv_glu_kernel(x_ref, mod_ref, g_ref, w1_ref, b1_ref, *rest, ctx):
    w1b_ref = rest[-1]
    _cast_once(w1_ref, w1b_ref)
    u_ref = rest[2] if ctx else rest[0]
    _conv_glu_block(x_ref[...], mod_ref, g_ref, w1b_ref, b1_ref, u_ref)
    if ctx:
        xc_ref, modc_ref, _, uc_ref = rest[:4]
        _on_last_step(lambda: _conv_glu_block(xc_ref[...], modc_ref, g_ref, w1b_ref, b1_ref, uc_ref))


def _conv_glu(x, mods, norm_g, w1, b1, layer, j, tm, xc=None):
    t = x.shape[0]
    ctx = xc is not None
    in_specs = [
        _tok_spec(tm),
        _sel_spec((layer, 0), (N_MOD, D_MODEL)),
        _sel_spec((layer, 1), (1, D_MODEL)),
        _sel_spec((j,), (D_MODEL, 2 * D_MODEL)),
        _sel_spec((j,), (1, 2 * D_MODEL)),
    ]
    args = [x, mods, norm_g, w1, b1]
    out_specs = [_tok_spec(tm)]
    out_shape = [jax.ShapeDtypeStruct((t, D_MODEL), F32)]
    if ctx:
        in_specs += [_sel_spec((), xc.shape), _sel_spec((layer, 1), (N_MOD, D_MODEL))]
        args += [xc, mods]
        out_specs.append(_ctx_spec(xc))
        out_shape.append(jax.ShapeDtypeStruct(xc.shape, F32))
    return pl.pallas_call(
        functools.partial(_conv_glu_kernel, ctx=ctx),
        grid=(t // tm,),
        in_specs=in_specs,
        out_specs=out_specs,
        out_shape=out_shape,
        scratch_shapes=[pltpu.VMEM((D_MODEL, 2 * D_MODEL), BF16)],
        compiler_params=_params(1),
        name="conv_glu",
    )(*args)


def _conv_tail_kernel(ul_ref, um_ref, ur_ref, x_ref, mod_ref, wdw_ref, bdw_ref, lng_ref, lnb_ref,
                      w2_ref, b2_ref, o_ref, win_ref, cv_ref, w2b_ref, *, tm):
    _cast_once(w2_ref, w2b_ref)
    i = pl.program_id(0)
    last = pl.num_programs(0) - 1
    n_slabs = D_MODEL // V7X_LANES
    for c in range(n_slabs):
        cs = slice(c * V7X_LANES, (c + 1) * V7X_LANES)
        win_ref[c, 0:CONV_HALO, :] = jnp.where(i > 0, ul_ref[:, cs], 0.0)
        win_ref[c, CONV_HALO:CONV_HALO + tm, :] = um_ref[:, cs]
        win_ref[c, CONV_HALO + tm:, :] = jnp.where(i < last, ur_ref[:, cs], 0.0)
    off0 = CONV_HALO - CONV_WIDTH // 2
    group = V7X_SUBLANES * CONV_STRIDE
    for c in range(n_slabs):
        cs = slice(c * V7X_LANES, (c + 1) * V7X_LANES)
        strided = lambda start: pl.ds(start, V7X_SUBLANES, stride=CONV_STRIDE)
        for k0 in range(0, CONV_WIDTH, CONV_TAPS_PER_PASS):
            taps = [wdw_ref[k:k + 1, cs] for k in range(k0, min(k0 + CONV_TAPS_PER_PASS, CONV_WIDTH))]
            for g in range(tm // group):
                v = [win_ref[c, strided(g * group + off0 + k0 + m), :]
                     for m in range(len(taps) + CONV_STRIDE - 1)]
                for j in range(CONV_STRIDE):
                    acc = taps[0] * v[j]
                    for k in range(1, len(taps)):
                        acc = acc + taps[k] * v[j + k]
                    if k0 > 0:
                        acc = acc + cv_ref[c, strided(g * group + j), :]
                    cv_ref[c, strided(g * group + j), :] = acc
    cv = jnp.concatenate([cv_ref[c] for c in range(n_slabs)], axis=1) + bdw_ref[...]
    mu = jnp.mean(cv, axis=-1, keepdims=True)
    cen = cv - mu
    var = jnp.mean(cen * cen, axis=-1, keepdims=True)
    y = cen * lax.rsqrt(var + LN_EPS) * lng_ref[...] + lnb_ref[...]
    out = _dot(jax.nn.silu(y).astype(BF16), w2b_ref[...]) + b2_ref[...]
    o_ref[...] = x_ref[...] + mod_ref[5:6, :] * out


def _conv_tail(u, x, mods, wdw, bdw, lng, lnb, w2, b2, layer, stream, j, tm):
    t = x.shape[0]
    hb = tm // CONV_HALO
    n_halo_blocks = t // CONV_HALO
    vec = _sel_spec((j,), (1, D_MODEL))
    return pl.pallas_call(
        functools.partial(_conv_tail_kernel, tm=tm),
        grid=(t // tm,),
        in_specs=[
            pl.BlockSpec((CONV_HALO, D_MODEL), lambda i: (jnp.maximum(i * hb - 1, 0), 0)),
            _tok_spec(tm),
            pl.BlockSpec((CONV_HALO, D_MODEL),
                         lambda i: (jnp.minimum((i + 1) * hb, n_halo_blocks - 1), 0)),
            _tok_spec(tm),
            _sel_spec((layer, stream), (N_MOD, D_MODEL)),
            _sel_spec((j,), (CONV_WIDTH, D_MODEL)),
            vec, vec, vec,
            _sel_spec((j,), (D_MODEL, D_MODEL)),
            vec,
        ],
        out_specs=_tok_spec(tm),
        out_shape=jax.ShapeDtypeStruct((t, D_MODEL), F32),
        scratch_shapes=[
            pltpu.VMEM((D_MODEL // V7X_LANES, tm + 2 * CONV_HALO, V7X_LANES), F32),
            pltpu.VMEM((D_MODEL // V7X_LANES, tm, V7X_LANES), F32),
            pltpu.VMEM((D_MODEL, D_MODEL), BF16),
        ],
        compiler_params=_params(1),
        name="conv_tail",
    )(u, u, u, x, mods, wdw, bdw, lng, lnb, w2, b2)


def _qkv_block(x, mod_ref, g_ref, wb_ref, b_ref, q_ref, k_ref, v_ref):
    h = _rms_mod(x, g_ref[...], mod_ref[3:4, :], mod_ref[4:5, :]).astype(BF16)
    d = D_MODEL
    q = _dot(h, wb_ref[:, 0:d]) + b_ref[:, 0:d]
    q_ref[...] = (q * SCORE_SCALE).astype(BF16)
    k_ref[...] = _dot(h, wb_ref[:, d:2 * d]) + b_ref[:, d:2 * d]
    v_ref[...] = _dot(h, wb_ref[:, 2 * d:3 * d]) + b_ref[:, 2 * d:3 * d]


def _qkv_kernel(x_ref, mod_ref, g_ref, w_ref, b_ref, xc_ref, modc_ref, q_ref, k_ref, v_ref,
                qc_ref, kc_ref, vc_ref, wb_ref):
    _cast_once(w_ref, wb_ref)
    _qkv_block(x_ref[...], mod_ref, g_ref, wb_ref, b_ref, q_ref, k_ref, v_ref)
    _on_last_step(lambda: _qkv_block(xc_ref[...], modc_ref, g_ref, wb_ref, b_ref, qc_ref, kc_ref, vc_ref))


def _qkv_proj(x, xc, mods, norm_g, w, b, layer, j, tm):
    t, tc = x.shape[0], xc.shape[0]
    shapes = lambda n: [jax.ShapeDtypeStruct((n, D_MODEL), BF16), jax.ShapeDtypeStruct((n, D_MODEL), F32),
                        jax.ShapeDtypeStruct((n, D_MODEL), F32)]
    outs = pl.pallas_call(
        _qkv_kernel,
        grid=(t // tm,),
        in_specs=[
            _tok_spec(tm),
            _sel_spec((layer, 0), (N_MOD, D_MODEL)),
            _sel_spec((layer, 1), (1, D_MODEL)),
            _sel_spec((j,), (D_MODEL, 3 * D_MODEL)),
            _sel_spec((j,), (1, 3 * D_MODEL)),
            _sel_spec((), xc.shape),
            _sel_spec((layer, 1), (N_MOD, D_MODEL)),
        ],
        out_specs=[_tok_spec(tm)] * 3 + [_ctx_spec(xc)] * 3,
        out_shape=shapes(t) + shapes(tc),
        scratch_shapes=[pltpu.VMEM((D_MODEL, 3 * D_MODEL), BF16)],
        compiler_params=_params(1),
        name="qkv_proj",
    )(x, mods, norm_g, w, b, xc, mods)
    return tuple(outs[:3]), tuple(outs[3:])


def _head_lane_masks():
    lane = lax.broadcasted_iota(jnp.int32, (1, V7X_MXU), 1)
    return [(lane >= h * HEAD_DIM) & (lane < (h + 1) * HEAD_DIM) for h in range(NA_GROUP)]


def _stack_heads(q, masks):
    return jnp.concatenate([q * m.astype(q.dtype) for m in masks], axis=0)


def _softmax_pv(scores, values):
    mx = functools.reduce(jnp.maximum, [jnp.max(s, axis=-1, keepdims=True) for s in scores])
    ps = [jnp.exp2(s - mx) for s in scores]
    denom = functools.reduce(jnp.add, [jnp.sum(p, axis=-1, keepdims=True) for p in ps])
    o = functools.reduce(jnp.add, [_dot(p.astype(BF16), v) for p, v in zip(ps, values)])
    return o / denom


def _unstack_heads(o, masks, nq):
    out = o[(NA_GROUP - 1) * nq:]
    for h in range(NA_GROUP - 2, -1, -1):
        out = jnp.where(masks[h], o[h * nq:(h + 1) * nq], out)
    return out


def _build_bias_tables(rpb_ref, colmask_ref, rowmask_ref, tab_ref):
    n_rows = 2 * WIN_R - 1
    front = NA_Q_ROWS - WIN_R + NA_HALO_ROWS
    back = NA_Q_ROWS + NA_KEY_ROWS - front - n_rows
    masked = jnp.full((WIN_C, NA_SPAN), MASK_VALUE, F32)
    top = (NA_Q_ROWS - 1) * NA_SPAN
    for n in range(GRID_W // WIN_C):
        c0 = min(max(n * WIN_C - WIN_C // 2, 0), GRID_W - NA_SPAN)
        lane0 = c0 - n * WIN_C + WIN_C - 1
        keep = colmask_ref[n] > 0.5
        for hh in range(NA_GROUP):
            pieces = []
            for d in range(n_rows):
                row = jnp.broadcast_to(rpb_ref[hh, d:d + 1, :] * LOG2_E, (WIN_C, V7X_LANES))
                rolled = pltpu.roll(row, (V7X_LANES - lane0) % V7X_LANES, 1, stride=1, stride_axis=0)
                pieces.append(jnp.where(keep, rolled[:, :NA_SPAN], MASK_VALUE))
            compact = jnp.concatenate([masked] * front + pieces + [masked] * back, axis=1)
            for rho in range(NA_Q_ROWS):
                strip = compact[:, top - rho * NA_SPAN:top - rho * NA_SPAN + NA_NK]
                r0 = hh * NA_NQ + rho * WIN_C
                for cls in range(3):
                    tab_ref[cls, n, r0:r0 + WIN_C, :] = jnp.where(
                        rowmask_ref[cls, rho] > 0.5, strip, MASK_VALUE)


def _na_kernel(q_ref, kt_ref, km_ref, kb_ref, vt_ref, vm_ref, vb_ref, kc_ref, vc_ref, rpb_ref, colmask_ref,
               rowmask_ref, o_ref, tab_ref):
    b = pl.program_id(1)
    last = pl.num_programs(1) - 1

    @pl.when(b == 0)
    def _():
        _build_bias_tables(rpb_ref, colmask_ref, rowmask_ref, tab_ref)

    masks = _head_lane_masks()
    step_rows = NA_STEP_BLOCKS * NA_Q_ROWS
    q3 = q_ref[...].reshape(step_rows, GRID_W, V7X_MXU)
    kw = jnp.concatenate([kt_ref[...], km_ref[...], kb_ref[...]], axis=0)
    vw = jnp.concatenate([vt_ref[...], vm_ref[...], vb_ref[...]], axis=0)
    kw3 = kw.reshape(step_rows + 2 * NA_HALO_ROWS, GRID_W, V7X_MXU)
    vw3 = vw.reshape(step_rows + 2 * NA_HALO_ROWS, GRID_W, V7X_MXU)
    col0 = [min(max(n * WIN_C - WIN_C // 2, 0), GRID_W - NA_SPAN) for n in range(GRID_W // WIN_C)]
    k_span = [kw3[:, c:c + NA_SPAN, :].astype(BF16) for c in col0]
    v_span = [vw3[:, c:c + NA_SPAN, :].astype(BF16) for c in col0]
    kc = kc_ref[...].astype(BF16)
    vc = vc_ref[...].astype(BF16)
    for sb in range(NA_STEP_BLOCKS):
        cls = 1
        if sb == 0:
            cls = jnp.where(b == 0, 0, cls)
        if sb == NA_STEP_BLOCKS - 1:
            cls = jnp.where(b == last, 2, cls)
        r_lo = sb * NA_Q_ROWS
        for n in range(GRID_W // WIN_C):
            qn = q3[r_lo:r_lo + NA_Q_ROWS, n * WIN_C:(n + 1) * WIN_C, :].reshape(NA_NQ, V7X_MXU)
            kn = k_span[n][r_lo:r_lo + NA_KEY_ROWS].reshape(NA_NK, V7X_MXU)
            vn = v_span[n][r_lo:r_lo + NA_KEY_ROWS].reshape(NA_NK, V7X_MXU)
            qs = _stack_heads(qn, masks)
            half = NA_GROUP * NA_NQ // 2
            parts = []
            for lo in range(0, NA_GROUP * NA_NQ, half):
                qh = qs[lo:lo + half]
                s_lat = _dot_nt(qh, kn) + tab_ref[cls, n, lo:lo + half, :]
                s_ctx = _dot_nt(qh, kc)
                parts.append(_softmax_pv([s_lat, s_ctx], [vn, vc]))
            o = _unstack_heads(jnp.concatenate(parts, axis=0), masks, NA_NQ)
            o3 = o.astype(BF16).reshape(NA_Q_ROWS, WIN_C, V7X_MXU)
            for r in range(NA_Q_ROWS):
                row0 = (r_lo + r) * GRID_W + n * WIN_C
                o_ref[row0:row0 + WIN_C, :] = o3[r]


def _na_attention(q, k, v, kc, vc, rpb, colmask, rowmask, j):
    t = q.shape[0]
    tq = NA_STEP_BLOCKS * NA_Q_ROWS * GRID_W
    th = NA_HALO_ROWS * GRID_W
    nb = t // tq
    n_halo_blocks = t // th
    hpb = tq // th
    ncb = GRID_W // WIN_C
    main = lambda: pl.BlockSpec((tq, V7X_MXU), lambda g, b: (b, g))
    top = lambda: pl.BlockSpec((th, V7X_MXU), lambda g, b: (jnp.maximum(b * hpb - 1, 0), g))
    bot = lambda: pl.BlockSpec(
        (th, V7X_MXU), lambda g, b: (jnp.minimum((b + 1) * hpb, n_halo_blocks - 1), g))
    ctx = lambda: pl.BlockSpec((kc.shape[0], V7X_MXU), lambda g, b: (0, g))
    return pl.pallas_call(
        _na_kernel,
        grid=(N_HEADS // NA_GROUP, nb),
        in_specs=[main(), top(), main(), bot(), top(), main(), bot(), ctx(), ctx(),
                  pl.BlockSpec((None, NA_GROUP) + rpb.shape[2:], lambda g, b: (j, g, 0, 0)),
                  pl.BlockSpec(colmask.shape, lambda g, b: (0, 0, 0)),
                  pl.BlockSpec(rowmask.shape, lambda g, b: (0, 0, 0, 0))],
        out_specs=main(),
        out_shape=jax.ShapeDtypeStruct((t, D_MODEL), BF16),
        scratch_shapes=[pltpu.VMEM((3, ncb, NA_GROUP * NA_NQ, NA_NK), F32)],
        compiler_params=_params(2),
        name="neighbourhood_attention",
    )(q, k, k, k, v, v, v, kc, vc, rpb, colmask, rowmask)


def _ctx_attn_kernel(q_ref, k_ref, v_ref, o_ref):
    masks = _head_lane_masks()
    nq = q_ref.shape[0]
    qs = _stack_heads(q_ref[...], masks)
    k = k_ref[...].astype(BF16)
    v = v_ref[...].astype(BF16)
    o_ref[...] = _unstack_heads(_softmax_pv([_dot_nt(qs, k)], [v]), masks, nq).astype(BF16)


def _ctx_attention(q, k, v):
    t = q.shape[0]
    blk = pl.BlockSpec((t, V7X_MXU), lambda g: (0, g))
    return pl.pallas_call(
        _ctx_attn_kernel,
        grid=(N_HEADS // NA_GROUP,),
        in_specs=[blk, blk, blk],
        out_specs=blk,
        out_shape=jax.ShapeDtypeStruct((t, D_MODEL), BF16),
        compiler_params=_params(1),
        name="context_attention",
    )(q, k, v)


def _out_proj_block(o, x, mod_ref, wb_ref, b_ref, y_ref):
    y_ref[...] = x + mod_ref[5:6, :] * (_dot(o, wb_ref[...]) + b_ref[...])


def _out_proj_kernel(o_ref, x_ref, mod_ref, w_ref, b_ref, *rest, ctx):
    wb_ref = rest[-1]
    _cast_once(w_ref, wb_ref)
    y_ref = rest[3] if ctx else rest[0]
    _out_proj_block(o_ref[...], x_ref[...], mod_ref, wb_ref, b_ref, y_ref)
    if ctx:
        oc_ref, xc_ref, modc_ref, _, yc_ref = rest[:5]
        _on_last_step(lambda: _out_proj_block(oc_ref[...], xc_ref[...], modc_ref, wb_ref, b_ref, yc_ref))


def _out_proj(o, x, mods, w, b, layer, j, tm, ctx_streams=None):
    t = x.shape[0]
    ctx = ctx_streams is not None
    in_specs = [_tok_spec(tm), _tok_spec(tm),
                _sel_spec((layer, 0), (N_MOD, D_MODEL)),
                _sel_spec((j,), (D_MODEL, D_MODEL)),
                _sel_spec((j,), (1, D_MODEL))]
    args = [o, x, mods, w, b]
    out_specs = [_tok_spec(tm)]
    out_shape = [jax.ShapeDtypeStruct((t, D_MODEL), F32)]
    if ctx:
        oc, xc = ctx_streams
        in_specs += [_sel_spec((), oc.shape), _sel_spec((), xc.shape), _sel_spec((layer, 1), (N_MOD, D_MODEL))]
        args += [oc, xc, mods]
        out_specs.append(_ctx_spec(xc))
        out_shape.append(jax.ShapeDtypeStruct(xc.shape, F32))
    return pl.pallas_call(
        functools.partial(_out_proj_kernel, ctx=ctx),
        grid=(t // tm,),
        in_specs=in_specs,
        out_specs=out_specs,
        out_shape=out_shape,
        scratch_shapes=[pltpu.VMEM((D_MODEL, D_MODEL), BF16)],
        compiler_params=_params(1),
        name="attn_out_proj",
    )(*args)


def _na_index_tables(rows):
    ncb = GRID_W // WIN_C
    n = np.arange(ncb)
    c0 = np.clip(n * WIN_C - WIN_C // 2, 0, GRID_W - NA_SPAN)
    q_col = n[:, None] * WIN_C + np.arange(WIN_C)[None, :]
    k_col = c0[:, None] + np.arange(NA_SPAN)[None, :]
    q_start = np.clip(q_col - WIN_C // 2, 0, GRID_W - WIN_C)
    col_valid = ((k_col[:, None, :] >= q_start[:, :, None])
                 & (k_col[:, None, :] < q_start[:, :, None] + WIN_C))
    wr = min(WIN_R, rows)
    rho = np.arange(NA_Q_ROWS)
    kappa = np.arange(NA_KEY_ROWS)
    row_valid = []
    for r_start in (0, NA_Q_ROWS, rows - NA_Q_ROWS):
        r = r_start + rho
        j = r_start - NA_HALO_ROWS + kappa
        r0 = np.clip(r - wr // 2, 0, rows - wr)
        row_valid.append((j[None, :] >= r0[:, None]) & (j[None, :] < r0[:, None] + wr))
    row_valid = np.repeat(np.stack(row_valid), NA_SPAN, axis=-1)
    rowmask = row_valid.astype(np.float32).reshape(3, NA_Q_ROWS, 1, NA_NK)
    return col_valid, rowmask


def kernel(x, c, ctx, c_ctx, mod_w, mod_b, norm_g, ffn_w_gate, ffn_w_up, ffn_w_down, conv_w_pw1, conv_b_pw1, conv_w_dw, conv_b_dw, conv_ln_g, conv_ln_b, conv_w_pw2, conv_b_pw2, na_w_qkv, na_b_qkv, na_rpb, na_w_o, na_b_o, final_g):
    b, t, d = x.shape
    assert b == 1 and d == D_MODEL and t % (NA_STEP_BLOCKS * NA_Q_ROWS * GRID_W) == 0
    n_ctx = ctx.shape[1]
    tm = {0: 512, 1: n_ctx}
    tm_mm = {0: 1024, 1: n_ctx}
    col_valid, rowmask = _na_index_tables(t // GRID_W)
    rowmask = jnp.asarray(rowmask)
    colmask = jnp.asarray(col_valid, F32)
    rpb = jnp.pad(na_rpb, ((0, 0),) * 3 + ((0, V7X_LANES - na_rpb.shape[-1]),))
    xs = {0: x.reshape(t, d), 1: ctx.reshape(n_ctx, d)}

    cc = jnp.zeros((V7X_SUBLANES, d), F32).at[0].set(c[0]).at[1].set(c_ctx)
    mods = _modulations(cc, mod_w, mod_b).reshape(DEPTH, V7X_SUBLANES, N_MOD, d)
    ng = norm_g.reshape(DEPTH, 3, 1, d)
    row = lambda v: v.reshape(v.shape[0], 1, v.shape[1])
    conv_vecs = tuple(row(v) for v in (conv_b_dw, conv_ln_g, conv_ln_b))
    ffn_w32 = (ffn_w_gate, ffn_w_up, ffn_w_down)
    w_cur = tuple(w[0, 0].astype(BF16) for w in ffn_w32)

    for i in range(DEPTH):
        mixer, j = i % 2, i // 2
        last = i == DEPTH - 1
        for half in (0, 1):
            if half == 1:
                if mixer == 0:
                    us = _conv_glu(xs[0], mods, ng, conv_w_pw1, row(conv_b_pw1), i, j, tm_mm[0],
                                   xc=None if last else xs[1])
                    for s, u in enumerate(us):
                        xs[s] = _conv_tail(u, xs[s], mods, conv_w_dw, *conv_vecs, conv_w_pw2,
                                           row(conv_b_pw2), i, s, j, tm[s])
                else:
                    qkv, qkv_c = _qkv_proj(xs[0], xs[1], mods, ng, na_w_qkv, row(na_b_qkv), i, j, tm_mm[0])
                    o = _na_attention(*qkv, qkv_c[1], qkv_c[2], rpb, colmask, rowmask, j)
                    ctx_streams = None if last else (_ctx_attention(*qkv_c), xs[1])
                    ys = _out_proj(o, xs[0], mods, na_w_o, row(na_b_o), i, j, tm_mm[0], ctx_streams)
                    for s, y in enumerate(ys):
                        xs[s] = y
            final = last and half == 1
            nxt = None if final else (ffn_w32, i + half, 1 - half)
            outs, w_next = _ffn_half(xs[0], mods, ng, w_cur, i, half, tm[0], xc=None if final else xs[1],
                                     final_g=final_g if final else None, next_w=nxt)
            xs[0] = outs[0]
            if not final:
                xs[1] = outs[1]
                w_cur = w_next
    return xs[0].reshape(b, t, d)
```
